```python
import math
import jax
import jax.numpy as jnp
from jax import lax
import numpy as np

D_MODEL = 2048
BATCH = 1
SEQ = 8192
DEPTH = 4

MIX_WIDTH = D_MODEL
SB_HEAD_DIM = 128
SB_WIDTH = MIX_WIDTH // 2
SB_HEADS = SB_WIDTH // SB_HEAD_DIM
RW_HEAD_DIM = 64
RW_WIDTH = MIX_WIDTH - SB_WIDTH
RW_HEADS = RW_WIDTH // RW_HEAD_DIM
LORA_W = 64
LORA_A = 64
LORA_G = 160
RW_SHIFT_COLS = 3 * RW_WIDTH + LORA_W + LORA_A + LORA_G
IN_COLS = 3 * SB_WIDTH + RW_SHIFT_COLS
Q_BLOCK = 128
PLE_DIM = 256
N_GROUPS = 4
EXPERTS_PER_GROUP = 8
N_EXPERTS = N_GROUPS * EXPERTS_PER_GROUP
TOP_K_IN_GROUP = 2
EXPERT_FF = 512
MOE_BLOCK = 128
RMS_EPS = 1e-6
GN_EPS = 64e-5

kernel_name = "hymba_sb_rwkv7_hmoe_trunk"


def rms_norm(x, g=None, eps=RMS_EPS):
    xf = x.astype(jnp.float32)
    y = xf * lax.rsqrt(jnp.mean(xf * xf, axis=-1, keepdims=True) + eps)
    if g is not None:
        y = y * g.astype(jnp.float32)
    return y.astype(x.dtype)


def token_shift(u, mu):
    prev = jnp.pad(u[:, :-1], ((0, 0), (1, 0), (0, 0)))
    return u + mu * (prev - u)


def stick_breaking_attention(q, k, v):
    B, H, S, hd = q.shape
    n_blocks = S // Q_BLOCK
    scale = 1.0 / math.sqrt(hd)
    kf = k.astype(jnp.float32)
    vf = v.astype(jnp.float32)
    q_blocks = q.astype(jnp.float32).reshape(B, H, n_blocks, Q_BLOCK, hd).transpose(2, 0, 1, 3, 4)
    key_pos = jnp.arange(S)

    def one_block(args):
        q_blk, b_idx = args
        z = jnp.einsum("bhqd,bhkd->bhqk", q_blk, kf) * scale
        q_pos = b_idx * Q_BLOCK + jnp.arange(Q_BLOCK)
        causal = key_pos[None, :] < q_pos[:, None]
        log_keep = jnp.where(causal, jax.nn.log_sigmoid(-z), 0.0)
        later = lax.cumsum(log_keep, axis=3, reverse=True) - log_keep
        weights = jnp.where(causal, jnp.exp(jax.nn.log_sigmoid(z) + later), 0.0)
        return jnp.einsum("bhqk,bhkd->bhqd", weights, vf)

    out = lax.map(one_block, (q_blocks, jnp.arange(n_blocks)))
    return out.transpose(1, 2, 0, 3, 4).reshape(B, H, S, hd)


def rwkv7_scan(r, decay, k, v, a_in, b_in):
    B, S, H, N = r.shape

    def step(state, inp):
        r_t, w_t, k_t, v_t, a_t, b_t = inp
        sa = jnp.einsum("bhvk,bhk->bhv", state, a_t)
        state = (state * w_t[:, :, None, :]
                 + sa[..., None] * b_t[:, :, None, :]
                 + v_t[..., None] * k_t[:, :, None, :])
        return state, jnp.einsum("bhvk,bhk->bhv", state, r_t)

    xs = tuple(t.transpose(1, 0, 2, 3) for t in (r, decay, k, v, a_in, b_in))
    state0 = jnp.zeros((B, H, N, N), jnp.float32)
    _, ys = lax.scan(step, state0, xs)
    return ys.transpose(1, 0, 2, 3)


def rwkv7_time_mix(u, w0, w2, a0, a2, g2, k_k, k_a, r_k, ln_g, ln_b):
    f32 = lambda t: t.astype(jnp.float32)
    uf = f32(u)
    B, S, _ = u.shape
    r = uf[..., :RW_WIDTH]
    k = uf[..., RW_WIDTH:2 * RW_WIDTH]
    v = uf[..., 2 * RW_WIDTH:3 * RW_WIDTH]
    o = 3 * RW_WIDTH
    w_lo = uf[..., o:o + LORA_W]
    a_lo = uf[..., o + LORA_W:o + LORA_W + LORA_A]
    g_lo = uf[..., o + LORA_W + LORA_A:]
    w = -jax.nn.softplus(-(f32(w0) + jnp.tanh(w_lo) @ f32(w2))) - 0.5
    decay = jnp.exp(-jnp.exp(w))
    a = jax.nn.sigmoid(f32(a0) + a_lo @ f32(a2))
    g = jax.nn.sigmoid(g_lo) @ f32(g2)
    heads = lambda t: t.reshape(B, S, RW_HEADS, RW_HEAD_DIM)
    kk = heads(k * f32(k_k))
    kk = kk * lax.rsqrt(jnp.sum(kk * kk, axis=-1, keepdims=True) + 1e-12)
    k = heads(k * (1.0 + (a - 1.0) * f32(k_a)))
    r, v, decay, a = heads(r), heads(v), heads(decay), heads(a)
    y = rwkv7_scan(r, decay, k, v, -kk, kk * a)
    mean = jnp.mean(y, axis=-1, keepdims=True)
    var = jnp.mean(jnp.square(y - mean), axis=-1, keepdims=True)
    y = ((y - mean) * lax.rsqrt(var + GN_EPS)).reshape(B, S, RW_WIDTH) * f32(ln_g) + f32(ln_b)
    bonus = jnp.sum(r * k * f32(r_k), axis=-1, keepdims=True) * v
    y = y + bonus.reshape(B, S, RW_WIDTH)
    return (y * g).astype(u.dtype)


def hier_moe(h, router_g, router_g_b, router_e, router_e_b, w_up, w_down):
    T, D = h.shape
    TK = T * TOP_K_IN_GROUP
    hf = h.astype(jnp.float32)
    g_prob = jax.nn.softmax(hf @ router_g.astype(jnp.float32) + router_g_b.astype(jnp.float32), axis=-1)
    g_w, grp = lax.top_k(g_prob, 1)
    e_logits = (hf @ router_e.astype(jnp.float32) + router_e_b.astype(jnp.float32)).reshape(T, N_GROUPS, EXPERTS_PER_GROUP)
    e_sel = e_logits[jnp.arange(T), grp[:, 0]]
    e_w, e_idx = lax.top_k(jax.nn.softmax(e_sel, axis=-1), TOP_K_IN_GROUP)
    e_w = e_w / jnp.sum(e_w, axis=-1, keepdims=True)
    weights = (g_w * e_w).reshape(-1)
    expert_id = (grp * EXPERTS_PER_GROUP + e_idx).reshape(-1)
    token_id = jnp.repeat(jnp.arange(T), TOP_K_IN_GROUP)
    order = jnp.argsort(expert_id)
    sorted_ids = expert_id[order]
    tok_sorted = token_id[order]
    w_sorted = weights[order]
    counts = jnp.bincount(expert_id, length=N_EXPERTS)
    starts = jnp.cumsum(counts) - counts
    padded = ((counts + MOE_BLOCK - 1) // MOE_BLOCK) * MOE_BLOCK
    pad_ends = jnp.cumsum(padded)
    pad_starts = pad_ends - padded
    dest = pad_starts[sorted_ids] + (jnp.arange(TK) - starts[sorted_ids])
    n_blocks = (TK + MOE_BLOCK - 1) // MOE_BLOCK + N_EXPERTS
    x_pad = jnp.zeros((n_blocks * MOE_BLOCK, D), h.dtype).at[dest].set(h[tok_sorted])
    blk_expert = jnp.minimum(jnp.searchsorted(pad_ends, jnp.arange(n_blocks) * MOE_BLOCK, side="right"), N_EXPERTS - 1)

    def expert_block(args):
        xb, e = args
        gate, up = jnp.split(xb @ w_up[e], 2, axis=-1)
        return (jax.nn.silu(gate) * up) @ w_down[e]

    y_pad = lax.map(expert_block, (x_pad.reshape(n_blocks, MOE_BLOCK, D), blk_expert)).reshape(-1, D)
    y = y_pad[dest] * w_sorted[:, None].astype(y_pad.dtype)
    return jax.ops.segment_sum(y, tok_sorted, num_segments=T)


def setup_inputs(seed: int = 0) -> dict:
    key = jax.random.key(seed)
    counter = [0]

    def nk():
        counter[0] += 1
        return jax.random.fold_in(key, counter[0])

    def normal(shape, scale):
        return jax.random.normal(nk(), shape, jnp.float32) * scale

    def gain(shape):
        return 1.0 + 0.05 * jax.random.normal(nk(), shape, jnp.float32)

    def uniform(shape, lo, hi):
        return jax.random.uniform(nk(), shape, jnp.float32, lo, hi)

    L, D = DEPTH, D_MODEL
    return {
        "x": normal((BATCH, SEQ, D), 1.0),
        "p": normal((L, BATCH, SEQ, PLE_DIM), 1.0),
        "ln1_g": gain((L, D)),
        "w_in": normal((L, D, IN_COLS), D ** -0.5),
        "sb_q_g": gain((L, SB_HEAD_DIM)),
        "sb_k_g": gain((L, SB_HEAD_DIM)),
        "sb_out_g": gain((L, SB_HEAD_DIM)),
        "rw_mu": uniform((L, RW_SHIFT_COLS), 0.0, 1.0),
        "rw_w0": uniform((L, RW_WIDTH), -5.0, 1.0),
        "rw_w2": normal((L, LORA_W, RW_WIDTH), 0.5 * LORA_W ** -0.5),
        "rw_a0": normal((L, RW_WIDTH), 0.5),
        "rw_a2": normal((L, LORA_A, RW_WIDTH), 0.5 * LORA_A ** -0.5),
        "rw_g2": normal((L, LORA_G, RW_WIDTH), LORA_G ** -0.5),
        "rw_k_k": 0.85 + normal((L, RW_WIDTH), 0.05),
        "rw_k_a": gain((L, RW_WIDTH)),
        "rw_r_k": normal((L, RW_HEADS, RW_HEAD_DIM), 0.1),
        "rw_ln_g": gain((L, RW_WIDTH)),
        "rw_ln_b": normal((L, RW_WIDTH), 0.01),
        "w_out": normal((L, D, D), D ** -0.5),
        "ln2_g": gain((L, D)),
        "router_g": normal((L, D, N_GROUPS), D ** -0.5),
        "router_g_b": normal((L, N_GROUPS), 0.01),
        "router_e": normal((L, D, N_EXPERTS), D ** -0.5),
        "router_e_b": normal((L, N_EXPERTS), 0.01),
        "w_up": normal((L, N_EXPERTS, D, 2 * EXPERT_FF), D ** -0.5),
        "w_down": normal((L, N_EXPERTS, EXPERT_FF, D), EXPERT_FF ** -0.5),
        "ple_proj": normal((L, PLE_DIM, D), PLE_DIM ** -0.5),
        "ple_gate": normal((L, D, D), D ** -0.5),
        "ple_norm_g": gain((L, D)),
    }


def reference(x, p, ln1_g, w_in, sb_q_g, sb_k_g, sb_out_g, rw_mu, rw_w0, rw_w2, rw_a0, rw_a2, rw_g2,
              rw_k_k, rw_k_a, rw_r_k, rw_ln_g, rw_ln_b, w_out, ln2_g, router_g, router_g_b, router_e,
              router_e_b, w_up, w_down, ple_proj, ple_gate, ple_norm_g):
    B, S, D = x.shape
    heads_sb = lambda t: t.reshape(B, S, SB_HEADS, SB_HEAD_DIM)
    for i in range(DEPTH):
        xn = rms_norm(x, ln1_g[i])
        proj = xn @ w_in[i]
        q = rms_norm(heads_sb(proj[..., :SB_WIDTH]), sb_q_g[i])
        k = rms_norm(heads_sb(proj[..., SB_WIDTH:2 * SB_WIDTH]), sb_k_g[i])
        v = heads_sb(proj[..., 2 * SB_WIDTH:3 * SB_WIDTH])
        sb = stick_breaking_attention(q.transpose(0, 2, 1, 3), k.transpose(0, 2, 1, 3), v.transpose(0, 2, 1, 3))
        sb = rms_norm(sb.transpose(0, 2, 1, 3), sb_out_g[i]).reshape(B, S, SB_WIDTH).astype(x.dtype)
        u = token_shift(proj[..., 3 * SB_WIDTH:], rw_mu[i])
        rw = rwkv7_time_mix(u, rw_w0[i], rw_w2[i], rw_a0[i], rw_a2[i], rw_g2[i], rw_k_k[i], rw_k_a[i],
                            rw_r_k[i], rw_ln_g[i], rw_ln_b[i])
        x = x + jnp.concatenate([sb, rw], axis=-1) @ w_out[i]
        hn = rms_norm(x, ln2_g[i])
        x = x + hier_moe(hn.reshape(B * S, D), router_g[i], router_g_b[i], router_e[i], router_e_b[i],
                         w_up[i], w_down[i]).reshape(B, S, D)
        gate = jax.nn.sigmoid(rms_norm(x) @ ple_gate[i])
        x = x + rms_norm((p[i] @ ple_proj[i]) * gate, ple_norm_g[i])
    return x
```

```python
import functools
import math

import jax
import jax.numpy as jnp
from jax import lax
from jax.experimental import pallas as pl
from jax.experimental.pallas import tpu as pltpu

F32 = jnp.float32
BF16 = jnp.bfloat16

SB_HEAD_DIM = 128
RW_HEAD_DIM = 64
LORA_W = 64
LORA_A = 64
LORA_G = 160
LORA_PAD = 512
N_GROUPS = 4
EXPERTS_PER_GROUP = 8
N_EXPERTS = N_GROUPS * EXPERTS_PER_GROUP
RMS_EPS = 1e-6
GN_EPS = 64e-5
LANES = 128
VMEM_LIMIT = 56 * 1024 * 1024

RW_CHUNK = 64
MOE_ROWS = 256


def _cparams(sem):
    return pltpu.CompilerParams(dimension_semantics=sem, vmem_limit_bytes=VMEM_LIMIT)


def _rms(x, eps=RMS_EPS):
    return x * lax.rsqrt(jnp.mean(x * x, axis=-1, keepdims=True) + eps)


def _softplus(y):
    return jnp.maximum(y, 0.0) + jnp.log1p(jnp.exp(-jnp.abs(y)))


def _mm(a, b):
    return jnp.dot(a.astype(BF16), b.astype(BF16), preferred_element_type=F32)


def _mm_nt(a, b):
    return lax.dot_general(a.astype(BF16), b.astype(BF16), (((1,), (1,)), ((), ())),
                           preferred_element_type=F32)


def _mm_tn(a, b):
    return lax.dot_general(a.astype(BF16), b.astype(BF16), (((0,), (0,)), ((), ())),
                           preferred_element_type=F32)


def _mm_f32(a, b):
    return jnp.dot(a, b, preferred_element_type=F32, precision=lax.Precision.HIGHEST)


def _mm_split(a, b_exact):
    hi = a.astype(BF16)
    lo = (a - hi.astype(F32)).astype(BF16)
    return (jnp.dot(hi, b_exact, preferred_element_type=F32)
            + jnp.dot(lo, b_exact, preferred_element_type=F32))


def _norm_matmul_kernel(x_ref, g_ref, w_ref, o_ref, xn_ref):
    @pl.when(pl.program_id(1) == 0)
    def _():
        xn_ref[...] = (_rms(x_ref[...]) * g_ref[...]).astype(BF16)

    o_ref[...] = jnp.dot(xn_ref[...], w_ref[...], preferred_element_type=F32).astype(o_ref.dtype)


def _norm_matmul(x, g, w, *, tm, tn, out_dtype, name):
    T, D = x.shape
    N = w.shape[1]
    return pl.pallas_call(
        _norm_matmul_kernel,
        grid=(T // tm, N // tn),
        in_specs=[
            pl.BlockSpec((tm, D), lambda i, j: (i, 0)),
            pl.BlockSpec((1, D), lambda i, j: (0, 0)),
            pl.BlockSpec((D, tn), lambda i, j: (0, j)),
        ],
        out_specs=pl.BlockSpec((tm, tn), lambda i, j: (i, j)),
        out_shape=jax.ShapeDtypeStruct((T, N), out_dtype),
        scratch_shapes=[pltpu.VMEM((tm, D), BF16)],
        compiler_params=_cparams(("parallel", "arbitrary")),
        name=name,
    )(x, g, w)


def _sb_attn_kernel(q_ref, k_ref, v_ref, qg_ref, kg_ref, og_ref, o_ref, kn_ref, vb_ref, *, tq, seq):
    i = pl.program_id(1)
    hd = SB_HEAD_DIM
    prep_rows = 512

    @pl.when(i == 0)
    def _():
        def body(c, carry):
            rows = pl.ds(pl.multiple_of(c * prep_rows, prep_rows), prep_rows)
            kn_ref[rows, :] = (_rms(k_ref[rows, :]) * kg_ref[...]).astype(BF16)
            vb_ref[rows, :] = v_ref[rows, :].astype(BF16)
            return carry

        lax.fori_loop(0, seq // prep_rows, body, 0)

    qn = (_rms(q_ref[...]) * qg_ref[...] * (1.0 / math.sqrt(hd))).astype(BF16)
    row = lax.broadcasted_iota(jnp.int32, (tq, tq), 0)
    col = lax.broadcasted_iota(jnp.int32, (tq, tq), 1)
    tri = (row > col).astype(BF16)
    causal = col < row

    def tile(j, carry, acc, masked):
        rows = pl.ds(pl.multiple_of(j * tq, tq), tq)
        z = _mm_nt(qn, kn_ref[rows, :])
        sp = _softplus(z)
        log_keep = -sp
        log_beta = z - sp
        if masked:
            log_keep = jnp.where(causal, log_keep, 0.0)
        later = _mm_split(log_keep, tri)
        w = jnp.exp(log_beta + later + carry)
        if masked:
            w = jnp.where(causal, w, 0.0)
        acc = acc + jnp.dot(w.astype(BF16), vb_ref[rows, :], preferred_element_type=F32)
        carry = carry + later[:, 0:1] + log_keep[:, 0:1]
        return carry, acc

    carry, acc = tile(i, jnp.zeros((tq, 1), F32), jnp.zeros((tq, hd), F32), True)
    carry, acc = lax.fori_loop(0, i, lambda n, ca: tile(i - 1 - n, ca[0], ca[1], False), (carry, acc))
    o_ref[...] = (_rms(acc) * og_ref[...]).astype(o_ref.dtype)


def _sb_attention(proj, q_g, k_g, out_g, *, n_heads, tq):
    T = proj.shape[0]
    hd = SB_HEAD_DIM
    gspec = pl.BlockSpec((1, hd), lambda h, i: (0, 0))
    return pl.pallas_call(
        functools.partial(_sb_attn_kernel, tq=tq, seq=T),
        grid=(n_heads, T // tq),
        in_specs=[
            pl.BlockSpec((tq, hd), lambda h, i: (i, h)),
            pl.BlockSpec((T, hd), lambda h, i: (0, n_heads + h)),
            pl.BlockSpec((T, hd), lambda h, i: (0, 2 * n_heads + h)),
            gspec, gspec, gspec,
        ],
        out_specs=pl.BlockSpec((tq, hd), lambda h, i: (i, h)),
        out_shape=jax.ShapeDtypeStruct((T, n_heads * hd), BF16),
        scratch_shapes=[pltpu.VMEM((T, hd), BF16), pltpu.VMEM((T, hd), BF16)],
        compiler_params=_cparams(("parallel", "arbitrary")),
        name="sb_attention",
    )(proj, proj, proj, q_g, k_g, out_g)


def _rw_prep_kernel(r_ref, k_ref, v_ref, lo_ref, rp_ref, kp_ref, vp_ref, lop_ref,
                    mu_r_ref, mu_k_ref, mu_v_ref, mu_lo_ref, w0_ref, w2_ref, a0_ref, a2_ref, g2_ref,
                    kk_ref, ka_ref,
                    r_out, lw_out, k_out, v_out, kk_out, a_out, g_out):
    first = pl.program_id(0) == 0

    def shift(cur_ref, prev_ref, mu_ref):
        cur = cur_ref[...]
        prev_row = jnp.where(first, 0.0, prev_ref[7:8, :])
        rolled = pltpu.roll(cur, 1, 0)
        rowi = lax.broadcasted_iota(jnp.int32, cur.shape, 0)
        prev = jnp.where(rowi == 0, prev_row, rolled)
        return cur + mu_ref[...] * (prev - cur)

    r = shift(r_ref, rp_ref, mu_r_ref)
    k = shift(k_ref, kp_ref, mu_k_ref)
    v = shift(v_ref, vp_ref, mu_v_ref)
    lo = shift(lo_ref, lop_ref, mu_lo_ref)
    wa_lo = lo[:, :LORA_W + LORA_A]
    g_lo = lo[:, LORA_W + LORA_A:LORA_W + LORA_A + g2_ref.shape[0]]
    w = -_softplus(-(w0_ref[...] + _mm_f32(jnp.tanh(wa_lo), w2_ref[...]))) - 0.5
    a = jax.nn.sigmoid(a0_ref[...] + _mm_f32(wa_lo, a2_ref[...]))
    g = _mm_f32(jax.nn.sigmoid(g_lo), g2_ref[...])
    r_out[...] = r
    lw_out[...] = -jnp.exp(w)
    k_out[...] = k * (1.0 + (a - 1.0) * ka_ref[...])
    v_out[...] = v
    kk_out[...] = k * kk_ref[...]
    a_out[...] = a
    g_out[...] = g


def _rw_prep(proj, mu, w0, w2, a0, a2, g2, k_k, k_a, *, rw_width, col0, tm):
    T = proj.shape[0]
    W = rw_width
    cb = col0 // W
    lb = (col0 + 3 * W) // LORA_PAD
    sub = 8

    def cur(width, blk):
        return pl.BlockSpec((tm, width), lambda i: (i, blk))

    def prev(width, blk):
        return pl.BlockSpec((sub, width), lambda i: (jnp.maximum(i * (tm // sub) - 1, 0), blk))

    def vec(width):
        return pl.BlockSpec((1, width), lambda i: (0, 0))

    def full(a):
        return pl.BlockSpec(a.shape, lambda i: (0, 0))

    mu_r, mu_k, mu_v = mu[:, :W], mu[:, W:2 * W], mu[:, 2 * W:3 * W]
    mu_lo = jnp.pad(mu[:, 3 * W:], ((0, 0), (0, LORA_PAD - (LORA_W + LORA_A + LORA_G))))
    out = jax.ShapeDtypeStruct((T, W), F32)
    return pl.pallas_call(
        _rw_prep_kernel,
        grid=(T // tm,),
        in_specs=[cur(W, cb), cur(W, cb + 1), cur(W, cb + 2), cur(LORA_PAD, lb),
                  prev(W, cb), prev(W, cb + 1), prev(W, cb + 2), prev(LORA_PAD, lb),
                  vec(W), vec(W), vec(W), vec(LORA_PAD), vec(W), full(w2), vec(W), full(a2), full(g2),
                  vec(W), vec(W)],
        out_specs=[pl.BlockSpec((tm, W), lambda i: (i, 0))] * 7,
        out_shape=[out] * 7,
        compiler_params=_cparams(("parallel",)),
        name="rw_prep",
    )(proj, proj, proj, proj, proj, proj, proj, proj,
      mu_r, mu_k, mu_v, mu_lo, w0, w2, a0, a2, g2, k_k, k_a)


def _rw_scan_kernel(r_ref, lw_ref, k_ref, v_ref, kk_ref, a_ref, g_ref, rk_ref, lng_ref, lnb_ref,
                    o_ref, state_ref):
    C = RW_CHUNK
    N = RW_HEAD_DIM

    @pl.when(pl.program_id(1) == 0)
    def _():
        state_ref[...] = jnp.zeros_like(state_ref)

    row = lax.broadcasted_iota(jnp.int32, (C, C), 0)
    col = lax.broadcasted_iota(jnp.int32, (C, C), 1)
    lower_incl = col <= row
    lower_strict = col < row
    tril = lower_incl.astype(BF16)
    eye = (row == col).astype(F32)

    outs = []
    for h in range(LANES // N):
        sl = slice(h * N, (h + 1) * N)
        r = r_ref[:, sl]
        lw = lw_ref[:, sl]
        k = k_ref[:, sl]
        v = v_ref[:, sl]
        kk = kk_ref[:, sl]
        a = a_ref[:, sl]
        kkn = kk * lax.rsqrt(jnp.sum(kk * kk, axis=-1, keepdims=True) + 1e-12)
        av = -kkn
        bv = kkn * a
        cum = _cumsum_rows(lw, tril)
        cum_ex = cum - lw
        at = av * jnp.exp(cum_ex)
        rt = r * jnp.exp(cum)
        inv = jnp.exp(-cum)
        bt = bv * inv
        kt = k * inv
        l_ab = jnp.where(lower_strict, _mm_nt(at, bt), 0.0)
        l_ak = jnp.where(lower_strict, _mm_nt(at, kt), 0.0)
        m_rb = jnp.where(lower_incl, _mm_nt(rt, bt), 0.0)
        m_rk = jnp.where(lower_incl, _mm_nt(rt, kt), 0.0)
        tinv = eye + l_ab
        p = l_ab
        for _ in range(int(math.log2(C)) - 1):
            p = _mm(p, p)
            tinv = tinv + _mm(tinv, p)
        s0 = state_ref[h]
        u = _mm(tinv, _mm_nt(at, s0) + _mm(l_ak, v))
        y = _mm_nt(rt, s0) + _mm(m_rb, u) + _mm(m_rk, v)
        g_last = jnp.exp(cum[C - 1:C, :])
        state_ref[h] = (s0 + _mm_tn(u, bt) + _mm_tn(v, kt)) * g_last
        mean = jnp.mean(y, axis=-1, keepdims=True)
        yc = y - mean
        var = jnp.mean(yc * yc, axis=-1, keepdims=True)
        yn = yc * lax.rsqrt(var + GN_EPS) * lng_ref[:, sl] + lnb_ref[:, sl]
        bonus = jnp.sum(r * k * rk_ref[:, sl], axis=-1, keepdims=True) * v
        outs.append((yn + bonus) * g_ref[:, sl])
    o_ref[...] = jnp.concatenate(outs, axis=-1).astype(o_ref.dtype)


def _cumsum_rows(x, tril_bf16):
    hi = x.astype(BF16)
    lo = (x - hi.astype(F32)).astype(BF16)
    return (jnp.dot(tril_bf16, hi, preferred_element_type=F32)
            + jnp.dot(tril_bf16, lo, preferred_element_type=F32))


def _rw_scan(r, lw, k, v, kk, a, g, r_k, ln_g, ln_b):
    T, W = r.shape
    C = RW_CHUNK
    blk = pl.BlockSpec((C, LANES), lambda p, c: (c, p))
    vec = pl.BlockSpec((1, LANES), lambda p, c: (0, p))
    return pl.pallas_call(
        _rw_scan_kernel,
        grid=(W // LANES, T // C),
        in_specs=[blk] * 7 + [vec] * 3,
        out_specs=blk,
        out_shape=jax.ShapeDtypeStruct((T, W), BF16),
        scratch_shapes=[pltpu.VMEM((LANES // RW_HEAD_DIM, RW_HEAD_DIM, RW_HEAD_DIM), F32)],
        compiler_params=_cparams(("parallel", "arbitrary")),
        name="rw_scan",
    )(r, lw, k, v, kk, a, g, r_k, ln_g, ln_b)


def _out_proj_kernel(x_ref, sb_ref, rw_ref, wa_ref, wb_ref, o_ref):
    o_ref[...] = (x_ref[...]
                  + jnp.dot(sb_ref[...], wa_ref[...], preferred_element_type=F32)
                  + jnp.dot(rw_ref[...], wb_ref[...], preferred_element_type=F32))


def _out_proj(x, sb, rw, w_out, *, tm):
    T, D = x.shape
    Wa = sb.shape[1]
    Wb = rw.shape[1]
    return pl.pallas_call(
        _out_proj_kernel,
        grid=(T // tm,),
        in_specs=[
            pl.BlockSpec((tm, D), lambda i: (i, 0)),
            pl.BlockSpec((tm, Wa), lambda i: (i, 0)),
            pl.BlockSpec((tm, Wb), lambda i: (i, 0)),
            pl.BlockSpec((Wa, D), lambda i: (0, 0)),
            pl.BlockSpec((Wb, D), lambda i: (1, 0)),
        ],
        out_specs=pl.BlockSpec((tm, D), lambda i: (i, 0)),
        out_shape=jax.ShapeDtypeStruct((T, D), F32),
        compiler_params=_cparams(("parallel",)),
        name="out_proj",
    )(x, sb, rw, w_out, w_out)


def _router_kernel(x_ref, g_ref, w_ref, b_ref, hn_ref, meta_ref, cnt_ref, carry_ref, *, tm):
    i = pl.program_id(0)

    @pl.when(i == 0)
    def _():
        carry_ref[...] = jnp.zeros_like(carry_ref)

    hn = _rms(x_ref[...]) * g_ref[...]
    hn_ref[...] = hn.astype(hn_ref.dtype)
    logits = _mm_f32(hn, w_ref[...]) + b_ref[...]
    lane = lax.broadcasted_iota(jnp.int32, logits.shape, 1).astype(F32)
    big = float(LANES)
    neg = -jnp.inf
    is_grp = (lane >= N_EXPERTS) & (lane < N_EXPERTS + N_GROUPS)
    gl = jnp.where(is_grp, logits, neg)
    gmax = jnp.max(gl, axis=-1, keepdims=True)
    g_w = 1.0 / jnp.sum(jnp.where(is_grp, jnp.exp(gl - gmax), 0.0), axis=-1, keepdims=True)
    grp = jnp.min(jnp.where(gl == gmax, lane, big), axis=-1, keepdims=True) - N_EXPERTS
    lo = grp * EXPERTS_PER_GROUP
    in_grp = (lane >= lo) & (lane < lo + EXPERTS_PER_GROUP)
    es = jnp.where(in_grp, logits, neg)
    m1 = jnp.max(es, axis=-1, keepdims=True)
    i1 = jnp.min(jnp.where(es == m1, lane, big), axis=-1, keepdims=True)
    es2 = jnp.where(lane == i1, neg, es)
    m2 = jnp.max(es2, axis=-1, keepdims=True)
    i2 = jnp.min(jnp.where(es2 == m2, lane, big), axis=-1, keepdims=True)
    z = jnp.sum(jnp.where(in_grp, jnp.exp(es - m1), 0.0), axis=-1, keepdims=True)
    p1 = 1.0 / z
    p2 = jnp.exp(m2 - m1) / z
    w1 = g_w * (p1 / (p1 + p2))
    w2 = g_w * (p2 / (p1 + p2))
    oh1 = lane == i1
    oh2 = lane == i2
    oh = (oh1 | oh2).astype(BF16)
    row = lax.broadcasted_iota(jnp.int32, (tm, tm), 0)
    col = lax.broadcasted_iota(jnp.int32, (tm, tm), 1)
    before = jnp.dot((col < row).astype(BF16), oh, preferred_element_type=F32) + carry_ref[0:1, :]
    rank1 = jnp.sum(jnp.where(oh1, before, 0.0), axis=-1, keepdims=True)
    rank2 = jnp.sum(jnp.where(oh2, before, 0.0), axis=-1, keepdims=True)
    carry_ref[...] = carry_ref[...] + jnp.sum(oh.astype(F32), axis=0, keepdims=True)
    meta = jnp.where(lane == 0, i1, 0.0)
    meta = jnp.where(lane == 1, i2, meta)
    meta = jnp.where(lane == 2, w1, meta)
    meta = jnp.where(lane == 3, w2, meta)
    meta = jnp.where(lane == 4, rank1, meta)
    meta = jnp.where(lane == 5, rank2, meta)
    meta_ref[...] = meta
    cnt_ref[...] = carry_ref[...]


def _router(x, g, w, b, *, tm):
    T, D = x.shape
    return pl.pallas_call(
        functools.partial(_router_kernel, tm=tm),
        grid=(T // tm,),
        in_specs=[
            pl.BlockSpec((tm, D), lambda i: (i, 0)),
            pl.BlockSpec((1, D), lambda i: (0, 0)),
            pl.BlockSpec((D, LANES), lambda i: (0, 0)),
            pl.BlockSpec((1, LANES), lambda i: (0, 0)),
        ],
        out_specs=[
            pl.BlockSpec((tm, D), lambda i: (i, 0)),
            pl.BlockSpec((tm, LANES), lambda i: (i, 0)),
            pl.BlockSpec((8, LANES), lambda i: (0, 0)),
        ],
        out_shape=[
            jax.ShapeDtypeStruct((T, D), BF16),
            jax.ShapeDtypeStruct((T, LANES), F32),
            jax.ShapeDtypeStruct((8, LANES), F32),
        ],
        scratch_shapes=[pltpu.VMEM((8, LANES), F32)],
        compiler_params=_cparams(("arbitrary",)),
        name="router",
    )(x, g, w, b)


def _expert_kernel(be_ref, nused_ref, x_ref, wu_ref, wd_ref, o_ref):
    b = pl.program_id(0)

    @pl.when(b < nused_ref[0])
    def _():
        h = jnp.dot(x_ref[...], wu_ref[0], preferred_element_type=F32)
        ff = h.shape[1] // 2
        gate = h[:, :ff]
        up = h[:, ff:]
        act = (gate * jax.nn.sigmoid(gate) * up).astype(BF16)
        o_ref[...] = jnp.dot(act, wd_ref[0], preferred_element_type=F32).astype(o_ref.dtype)

    @pl.when(b >= nused_ref[0])
    def _():
        o_ref[...] = jnp.zeros_like(o_ref)


def _expert_mlp(x_pad, blk_expert, n_used, w_up, w_down):
    R, D = x_pad.shape
    bm = MOE_ROWS
    F2 = w_up.shape[2]
    grid_spec = pltpu.PrefetchScalarGridSpec(
        num_scalar_prefetch=2,
        grid=(R // bm,),
        in_specs=[
            pl.BlockSpec((bm, D), lambda b, be, nu: (b, 0)),
            pl.BlockSpec((1, D, F2), lambda b, be, nu: (be[b], 0, 0)),
            pl.BlockSpec((1, F2 // 2, D), lambda b, be, nu: (be[b], 0, 0)),
        ],
        out_specs=pl.BlockSpec((bm, D), lambda b, be, nu: (b, 0)),
    )
    return pl.pallas_call(
        _expert_kernel,
        grid_spec=grid_spec,
        out_shape=jax.ShapeDtypeStruct((R, D), BF16),
        compiler_params=_cparams(("arbitrary",)),
        name="expert_mlp",
    )(blk_expert, n_used, x_pad, w_up, w_down)


def _combine_ple_kernel(x_ref, y1_ref, y2_ref, meta_ref, p_ref, wg_ref, wp_ref, g_ref, o_ref):
    meta = meta_ref[...]
    x = x_ref[...] + meta[:, 2:3] * y1_ref[...].astype(F32) + meta[:, 3:4] * y2_ref[...].astype(F32)
    gate = jax.nn.sigmoid(jnp.dot(_rms(x).astype(BF16), wg_ref[...], preferred_element_type=F32))
    pe = jnp.dot(p_ref[...].astype(BF16), wp_ref[...], preferred_element_type=F32)
    o_ref[...] = x + _rms(pe * gate) * g_ref[...]


def _combine_ple(x, y1, y2, meta, p, w_gate, w_proj, g, *, tm):
    T, D = x.shape
    P = p.shape[1]
    row = lambda w: pl.BlockSpec((tm, w), lambda i: (i, 0))
    return pl.pallas_call(
        _combine_ple_kernel,
        grid=(T // tm,),
        in_specs=[row(D), row(D), row(D), row(LANES), row(P),
                  pl.BlockSpec((D, D), lambda i: (0, 0)),
                  pl.BlockSpec((P, D), lambda i: (0, 0)),
                  pl.BlockSpec((1, D), lambda i: (0, 0))],
        out_specs=row(D),
        out_shape=jax.ShapeDtypeStruct((T, D), F32),
        compiler_params=_cparams(("parallel",)),
        name="combine_ple",
    )(x, y1, y2, meta, p, w_gate, w_proj, g)


def _moe_layout(meta, counts, T):
    bm = MOE_ROWS
    n_rows = 2 * T + N_EXPERTS * bm
    nb = n_rows // bm
    e = meta[:, 0:2].astype(jnp.int32)
    rank = meta[:, 4:6].astype(jnp.int32)
    cnt = counts[0, :N_EXPERTS].astype(jnp.int32)
    padded = ((cnt + bm - 1) // bm) * bm
    pad_ends = jnp.cumsum(padded)
    pad_starts = pad_ends - padded
    dest = pad_starts[e] + rank
    tok = jnp.broadcast_to(jnp.arange(T, dtype=jnp.int32)[:, None], (T, 2))
    row_tok = jnp.zeros((n_rows,), jnp.int32).at[dest.reshape(-1)].set(tok.reshape(-1))
    blk_expert = jnp.minimum(
        jnp.searchsorted(pad_ends, jnp.arange(nb, dtype=jnp.int32) * bm, side="right"), N_EXPERTS - 1
    ).astype(jnp.int32)
    n_used = (pad_ends[-1:] // bm).astype(jnp.int32)
    return dest, row_tok, blk_expert, n_used


def kernel(x, p, ln1_g, w_in, sb_q_g, sb_k_g, sb_out_g, rw_mu, rw_w0, rw_w2, rw_a0, rw_a2, rw_g2, rw_k_k, rw_k_a, rw_r_k, rw_ln_g, rw_ln_b, w_out, ln2_g, router_g, router_g_b, router_e, router_e_b, w_up, w_down, ple_proj, ple_gate, ple_norm_g):
    B, S, D = x.shape
    depth = w_in.shape[0]
    T = B * S
    assert B == 1, "token shift and attention assume one sequence"
    sb_width = D // 2
    rw_width = D - sb_width
    n_sb_heads = sb_width // SB_HEAD_DIM
    lora = LORA_W + LORA_A + LORA_G
    col0 = 3 * sb_width
    row2 = lambda a: a.reshape(1, -1)

    xf = x.reshape(T, D)
    for l in range(depth):
        w_in_l = jnp.pad(w_in[l], ((0, 0), (0, LORA_PAD - lora))).astype(BF16)
        proj = _norm_matmul(xf, row2(ln1_g[l]), w_in_l, tm=min(1024, T), tn=512, out_dtype=F32, name="in_proj")
        sb = _sb_attention(proj, row2(sb_q_g[l]), row2(sb_k_g[l]), row2(sb_out_g[l]),
                           n_heads=n_sb_heads, tq=256)
        w2 = jnp.pad(rw_w2[l], ((0, LORA_A), (0, 0)))
        a2 = jnp.pad(rw_a2[l], ((LORA_W, 0), (0, 0)))
        g2 = jnp.pad(rw_g2[l], ((0, 2 * LANES - LORA_G), (0, 0)))
        r, lw, k, v, kk, a, g = _rw_prep(
            proj, row2(rw_mu[l]), row2(rw_w0[l]), w2, row2(rw_a0[l]), a2, g2,
            row2(rw_k_k[l]), row2(rw_k_a[l]), rw_width=rw_width, col0=col0, tm=256)
        rw = _rw_scan(r, lw, k, v, kk, a, g, row2(rw_r_k[l]), row2(rw_ln_g[l]), row2(rw_ln_b[l]))
        x1 = _out_proj(xf, sb, rw, w_out[l].astype(BF16), tm=256)

        w_r = jnp.zeros((D, LANES), F32)
        w_r = w_r.at[:, :N_EXPERTS].set(router_e[l]).at[:, N_EXPERTS:N_EXPERTS + N_GROUPS].set(router_g[l])
        b_r = jnp.zeros((1, LANES), F32)
        b_r = b_r.at[0, :N_EXPERTS].set(router_e_b[l]).at[0, N_EXPERTS:N_EXPERTS + N_GROUPS].set(router_g_b[l])
        hn, meta, counts = _router(x1, row2(ln2_g[l]), w_r, b_r, tm=256)
        dest, row_tok, blk_expert, n_used = _moe_layout(meta, counts, T)
        x_pad = jnp.take(hn, row_tok, axis=0)
        y_pad = _expert_mlp(x_pad, blk_expert, n_used, w_up[l].astype(BF16), w_down[l].astype(BF16))
        y1 = jnp.take(y_pad, dest[:, 0], axis=0)
        y2 = jnp.take(y_pad, dest[:, 1], axis=0)
        xf = _combine_ple(x1, y1, y2, meta, p[l].reshape(T, -1), ple_gate[l].astype(BF16),
                          ple_proj[l].astype(BF16), row2(ple_norm_g[l]), tm=256)
    return xf.reshape(B, S, D)
```

```python
import functools
import math

import jax
import jax.numpy as jnp
from jax import lax
from jax.experimental import pallas as pl
from jax.experimental.pallas import tpu as pltpu

F32 = jnp.float32
BF16 = jnp.bfloat16

SB_HEAD_DIM = 128
RW_HEAD_DIM = 64
LORA_W = 64
LORA_A = 64
LORA_G = 160
LORA_PAD = 512
N_GROUPS = 4
EXPERTS_PER_GROUP = 8
N_EXPERTS = N_GROUPS * EXPERTS_PER_GROUP
RMS_EPS = 1e-6
GN_EPS = 64e-5
LANES = 128
VMEM_LIMIT = 56 * 1024 * 1024

EXP_UNDERFLOW = -104.0
RW_CHUNK = 64
MOE_ROWS = 256


def _cparams(sem):
    return pltpu.CompilerParams(dimension_semantics=sem, vmem_limit_bytes=VMEM_LIMIT)


def _rms(x, eps=RMS_EPS):
    return x * lax.rsqrt(jnp.mean(x * x, axis=-1, keepdims=True) + eps)


def _softplus(y):
    return jnp.maximum(y, 0.0) + jnp.log(1.0 + jnp.exp(-jnp.abs(y)))


def _mm(a, b):
    return jnp.dot(a.astype(BF16), b.astype(BF16), preferred_element_type=F32)


def _mm_nt(a, b):
    return lax.dot_general(a.astype(BF16), b.astype(BF16), (((1,), (1,)), ((), ())),
                           preferred_element_type=F32)


def _mm_tn(a, b):
    return lax.dot_general(a.astype(BF16), b.astype(BF16), (((0,), (0,)), ((), ())),
                           preferred_element_type=F32)


def _mm_f32(a, b):
    return jnp.dot(a, b, preferred_element_type=F32, precision=lax.Precision.HIGHEST)


def _mm_split(a, b_exact):
    hi = a.astype(BF16)
    lo = (a - hi.astype(F32)).astype(BF16)
    return (jnp.dot(hi, b_exact, preferred_element_type=F32)
            + jnp.dot(lo, b_exact, preferred_element_type=F32))


def _norm_matmul_kernel(x_ref, g_ref, w_ref, o_ref, xn_ref):
    @pl.when(pl.program_id(1) == 0)
    def _():
        xn_ref[...] = (_rms(x_ref[...]) * g_ref[...]).astype(BF16)

    o_ref[...] = jnp.dot(xn_ref[...], w_ref[...], preferred_element_type=F32).astype(o_ref.dtype)


def _norm_matmul(x, g, w, *, tm, tn, out_dtype, name):
    T, D = x.shape
    N = w.shape[1]
    return pl.pallas_call(
        _norm_matmul_kernel,
        grid=(T // tm, N // tn),
        in_specs=[
            pl.BlockSpec((tm, D), lambda i, j: (i, 0)),
            pl.BlockSpec((1, D), lambda i, j: (0, 0)),
            pl.BlockSpec((D, tn), lambda i, j: (0, j)),
        ],
        out_specs=pl.BlockSpec((tm, tn), lambda i, j: (i, j)),
        out_shape=jax.ShapeDtypeStruct((T, N), out_dtype),
        scratch_shapes=[pltpu.VMEM((tm, D), BF16)],
        compiler_params=_cparams(("parallel", "arbitrary")),
        name=name,
    )(x, g, w)


def _sb_attn_kernel(q_ref, k_ref, v_ref, qg_ref, kg_ref, og_ref, o_ref, kn_ref, vb_ref, *, tq, seq):
    i = pl.program_id(1)
    hd = SB_HEAD_DIM
    prep_rows = 512

    @pl.when(i == 0)
    def _():
        def body(c, carry):
            rows = pl.ds(pl.multiple_of(c * prep_rows, prep_rows), prep_rows)
            kn_ref[rows, :] = (_rms(k_ref[rows, :]) * kg_ref[...]).astype(BF16)
            vb_ref[rows, :] = v_ref[rows, :].astype(BF16)
            return carry

        lax.fori_loop(0, seq // prep_rows, body, 0)

    qn = (_rms(q_ref[...]) * qg_ref[...] * (1.0 / math.sqrt(hd))).astype(BF16)
    row = lax.broadcasted_iota(jnp.int32, (tq, tq), 0)
    col = lax.broadcasted_iota(jnp.int32, (tq, tq), 1)
    tri = (row > col).astype(BF16)
    causal = col < row

    def tile(j, carry, acc, masked):
        rows = pl.ds(pl.multiple_of(j * tq, tq), tq)
        z = _mm_nt(qn, kn_ref[rows, :])
        sp = _softplus(z)
        log_keep = -sp
        log_beta = z - sp
        if masked:
            log_keep = jnp.where(causal, log_keep, 0.0)
        later = _mm_split(log_keep, tri)
        w = jnp.exp(log_beta + later + carry)
        if masked:
            w = jnp.where(causal, w, 0.0)
        acc = acc + jnp.dot(w.astype(BF16), vb_ref[rows, :], preferred_element_type=F32)
        carry = carry + later[:, 0:1] + log_keep[:, 0:1]
        return carry, acc

    def live(carry):
        return (jnp.max(carry) > EXP_UNDERFLOW).astype(jnp.int32)

    def cond(st):
        return (st[0] < i) & (st[1] > 0)

    def body(st):
        n, _, carry, acc = st
        carry, acc = tile(i - 1 - n, carry, acc, False)
        return n + 1, live(carry), carry, acc

    carry, acc = tile(i, jnp.zeros((tq, 1), F32), jnp.zeros((tq, hd), F32), True)
    _, _, carry, acc = lax.while_loop(cond, body, (jnp.int32(0), live(carry), carry, acc))
    o_ref[...] = (_rms(acc) * og_ref[...]).astype(o_ref.dtype)


def _sb_attention(proj, q_g, k_g, out_g, *, n_heads, tq):
    T = proj.shape[0]
    hd = SB_HEAD_DIM
    gspec = pl.BlockSpec((1, hd), lambda h, i: (0, 0))
    return pl.pallas_call(
        functools.partial(_sb_attn_kernel, tq=tq, seq=T),
        grid=(n_heads, T // tq),
        in_specs=[
            pl.BlockSpec((tq, hd), lambda h, i: (i, h)),
            pl.BlockSpec((T, hd), lambda h, i: (0, n_heads + h)),
            pl.BlockSpec((T, hd), lambda h, i: (0, 2 * n_heads + h)),
            gspec, gspec, gspec,
        ],
        out_specs=pl.BlockSpec((tq, hd), lambda h, i: (i, h)),
        out_shape=jax.ShapeDtypeStruct((T, n_heads * hd), BF16),
        scratch_shapes=[pltpu.VMEM((T, hd), BF16), pltpu.VMEM((T, hd), BF16)],
        compiler_params=_cparams(("parallel", "arbitrary")),
        name="sb_attention",
    )(proj, proj, proj, q_g, k_g, out_g)


def _rw_prep_kernel(r_ref, k_ref, v_ref, lo_ref, rp_ref, kp_ref, vp_ref, lop_ref,
                    mu_r_ref, mu_k_ref, mu_v_ref, mu_lo_ref, w0_ref, w2_ref, a0_ref, a2_ref, g2_ref,
                    kk_ref, ka_ref,
                    r_out, lw_out, k_out, v_out, kk_out, a_out, g_out):
    first = pl.program_id(0) == 0

    def shift(cur_ref, prev_ref, mu_ref):
        cur = cur_ref[...]
        prev_row = jnp.where(first, 0.0, prev_ref[7:8, :])
        rolled = pltpu.roll(cur, 1, 0)
        rowi = lax.broadcasted_iota(jnp.int32, cur.shape, 0)
        prev = jnp.where(rowi == 0, prev_row, rolled)
        return cur + mu_ref[...] * (prev - cur)

    r = shift(r_ref, rp_ref, mu_r_ref)
    k = shift(k_ref, kp_ref, mu_k_ref)
    v = shift(v_ref, vp_ref, mu_v_ref)
    lo = shift(lo_ref, lop_ref, mu_lo_ref)
    wa_lo = lo[:, :LORA_W + LORA_A]
    g_lo = lo[:, LORA_W + LORA_A:LORA_W + LORA_A + g2_ref.shape[0]]
    w = -_softplus(-(w0_ref[...] + _mm_f32(jnp.tanh(wa_lo), w2_ref[...]))) - 0.5
    a = jax.nn.sigmoid(a0_ref[...] + _mm_f32(wa_lo, a2_ref[...]))
    g = _mm_f32(jax.nn.sigmoid(g_lo), g2_ref[...])
    r_out[...] = r
    lw_out[...] = -jnp.exp(w)
    k_out[...] = k * (1.0 + (a - 1.0) * ka_ref[...])
    v_out[...] = v
    kk_out[...] = k * kk_ref[...]
    a_out[...] = a
    g_out[...] = g


def _rw_prep(proj, mu, w0, w2, a0, a2, g2, k_k, k_a, *, rw_width, col0, tm):
    T = proj.shape[0]
    W = rw_width
    cb = col0 // W
    lb = (col0 + 3 * W) // LORA_PAD
    sub = 8

    def cur(width, blk):
        return pl.BlockSpec((tm, width), lambda i: (i, blk))

    def prev(width, blk):
        return pl.BlockSpec((sub, width), lambda i: (jnp.maximum(i * (tm // sub) - 1, 0), blk))

    def vec(width):
        return pl.BlockSpec((1, width), lambda i: (0, 0))

    def full(a):
        return pl.BlockSpec(a.shape, lambda i: (0, 0))

    mu_r, mu_k, mu_v = mu[:, :W], mu[:, W:2 * W], mu[:, 2 * W:3 * W]
    mu_lo = jnp.pad(mu[:, 3 * W:], ((0, 0), (0, LORA_PAD - (LORA_W + LORA_A + LORA_G))))
    out = jax.ShapeDtypeStruct((T, W), F32)
    return pl.pallas_call(
        _rw_prep_kernel,
        grid=(T // tm,),
        in_specs=[cur(W, cb), cur(W, cb + 1), cur(W, cb + 2), cur(LORA_PAD, lb),
                  prev(W, cb), prev(W, cb + 1), prev(W, cb + 2), prev(LORA_PAD, lb),
                  vec(W), vec(W), vec(W), vec(LORA_PAD), vec(W), full(w2), vec(W), full(a2), full(g2),
                  vec(W), vec(W)],
        out_specs=[pl.BlockSpec((tm, W), lambda i: (i, 0))] * 7,
        out_shape=[out] * 7,
        compiler_params=_cparams(("parallel",)),
        name="rw_prep",
    )(proj, proj, proj, proj, proj, proj, proj, proj,
      mu_r, mu_k, mu_v, mu_lo, w0, w2, a0, a2, g2, k_k, k_a)


def _cumsum_rows(x, tril_bf16):
    hi = x.astype(BF16)
    lo = (x - hi.astype(F32)).astype(BF16)
    return (jnp.dot(tril_bf16, hi, preferred_element_type=F32)
            + jnp.dot(tril_bf16, lo, preferred_element_type=F32))


def _rw_scan_kernel(r_ref, lw_ref, k_ref, v_ref, kk_ref, a_ref, g_ref, rk_ref, lng_ref, lnb_ref,
                    o_ref, state_ref, *, npairs):
    C = RW_CHUNK
    N = RW_HEAD_DIM
    C2 = 2 * C
    assert LANES == 2 * N and C2 == LANES

    @pl.when(pl.program_id(1) == 0)
    def _():
        state_ref[...] = jnp.zeros_like(state_ref)

    row = lax.broadcasted_iota(jnp.int32, (C2, C2), 0)
    col = lax.broadcasted_iota(jnp.int32, (C2, C2), 1)
    lower_incl = col <= row
    lower_strict = col < row
    eye = (row == col).astype(F32)
    seg_ones = ((row < N) == (col < N)).astype(BF16)
    tril = (lax.broadcasted_iota(jnp.int32, (C, C), 1) <= lax.broadcasted_iota(jnp.int32, (C, C), 0)).astype(BF16)
    head0 = lax.broadcasted_iota(jnp.int32, (C, LANES), 1) < N

    def stack(x):
        return jnp.concatenate([jnp.where(head0, x, 0.0), jnp.where(head0, 0.0, x)], axis=0)

    def seg_sum(x):
        return _mm(x, seg_ones)

    P = range(npairs)
    sl = [slice(p * LANES, (p + 1) * LANES) for p in P]
    r = [r_ref[:, sl[p]] for p in P]
    lw = [lw_ref[:, sl[p]] for p in P]
    k = [k_ref[:, sl[p]] for p in P]
    v = [v_ref[:, sl[p]] for p in P]
    kk = [kk_ref[:, sl[p]] for p in P]
    a = [a_ref[:, sl[p]] for p in P]
    kkn = [kk[p] * lax.rsqrt(seg_sum(kk[p] * kk[p]) + 1e-12) for p in P]
    cum = [_cumsum_rows(lw[p], tril) for p in P]
    inv = [jnp.exp(-cum[p]) for p in P]
    a2 = [stack(-kkn[p] * jnp.exp(cum[p] - lw[p])).astype(BF16) for p in P]
    r2 = [stack(r[p] * jnp.exp(cum[p])).astype(BF16) for p in P]
    b2 = [stack(kkn[p] * a[p] * inv[p]).astype(BF16) for p in P]
    k2 = [stack(k[p] * inv[p]).astype(BF16) for p in P]
    v2 = [stack(v[p]).astype(BF16) for p in P]
    l_ab = [jnp.where(lower_strict, _mm_nt(a2[p], b2[p]), 0.0) for p in P]
    l_ak = [jnp.where(lower_strict, _mm_nt(a2[p], k2[p]), 0.0).astype(BF16) for p in P]
    m_rb = [jnp.where(lower_incl, _mm_nt(r2[p], b2[p]), 0.0).astype(BF16) for p in P]
    m_rk = [jnp.where(lower_incl, _mm_nt(r2[p], k2[p]), 0.0).astype(BF16) for p in P]
    tinv = [eye + l_ab[p] for p in P]
    pw = [l_ab[p].astype(BF16) for p in P]
    for _ in range(int(math.log2(C)) - 1):
        pw = [_mm(pw[p], pw[p]).astype(BF16) for p in P]
        tinv = [tinv[p] + _mm(tinv[p], pw[p]) for p in P]
    s0 = [state_ref[p] for p in P]
    s0b = [s0[p].astype(BF16) for p in P]
    rhs = [_mm_nt(a2[p], s0b[p]) + _mm(l_ak[p], v2[p]) for p in P]
    u2 = [_mm(tinv[p], rhs[p]).astype(BF16) for p in P]
    y2 = [_mm_nt(r2[p], s0b[p]) + _mm(m_rb[p], u2[p]) + _mm(m_rk[p], v2[p]) for p in P]
    for p in P:
        upd = _mm_tn(jnp.concatenate([u2[p], v2[p]], axis=0), jnp.concatenate([b2[p], k2[p]], axis=0))
        state_ref[p] = (s0[p] + upd) * jnp.exp(cum[p][C - 1:C, :])
    for p in P:
        y = y2[p][:C] + y2[p][C:]
        yc = y - seg_sum(y) * (1.0 / N)
        var = seg_sum(yc * yc) * (1.0 / N)
        yn = yc * lax.rsqrt(var + GN_EPS) * lng_ref[:, sl[p]] + lnb_ref[:, sl[p]]
        bonus = seg_sum(r[p] * k[p] * rk_ref[:, sl[p]]) * v[p]
        o_ref[:, sl[p]] = ((yn + bonus) * g_ref[:, sl[p]]).astype(o_ref.dtype)


def _rw_scan(r, lw, k, v, kk, a, g, r_k, ln_g, ln_b, *, npairs):
    T, W = r.shape
    C = RW_CHUNK
    wb = npairs * LANES
    blk = pl.BlockSpec((C, wb), lambda p, c: (c, p))
    vec = pl.BlockSpec((1, wb), lambda p, c: (0, p))
    return pl.pallas_call(
        functools.partial(_rw_scan_kernel, npairs=npairs),
        grid=(W // wb, T // C),
        in_specs=[blk] * 7 + [vec] * 3,
        out_specs=blk,
        out_shape=jax.ShapeDtypeStruct((T, W), BF16),
        scratch_shapes=[pltpu.VMEM((npairs, LANES, LANES), F32)],
        compiler_params=_cparams(("parallel", "arbitrary")),
        name="rw_scan",
    )(r, lw, k, v, kk, a, g, r_k, ln_g, ln_b)


def _out_proj_kernel(x_ref, sb_ref, rw_ref, wa_ref, wb_ref, o_ref):
    o_ref[...] = (x_ref[...]
                  + jnp.dot(sb_ref[...], wa_ref[...], preferred_element_type=F32)
                  + jnp.dot(rw_ref[...], wb_ref[...], preferred_element_type=F32))


def _out_proj(x, sb, rw, w_out, *, tm):
    T, D = x.shape
    Wa = sb.shape[1]
    Wb = rw.shape[1]
    return pl.pallas_call(
        _out_proj_kernel,
        grid=(T // tm,),
        in_specs=[
            pl.BlockSpec((tm, D), lambda i: (i, 0)),
            pl.BlockSpec((tm, Wa), lambda i: (i, 0)),
            pl.BlockSpec((tm, Wb), lambda i: (i, 0)),
            pl.BlockSpec((Wa, D), lambda i: (0, 0)),
            pl.BlockSpec((Wb, D), lambda i: (1, 0)),
        ],
        out_specs=pl.BlockSpec((tm, D), lambda i: (i, 0)),
        out_shape=jax.ShapeDtypeStruct((T, D), F32),
        compiler_params=_cparams(("parallel",)),
        name="out_proj",
    )(x, sb, rw, w_out, w_out)


def _router_kernel(x_ref, g_ref, w_ref, b_ref, hn_ref, meta_ref, cnt_ref, carry_ref, *, tm):
    i = pl.program_id(0)

    @pl.when(i == 0)
    def _():
        carry_ref[...] = jnp.zeros_like(carry_ref)

    hn = _rms(x_ref[...]) * g_ref[...]
    hn_ref[...] = hn.astype(hn_ref.dtype)
    logits = _mm_f32(hn, w_ref[...]) + b_ref[...]
    lane = lax.broadcasted_iota(jnp.int32, logits.shape, 1).astype(F32)
    big = float(LANES)
    neg = -jnp.inf
    is_grp = (lane >= N_EXPERTS) & (lane < N_EXPERTS + N_GROUPS)
    gl = jnp.where(is_grp, logits, neg)
    gmax = jnp.max(gl, axis=-1, keepdims=True)
    g_w = 1.0 / jnp.sum(jnp.where(is_grp, jnp.exp(gl - gmax), 0.0), axis=-1, keepdims=True)
    grp = jnp.min(jnp.where(gl == gmax, lane, big), axis=-1, keepdims=True) - N_EXPERTS
    lo = grp * EXPERTS_PER_GROUP
    in_grp = (lane >= lo) & (lane < lo + EXPERTS_PER_GROUP)
    es = jnp.where(in_grp, logits, neg)
    m1 = jnp.max(es, axis=-1, keepdims=True)
    i1 = jnp.min(jnp.where(es == m1, lane, big), axis=-1, keepdims=True)
    es2 = jnp.where(lane == i1, neg, es)
    m2 = jnp.max(es2, axis=-1, keepdims=True)
    i2 = jnp.min(jnp.where(es2 == m2, lane, big), axis=-1, keepdims=True)
    z = jnp.sum(jnp.where(in_grp, jnp.exp(es - m1), 0.0), axis=-1, keepdims=True)
    p1 = 1.0 / z
    p2 = jnp.exp(m2 - m1) / z
    w1 = g_w * (p1 / (p1 + p2))
    w2 = g_w * (p2 / (p1 + p2))
    oh1 = lane == i1
    oh2 = lane == i2
    oh = (oh1 | oh2).astype(BF16)
    row = lax.broadcasted_iota(jnp.int32, (tm, tm), 0)
    col = lax.broadcasted_iota(jnp.int32, (tm, tm), 1)
    before = jnp.dot((col < row).astype(BF16), oh, preferred_element_type=F32) + carry_ref[0:1, :]
    rank1 = jnp.sum(jnp.where(oh1, before, 0.0), axis=-1, keepdims=True)
    rank2 = jnp.sum(jnp.where(oh2, before, 0.0), axis=-1, keepdims=True)
    carry_ref[...] = carry_ref[...] + jnp.sum(oh.astype(F32), axis=0, keepdims=True)
    meta = jnp.where(lane == 0, i1, 0.0)
    meta = jnp.where(lane == 1, i2, meta)
    meta = jnp.where(lane == 2, w1, meta)
    meta = jnp.where(lane == 3, w2, meta)
    meta = jnp.where(lane == 4, rank1, meta)
    meta = jnp.where(lane == 5, rank2, meta)
    meta_ref[...] = meta
    cnt_ref[...] = carry_ref[...]


def _router(x, g, w, b, *, tm):
    T, D = x.shape
    return pl.pallas_call(
        functools.partial(_router_kernel, tm=tm),
        grid=(T // tm,),
        in_specs=[
            pl.BlockSpec((tm, D), lambda i: (i, 0)),
            pl.BlockSpec((1, D), lambda i: (0, 0)),
            pl.BlockSpec((D, LANES), lambda i: (0, 0)),
            pl.BlockSpec((1, LANES), lambda i: (0, 0)),
        ],
        out_specs=[
            pl.BlockSpec((tm, D), lambda i: (i, 0)),
            pl.BlockSpec((tm, LANES), lambda i: (i, 0)),
            pl.BlockSpec((8, LANES), lambda i: (0, 0)),
        ],
        out_shape=[
            jax.ShapeDtypeStruct((T, D), BF16),
            jax.ShapeDtypeStruct((T, LANES), F32),
            jax.ShapeDtypeStruct((8, LANES), F32),
        ],
        scratch_shapes=[pltpu.VMEM((8, LANES), F32)],
        compiler_params=_cparams(("arbitrary",)),
        name="router",
    )(x, g, w, b)


def _expert_kernel(be_ref, nused_ref, x_ref, wu_ref, wd_ref, o_ref):
    b = pl.program_id(0)

    @pl.when(b < nused_ref[0])
    def _():
        h = jnp.dot(x_ref[...], wu_ref[0], preferred_element_type=F32)
        ff = h.shape[1] // 2
        gate = h[:, :ff]
        up = h[:, ff:]
        act = (gate * jax.nn.sigmoid(gate) * up).astype(BF16)
        o_ref[...] = jnp.dot(act, wd_ref[0], preferred_element_type=F32).astype(o_ref.dtype)

    @pl.when(b >= nused_ref[0])
    def _():
        o_ref[...] = jnp.zeros_like(o_ref)


def _expert_mlp(x_pad, blk_expert, n_used, w_up, w_down):
    R, D = x_pad.shape
    bm = MOE_ROWS
    F2 = w_up.shape[2]
    grid_spec = pltpu.PrefetchScalarGridSpec(
        num_scalar_prefetch=2,
        grid=(R // bm,),
        in_specs=[
            pl.BlockSpec((bm, D), lambda b, be, nu: (b, 0)),
            pl.BlockSpec((1, D, F2), lambda b, be, nu: (be[b], 0, 0)),
            pl.BlockSpec((1, F2 // 2, D), lambda b, be, nu: (be[b], 0, 0)),
        ],
        out_specs=pl.BlockSpec((bm, D), lambda b, be, nu: (b, 0)),
    )
    return pl.pallas_call(
        _expert_kernel,
        grid_spec=grid_spec,
        out_shape=jax.ShapeDtypeStruct((R, D), BF16),
        compiler_params=_cparams(("arbitrary",)),
        name="expert_mlp",
    )(blk_expert, n_used, x_pad, w_up, w_down)


def _combine_ple_kernel(x_ref, y1_ref, y2_ref, meta_ref, p_ref, wg_ref, wp_ref, g_ref, o_ref):
    meta = meta_ref[...]
    x = x_ref[...] + meta[:, 2:3] * y1_ref[...].astype(F32) + meta[:, 3:4] * y2_ref[...].astype(F32)
    gate = jax.nn.sigmoid(jnp.dot(_rms(x).astype(BF16), wg_ref[...], preferred_element_type=F32))
    pe = jnp.dot(p_ref[...].astype(BF16), wp_ref[...], preferred_element_type=F32)
    o_ref[...] = x + _rms(pe * gate) * g_ref[...]


def _combine_ple(x, y1, y2, meta, p, w_gate, w_proj, g, *, tm):
    T, D = x.shape
    P = p.shape[1]
    row = lambda w: pl.BlockSpec((tm, w), lambda i: (i, 0))
    return pl.pallas_call(
        _combine_ple_kernel,
        grid=(T // tm,),
        in_specs=[row(D), row(D), row(D), row(LANES), row(P),
                  pl.BlockSpec((D, D), lambda i: (0, 0)),
                  pl.BlockSpec((P, D), lambda i: (0, 0)),
                  pl.BlockSpec((1, D), lambda i: (0, 0))],
        out_specs=row(D),
        out_shape=jax.ShapeDtypeStruct((T, D), F32),
        compiler_params=_cparams(("parallel",)),
        name="combine_ple",
    )(x, y1, y2, meta, p, w_gate, w_proj, g)


def _moe_layout(meta, counts, T):
    bm = MOE_ROWS
    n_rows = 2 * T + N_EXPERTS * bm
    nb = n_rows // bm
    e = meta[:, 0:2].astype(jnp.int32)
    rank = meta[:, 4:6].astype(jnp.int32)
    cnt = counts[0, :N_EXPERTS].astype(jnp.int32)
    padded = ((cnt + bm - 1) // bm) * bm
    pad_ends = jnp.cumsum(padded)
    pad_starts = pad_ends - padded
    dest = pad_starts[e] + rank
    tok = jnp.broadcast_to(jnp.arange(T, dtype=jnp.int32)[:, None], (T, 2))
    row_tok = jnp.zeros((n_rows,), jnp.int32).at[dest.reshape(-1)].set(tok.reshape(-1))
    blk_expert = jnp.minimum(
        jnp.searchsorted(pad_ends, jnp.arange(nb, dtype=jnp.int32) * bm, side="right"), N_EXPERTS - 1
    ).astype(jnp.int32)
    n_used = (pad_ends[-1:] // bm).astype(jnp.int32)
    return dest, row_tok, blk_expert, n_used


def kernel(x, p, ln1_g, w_in, sb_q_g, sb_k_g, sb_out_g, rw_mu, rw_w0, rw_w2, rw_a0, rw_a2, rw_g2, rw_k_k, rw_k_a, rw_r_k, rw_ln_g, rw_ln_b, w_out, ln2_g, router_g, router_g_b, router_e, router_e_b, w_up, w_down, ple_proj, ple_gate, ple_norm_g):
    B, S, D = x.shape
    depth = w_in.shape[0]
    T = B * S
    assert B == 1, "token shift and attention assume one sequence"
    sb_width = D // 2
    rw_width = D - sb_width
    n_sb_heads = sb_width // SB_HEAD_DIM
    lora = LORA_W + LORA_A + LORA_G
    col0 = 3 * sb_width
    row2 = lambda a: a.reshape(1, -1)

    xf = x.reshape(T, D)
    for l in range(depth):
        w_in_l = jnp.pad(w_in[l], ((0, 0), (0, LORA_PAD - lora))).astype(BF16)
        proj = _norm_matmul(xf, row2(ln1_g[l]), w_in_l, tm=min(1024, T), tn=512, out_dtype=F32, name="in_proj")
        sb = _sb_attention(proj, row2(sb_q_g[l]), row2(sb_k_g[l]), row2(sb_out_g[l]),
                           n_heads=n_sb_heads, tq=256)
        w2 = jnp.pad(rw_w2[l], ((0, LORA_A), (0, 0)))
        a2 = jnp.pad(rw_a2[l], ((LORA_W, 0), (0, 0)))
        g2 = jnp.pad(rw_g2[l], ((0, 2 * LANES - LORA_G), (0, 0)))
        r, lw, k, v, kk, a, g = _rw_prep(
            proj, row2(rw_mu[l]), row2(rw_w0[l]), w2, row2(rw_a0[l]), a2, g2,
            row2(rw_k_k[l]), row2(rw_k_a[l]), rw_width=rw_width, col0=col0, tm=256)
        rw = _rw_scan(r, lw, k, v, kk, a, g, row2(rw_r_k[l]), row2(rw_ln_g[l]), row2(rw_ln_b[l]), npairs=8)
        x1 = _out_proj(xf, sb, rw, w_out[l].astype(BF16), tm=256)

        w_r = jnp.zeros((D, LANES), F32)
        w_r = w_r.at[:, :N_EXPERTS].set(router_e[l]).at[:, N_EXPERTS:N_EXPERTS + N_GROUPS].set(router_g[l])
        b_r = jnp.zeros((1, LANES), F32)
        b_r = b_r.at[0, :N_EXPERTS].set(router_e_b[l]).at[0, N_EXPERTS:N_EXPERTS + N_GROUPS].set(router_g_b[l])
        hn, meta, counts = _router(x1, row2(ln2_g[l]), w_r, b_r, tm=256)
        dest, row_tok, blk_expert, n_used = _moe_layout(meta, counts, T)
        x_pad = jnp.take(hn, row_tok, axis=0)
        y_pad = _expert_mlp(x_pad, blk_expert, n_used, w_up[l].astype(BF16), w_down[l].astype(BF16))
        y1 = jnp.take(y_pad, dest[:, 0], axis=0)
        y2 = jnp.take(y_pad, dest[:, 1], axis=0)
        xf = _combine_ple(x1, y1, y2, meta, p[l].reshape(T, -1), ple_gate[l].astype(BF16),
                          ple_proj[l].astype(BF16), row2(ple_norm_g[l]), tm=256)
    return xf.reshape(B, S, D)
```

```python
import functools
import math

import jax
import jax.numpy as jnp
from jax import lax
from jax.experimental import pallas as pl
from jax.experimental.pallas import tpu as pltpu

F32 = jnp.float32
BF16 = jnp.bfloat16

SB_HEAD_DIM = 128
RW_HEAD_DIM = 64
LORA_W = 64
LORA_A = 64
LORA_G = 160
LORA_PAD = 512
N_GROUPS = 4
EXPERTS_PER_GROUP = 8
N_EXPERTS = N_GROUPS * EXPERTS_PER_GROUP
RMS_EPS = 1e-6
GN_EPS = 64e-5
LANES = 128
VMEM_LIMIT = 56 * 1024 * 1024

EXP_UNDERFLOW = -104.0
RW_CHUNK = 64
MOE_ROWS = 256


def _cparams(sem):
    return pltpu.CompilerParams(dimension_semantics=sem, vmem_limit_bytes=VMEM_LIMIT)


def _rms(x, eps=RMS_EPS):
    return x * lax.rsqrt(jnp.mean(x * x, axis=-1, keepdims=True) + eps)


def _softplus(y):
    return jnp.maximum(y, 0.0) + jnp.log(1.0 + jnp.exp(-jnp.abs(y)))


def _mm(a, b):
    return jnp.dot(a.astype(BF16), b.astype(BF16), preferred_element_type=F32)


def _mm_nt(a, b):
    return lax.dot_general(a.astype(BF16), b.astype(BF16), (((1,), (1,)), ((), ())),
                           preferred_element_type=F32)


def _mm_tn(a, b):
    return lax.dot_general(a.astype(BF16), b.astype(BF16), (((0,), (0,)), ((), ())),
                           preferred_element_type=F32)


def _mm_f32(a, b):
    return jnp.dot(a, b, preferred_element_type=F32, precision=lax.Precision.HIGHEST)


def _mm_split(a, b_exact):
    hi = a.astype(BF16)
    lo = (a - hi.astype(F32)).astype(BF16)
    return (jnp.dot(hi, b_exact, preferred_element_type=F32)
            + jnp.dot(lo, b_exact, preferred_element_type=F32))


def _norm_matmul_kernel(x_ref, g_ref, w_ref, o_ref, xn_ref):
    @pl.when(pl.program_id(1) == 0)
    def _():
        xn_ref[...] = (_rms(x_ref[...]) * g_ref[...]).astype(BF16)

    o_ref[...] = jnp.dot(xn_ref[...], w_ref[...], preferred_element_type=F32).astype(o_ref.dtype)


def _norm_matmul(x, g, w, *, tm, tn, out_dtype, name):
    T, D = x.shape
    N = w.shape[1]
    return pl.pallas_call(
        _norm_matmul_kernel,
        grid=(T // tm, N // tn),
        in_specs=[
            pl.BlockSpec((tm, D), lambda i, j: (i, 0)),
            pl.BlockSpec((1, D), lambda i, j: (0, 0)),
            pl.BlockSpec((D, tn), lambda i, j: (0, j)),
        ],
        out_specs=pl.BlockSpec((tm, tn), lambda i, j: (i, j)),
        out_shape=jax.ShapeDtypeStruct((T, N), out_dtype),
        scratch_shapes=[pltpu.VMEM((tm, D), BF16)],
        compiler_params=_cparams(("parallel", "arbitrary")),
        name=name,
    )(x, g, w)


def _sb_attn_kernel(q_ref, k_ref, v_ref, qg_ref, kg_ref, og_ref, o_ref, kn_ref, vb_ref, *, tq, seq):
    i = pl.program_id(1)
    hd = SB_HEAD_DIM
    prep_rows = 512

    @pl.when(i == 0)
    def _():
        def body(c, carry):
            rows = pl.ds(pl.multiple_of(c * prep_rows, prep_rows), prep_rows)
            kn_ref[rows, :] = (_rms(k_ref[rows, :]) * kg_ref[...]).astype(BF16)
            vb_ref[rows, :] = v_ref[rows, :].astype(BF16)
            return carry

        lax.fori_loop(0, seq // prep_rows, body, 0)

    qn = (_rms(q_ref[...]) * qg_ref[...] * (1.0 / math.sqrt(hd))).astype(BF16)
    row = lax.broadcasted_iota(jnp.int32, (tq, tq), 0)
    col = lax.broadcasted_iota(jnp.int32, (tq, tq), 1)
    tri = (row > col).astype(BF16)
    causal = col < row

    def tile(j, carry, acc, masked):
        rows = pl.ds(pl.multiple_of(j * tq, tq), tq)
        z = _mm_nt(qn, kn_ref[rows, :])
        sp = _softplus(z)
        log_keep = -sp
        log_beta = z - sp
        if masked:
            log_keep = jnp.where(causal, log_keep, 0.0)
        later = _mm_split(log_keep, tri)
        w = jnp.exp(log_beta + later + carry)
        if masked:
            w = jnp.where(causal, w, 0.0)
        acc = acc + jnp.dot(w.astype(BF16), vb_ref[rows, :], preferred_element_type=F32)
        carry = carry + later[:, 0:1] + log_keep[:, 0:1]
        return carry, acc

    def live(carry):
        return (jnp.max(carry) > EXP_UNDERFLOW).astype(jnp.int32)

    def cond(st):
        return (st[0] < i) & (st[1] > 0)

    def body(st):
        n, _, carry, acc = st
        carry, acc = tile(i - 1 - n, carry, acc, False)
        return n + 1, live(carry), carry, acc

    carry, acc = tile(i, jnp.zeros((tq, 1), F32), jnp.zeros((tq, hd), F32), True)
    _, _, carry, acc = lax.while_loop(cond, body, (jnp.int32(0), live(carry), carry, acc))
    o_ref[...] = (_rms(acc) * og_ref[...]).astype(o_ref.dtype)


def _sb_attention(proj, q_g, k_g, out_g, *, n_heads, tq):
    T = proj.shape[0]
    hd = SB_HEAD_DIM
    gspec = pl.BlockSpec((1, hd), lambda h, i: (0, 0))
    return pl.pallas_call(
        functools.partial(_sb_attn_kernel, tq=tq, seq=T),
        grid=(n_heads, T // tq),
        in_specs=[
            pl.BlockSpec((tq, hd), lambda h, i: (i, h)),
            pl.BlockSpec((T, hd), lambda h, i: (0, n_heads + h)),
            pl.BlockSpec((T, hd), lambda h, i: (0, 2 * n_heads + h)),
            gspec, gspec, gspec,
        ],
        out_specs=pl.BlockSpec((tq, hd), lambda h, i: (i, h)),
        out_shape=jax.ShapeDtypeStruct((T, n_heads * hd), BF16),
        scratch_shapes=[pltpu.VMEM((T, hd), BF16), pltpu.VMEM((T, hd), BF16)],
        compiler_params=_cparams(("parallel", "arbitrary")),
        name="sb_attention",
    )(proj, proj, proj, q_g, k_g, out_g)


def _rw_prep_kernel(r_ref, k_ref, v_ref, lo_ref, rp_ref, kp_ref, vp_ref, lop_ref,
                    mu_r_ref, mu_k_ref, mu_v_ref, mu_lo_ref, w0_ref, w2_ref, a0_ref, a2_ref, g2_ref,
                    kk_ref, ka_ref,
                    r_out, lw_out, k_out, v_out, kk_out, a_out, g_out):
    first = pl.program_id(0) == 0

    def shift(cur_ref, prev_ref, mu_ref):
        cur = cur_ref[...]
        prev_row = jnp.where(first, 0.0, prev_ref[7:8, :])
        rolled = pltpu.roll(cur, 1, 0)
        rowi = lax.broadcasted_iota(jnp.int32, cur.shape, 0)
        prev = jnp.where(rowi == 0, prev_row, rolled)
        return cur + mu_ref[...] * (prev - cur)

    r = shift(r_ref, rp_ref, mu_r_ref)
    k = shift(k_ref, kp_ref, mu_k_ref)
    v = shift(v_ref, vp_ref, mu_v_ref)
    lo = shift(lo_ref, lop_ref, mu_lo_ref)
    wa_lo = lo[:, :LORA_W + LORA_A]
    g_lo = lo[:, LORA_W + LORA_A:LORA_W + LORA_A + g2_ref.shape[0]]
    w = -_softplus(-(w0_ref[...] + _mm_f32(jnp.tanh(wa_lo), w2_ref[...]))) - 0.5
    a = jax.nn.sigmoid(a0_ref[...] + _mm_f32(wa_lo, a2_ref[...]))
    g = _mm_f32(jax.nn.sigmoid(g_lo), g2_ref[...])
    r_out[...] = r
    lw_out[...] = -jnp.exp(w)
    k_out[...] = k * (1.0 + (a - 1.0) * ka_ref[...])
    v_out[...] = v
    kk_out[...] = k * kk_ref[...]
    a_out[...] = a
    g_out[...] = g


def _rw_prep(proj, mu, w0, w2, a0, a2, g2, k_k, k_a, *, rw_width, col0, tm):
    T = proj.shape[0]
    W = rw_width
    cb = col0 // W
    lb = (col0 + 3 * W) // LORA_PAD
    sub = 8

    def cur(width, blk):
        return pl.BlockSpec((tm, width), lambda i: (i, blk))

    def prev(width, blk):
        return pl.BlockSpec((sub, width), lambda i: (jnp.maximum(i * (tm // sub) - 1, 0), blk))

    def vec(width):
        return pl.BlockSpec((1, width), lambda i: (0, 0))

    def full(a):
        return pl.BlockSpec(a.shape, lambda i: (0, 0))

    mu_r, mu_k, mu_v = mu[:, :W], mu[:, W:2 * W], mu[:, 2 * W:3 * W]
    mu_lo = jnp.pad(mu[:, 3 * W:], ((0, 0), (0, LORA_PAD - (LORA_W + LORA_A + LORA_G))))
    out = jax.ShapeDtypeStruct((T, W), F32)
    return pl.pallas_call(
        _rw_prep_kernel,
        grid=(T // tm,),
        in_specs=[cur(W, cb), cur(W, cb + 1), cur(W, cb + 2), cur(LORA_PAD, lb),
                  prev(W, cb), prev(W, cb + 1), prev(W, cb + 2), prev(LORA_PAD, lb),
                  vec(W), vec(W), vec(W), vec(LORA_PAD), vec(W), full(w2), vec(W), full(a2), full(g2),
                  vec(W), vec(W)],
        out_specs=[pl.BlockSpec((tm, W), lambda i: (i, 0))] * 7,
        out_shape=[out] * 7,
        compiler_params=_cparams(("parallel",)),
        name="rw_prep",
    )(proj, proj, proj, proj, proj, proj, proj, proj,
      mu_r, mu_k, mu_v, mu_lo, w0, w2, a0, a2, g2, k_k, k_a)


def _cumsum_rows(x, tril_bf16):
    hi = x.astype(BF16)
    lo = (x - hi.astype(F32)).astype(BF16)
    return (jnp.dot(tril_bf16, hi, preferred_element_type=F32)
            + jnp.dot(tril_bf16, lo, preferred_element_type=F32))


def _rw_scan_kernel(r_ref, lw_ref, k_ref, v_ref, kk_ref, a_ref, g_ref, rk_ref, lng_ref, lnb_ref,
                    o_ref, state_ref, *, npairs):
    C = RW_CHUNK
    N = RW_HEAD_DIM
    C2 = 2 * C
    assert LANES == 2 * N and C2 == LANES

    @pl.when(pl.program_id(1) == 0)
    def _():
        state_ref[...] = jnp.zeros_like(state_ref)

    row = lax.broadcasted_iota(jnp.int32, (C2, C2), 0)
    col = lax.broadcasted_iota(jnp.int32, (C2, C2), 1)
    lower_incl = col <= row
    lower_strict = col < row
    eye = (row == col).astype(F32)
    seg_ones = ((row < N) == (col < N)).astype(BF16)
    tril = (lax.broadcasted_iota(jnp.int32, (C, C), 1) <= lax.broadcasted_iota(jnp.int32, (C, C), 0)).astype(BF16)
    head0 = lax.broadcasted_iota(jnp.int32, (C, LANES), 1) < N

    def stack(x):
        return jnp.concatenate([jnp.where(head0, x, 0.0), jnp.where(head0, 0.0, x)], axis=0)

    def seg_sum(x):
        return _mm(x, seg_ones)

    P = range(npairs)
    sl = [slice(p * LANES, (p + 1) * LANES) for p in P]
    r = [r_ref[:, sl[p]] for p in P]
    lw = [lw_ref[:, sl[p]] for p in P]
    k = [k_ref[:, sl[p]] for p in P]
    v = [v_ref[:, sl[p]] for p in P]
    kk = [kk_ref[:, sl[p]] for p in P]
    a = [a_ref[:, sl[p]] for p in P]
    kkn = [kk[p] * lax.rsqrt(seg_sum(kk[p] * kk[p]) + 1e-12) for p in P]
    cum = [_cumsum_rows(lw[p], tril) for p in P]
    inv = [jnp.exp(-cum[p]) for p in P]
    a2 = [stack(-kkn[p] * jnp.exp(cum[p] - lw[p])).astype(BF16) for p in P]
    r2 = [stack(r[p] * jnp.exp(cum[p])).astype(BF16) for p in P]
    b2 = [stack(kkn[p] * a[p] * inv[p]).astype(BF16) for p in P]
    k2 = [stack(k[p] * inv[p]).astype(BF16) for p in P]
    v2 = [stack(v[p]).astype(BF16) for p in P]
    l_ab = [jnp.where(lower_strict, _mm_nt(a2[p], b2[p]), 0.0) for p in P]
    l_ak = [jnp.where(lower_strict, _mm_nt(a2[p], k2[p]), 0.0).astype(BF16) for p in P]
    m_rb = [jnp.where(lower_incl, _mm_nt(r2[p], b2[p]), 0.0).astype(BF16) for p in P]
    m_rk = [jnp.where(lower_incl, _mm_nt(r2[p], k2[p]), 0.0).astype(BF16) for p in P]
    tinv = [eye + l_ab[p] for p in P]
    pw = [l_ab[p].astype(BF16) for p in P]
    for _ in range(int(math.log2(C)) - 1):
        pw = [_mm(pw[p], pw[p]).astype(BF16) for p in P]
        tinv = [tinv[p] + _mm(tinv[p], pw[p]) for p in P]
    s0 = [state_ref[p] for p in P]
    s0b = [s0[p].astype(BF16) for p in P]
    rhs = [_mm_nt(a2[p], s0b[p]) + _mm(l_ak[p], v2[p]) for p in P]
    u2 = [_mm(tinv[p], rhs[p]).astype(BF16) for p in P]
    y2 = [_mm_nt(r2[p], s0b[p]) + _mm(m_rb[p], u2[p]) + _mm(m_rk[p], v2[p]) for p in P]
    for p in P:
        upd = _mm_tn(jnp.concatenate([u2[p], v2[p]], axis=0), jnp.concatenate([b2[p], k2[p]], axis=0))
        state_ref[p] = (s0[p] + upd) * jnp.exp(cum[p][C - 1:C, :])
    for p in P:
        y = y2[p][:C] + y2[p][C:]
        yc = y - seg_sum(y) * (1.0 / N)
        var = seg_sum(yc * yc) * (1.0 / N)
        yn = yc * lax.rsqrt(var + GN_EPS) * lng_ref[:, sl[p]] + lnb_ref[:, sl[p]]
        bonus = seg_sum(r[p] * k[p] * rk_ref[:, sl[p]]) * v[p]
        o_ref[:, sl[p]] = ((yn + bonus) * g_ref[:, sl[p]]).astype(o_ref.dtype)


def _rw_scan(r, lw, k, v, kk, a, g, r_k, ln_g, ln_b, *, npairs):
    T, W = r.shape
    C = RW_CHUNK
    wb = npairs * LANES
    blk = pl.BlockSpec((C, wb), lambda p, c: (c, p))
    vec = pl.BlockSpec((1, wb), lambda p, c: (0, p))
    return pl.pallas_call(
        functools.partial(_rw_scan_kernel, npairs=npairs),
        grid=(W // wb, T // C),
        in_specs=[blk] * 7 + [vec] * 3,
        out_specs=blk,
        out_shape=jax.ShapeDtypeStruct((T, W), BF16),
        scratch_shapes=[pltpu.VMEM((npairs, LANES, LANES), F32)],
        compiler_params=_cparams(("parallel", "arbitrary")),
        name="rw_scan",
    )(r, lw, k, v, kk, a, g, r_k, ln_g, ln_b)


def _out_proj_kernel(x_ref, sb_ref, rw_ref, wa_ref, wb_ref, o_ref):
    o_ref[...] = (x_ref[...]
                  + jnp.dot(sb_ref[...], wa_ref[...], preferred_element_type=F32)
                  + jnp.dot(rw_ref[...], wb_ref[...], preferred_element_type=F32))


def _out_proj(x, sb, rw, w_out, *, tm):
    T, D = x.shape
    Wa = sb.shape[1]
    Wb = rw.shape[1]
    return pl.pallas_call(
        _out_proj_kernel,
        grid=(T // tm,),
        in_specs=[
            pl.BlockSpec((tm, D), lambda i: (i, 0)),
            pl.BlockSpec((tm, Wa), lambda i: (i, 0)),
            pl.BlockSpec((tm, Wb), lambda i: (i, 0)),
            pl.BlockSpec((Wa, D), lambda i: (0, 0)),
            pl.BlockSpec((Wb, D), lambda i: (1, 0)),
        ],
        out_specs=pl.BlockSpec((tm, D), lambda i: (i, 0)),
        out_shape=jax.ShapeDtypeStruct((T, D), F32),
        compiler_params=_cparams(("parallel",)),
        name="out_proj",
    )(x, sb, rw, w_out, w_out)


def _router_kernel(x_ref, g_ref, w_ref, b_ref, hn_ref, meta_ref, cnt_ref, carry_ref, *, tm):
    i = pl.program_id(0)

    @pl.when(i == 0)
    def _():
        carry_ref[...] = jnp.zeros_like(carry_ref)

    hn = _rms(x_ref[...]) * g_ref[...]
    hn_ref[...] = hn.astype(hn_ref.dtype)
    logits = _mm_f32(hn, w_ref[...]) + b_ref[...]
    lane = lax.broadcasted_iota(jnp.int32, logits.shape, 1).astype(F32)
    big = float(LANES)
    neg = -jnp.inf
    is_grp = (lane >= N_EXPERTS) & (lane < N_EXPERTS + N_GROUPS)
    gl = jnp.where(is_grp, logits, neg)
    gmax = jnp.max(gl, axis=-1, keepdims=True)
    g_w = 1.0 / jnp.sum(jnp.where(is_grp, jnp.exp(gl - gmax), 0.0), axis=-1, keepdims=True)
    grp = jnp.min(jnp.where(gl == gmax, lane, big), axis=-1, keepdims=True) - N_EXPERTS
    lo = grp * EXPERTS_PER_GROUP
    in_grp = (lane >= lo) & (lane < lo + EXPERTS_PER_GROUP)
    es = jnp.where(in_grp, logits, neg)
    m1 = jnp.max(es, axis=-1, keepdims=True)
    i1 = jnp.min(jnp.where(es == m1, lane, big), axis=-1, keepdims=True)
    es2 = jnp.where(lane == i1, neg, es)
    m2 = jnp.max(es2, axis=-1, keepdims=True)
    i2 = jnp.min(jnp.where(es2 == m2, lane, big), axis=-1, keepdims=True)
    z = jnp.sum(jnp.where(in_grp, jnp.exp(es - m1), 0.0), axis=-1, keepdims=True)
    p1 = 1.0 / z
    p2 = jnp.exp(m2 - m1) / z
    w1 = g_w * (p1 / (p1 + p2))
    w2 = g_w * (p2 / (p1 + p2))
    oh1 = lane == i1
    oh2 = lane == i2
    oh = (oh1 | oh2).astype(BF16)
    row = lax.broadcasted_iota(jnp.int32, (tm, tm), 0)
    col = lax.broadcasted_iota(jnp.int32, (tm, tm), 1)
    before = jnp.dot((col < row).astype(BF16), oh, preferred_element_type=F32) + carry_ref[0:1, :]
    rank1 = jnp.sum(jnp.where(oh1, before, 0.0), axis=-1, keepdims=True)
    rank2 = jnp.sum(jnp.where(oh2, before, 0.0), axis=-1, keepdims=True)
    carry_ref[...] = carry_ref[...] + jnp.sum(oh.astype(F32), axis=0, keepdims=True)
    meta = jnp.where(lane == 0, i1, 0.0)
    meta = jnp.where(lane == 1, i2, meta)
    meta = jnp.where(lane == 2, w1, meta)
    meta = jnp.where(lane == 3, w2, meta)
    meta = jnp.where(lane == 4, rank1, meta)
    meta = jnp.where(lane == 5, rank2, meta)
    meta_ref[...] = meta
    cnt_ref[...] = carry_ref[...]


def _router(x, g, w, b, *, tm):
    T, D = x.shape
    return pl.pallas_call(
        functools.partial(_router_kernel, tm=tm),
        grid=(T // tm,),
        in_specs=[
            pl.BlockSpec((tm, D), lambda i: (i, 0)),
            pl.BlockSpec((1, D), lambda i: (0, 0)),
            pl.BlockSpec((D, LANES), lambda i: (0, 0)),
            pl.BlockSpec((1, LANES), lambda i: (0, 0)),
        ],
        out_specs=[
            pl.BlockSpec((tm, D), lambda i: (i, 0)),
            pl.BlockSpec((tm, LANES), lambda i: (i, 0)),
            pl.BlockSpec((8, LANES), lambda i: (0, 0)),
        ],
        out_shape=[
            jax.ShapeDtypeStruct((T, D), F32),
            jax.ShapeDtypeStruct((T, LANES), F32),
            jax.ShapeDtypeStruct((8, LANES), F32),
        ],
        scratch_shapes=[pltpu.VMEM((8, LANES), F32)],
        compiler_params=_cparams(("arbitrary",)),
        name="router",
    )(x, g, w, b)


GATHER_UNROLL = 8


def _gather_rows_start(idx_ref, src_hbm, dst, sem):
    n = dst.shape[0]

    def body(r, carry):
        t = idx_ref[0, 0, r]
        pltpu.make_async_copy(src_hbm.at[pl.ds(t, 1), :], dst.at[pl.ds(r, 1), :], sem).start()
        return carry

    lax.fori_loop(0, n, body, 0, unroll=GATHER_UNROLL)


def _gather_rows_wait(src_hbm, dst, sem):
    pltpu.make_async_copy(src_hbm.at[pl.ds(0, dst.shape[0]), :], dst, sem).wait()


def _expert_kernel(be_ref, nused_ref, tok_ref, tok_next_ref, hn_hbm, wu_ref, wd_ref, o_ref, xbuf, sem):
    b = pl.program_id(0)
    n_used = nused_ref[0]
    slot = b % 2

    @pl.when(b == 0)
    def _():
        _gather_rows_start(tok_ref, hn_hbm, xbuf.at[0], sem.at[0])

    @pl.when(b + 1 < n_used)
    def _():
        _gather_rows_start(tok_next_ref, hn_hbm, xbuf.at[1 - slot], sem.at[1 - slot])

    @pl.when(b < n_used)
    def _():
        _gather_rows_wait(hn_hbm, xbuf.at[slot], sem.at[slot])
        h = jnp.dot(xbuf[slot].astype(BF16), wu_ref[0], preferred_element_type=F32)
        ff = h.shape[1] // 2
        gate = h[:, :ff]
        up = h[:, ff:]
        act = (gate * jax.nn.sigmoid(gate) * up).astype(BF16)
        o_ref[...] = jnp.dot(act, wd_ref[0], preferred_element_type=F32).astype(o_ref.dtype)

    @pl.when(b >= n_used)
    def _():
        o_ref[...] = jnp.zeros_like(o_ref)


def _expert_mlp(hn, row_tok, blk_expert, n_used, w_up, w_down):
    T, D = hn.shape
    nb, _, bm = row_tok.shape
    F2 = w_up.shape[2]
    grid_spec = pltpu.PrefetchScalarGridSpec(
        num_scalar_prefetch=2,
        grid=(nb,),
        in_specs=[
            pl.BlockSpec((1, 1, bm), lambda b, be, nu: (b, 0, 0), memory_space=pltpu.SMEM),
            pl.BlockSpec((1, 1, bm), lambda b, be, nu: (jnp.minimum(b + 1, nb - 1), 0, 0),
                         memory_space=pltpu.SMEM),
            pl.BlockSpec(memory_space=pl.ANY),
            pl.BlockSpec((1, D, F2), lambda b, be, nu: (be[b], 0, 0)),
            pl.BlockSpec((1, F2 // 2, D), lambda b, be, nu: (be[b], 0, 0)),
        ],
        out_specs=pl.BlockSpec((bm, D), lambda b, be, nu: (b, 0)),
        scratch_shapes=[pltpu.VMEM((2, bm, D), F32), pltpu.SemaphoreType.DMA((2,))],
    )
    return pl.pallas_call(
        _expert_kernel,
        grid_spec=grid_spec,
        out_shape=jax.ShapeDtypeStruct((nb * bm, D), F32),
        compiler_params=_cparams(("arbitrary",)),
        name="expert_mlp",
    )(blk_expert, n_used, row_tok, row_tok, hn, w_up, w_down)


def _combine_ple_kernel(d0_ref, d1_ref, d0n_ref, d1n_ref, y_hbm, x_ref, meta_ref, p_ref, wg_ref, wp_ref, g_ref,
                        o_ref, ybuf, sem):
    i = pl.program_id(0)
    n = pl.num_programs(0)
    slot = i % 2

    def start(s, a_ref, b_ref):
        _gather_rows_start(a_ref, y_hbm, ybuf.at[s, 0], sem.at[s])
        _gather_rows_start(b_ref, y_hbm, ybuf.at[s, 1], sem.at[s])

    @pl.when(i == 0)
    def _():
        start(0, d0_ref, d1_ref)

    @pl.when(i + 1 < n)
    def _():
        start(1 - slot, d0n_ref, d1n_ref)

    _gather_rows_wait(y_hbm, ybuf.at[slot, 0], sem.at[slot])
    _gather_rows_wait(y_hbm, ybuf.at[slot, 1], sem.at[slot])
    meta = meta_ref[...]
    x = x_ref[...] + meta[:, 2:3] * ybuf[slot, 0] + meta[:, 3:4] * ybuf[slot, 1]
    gate = jax.nn.sigmoid(jnp.dot(_rms(x).astype(BF16), wg_ref[...], preferred_element_type=F32))
    pe = jnp.dot(p_ref[...].astype(BF16), wp_ref[...], preferred_element_type=F32)
    o_ref[...] = x + _rms(pe * gate) * g_ref[...]


def _combine_ple(x, y_pad, dest0, dest1, meta, p, w_gate, w_proj, g):
    T, D = x.shape
    P = p.shape[1]
    nt, _, tm = dest0.shape
    row = lambda w: pl.BlockSpec((tm, w), lambda i: (i, 0))
    cur = pl.BlockSpec((1, 1, tm), lambda i: (i, 0, 0), memory_space=pltpu.SMEM)
    nxt = pl.BlockSpec((1, 1, tm), lambda i: (jnp.minimum(i + 1, nt - 1), 0, 0), memory_space=pltpu.SMEM)
    return pl.pallas_call(
        _combine_ple_kernel,
        grid=(nt,),
        in_specs=[cur, cur, nxt, nxt, pl.BlockSpec(memory_space=pl.ANY),
                  row(D), row(LANES), row(P),
                  pl.BlockSpec((D, D), lambda i: (0, 0)),
                  pl.BlockSpec((P, D), lambda i: (0, 0)),
                  pl.BlockSpec((1, D), lambda i: (0, 0))],
        out_specs=row(D),
        out_shape=jax.ShapeDtypeStruct((T, D), F32),
        scratch_shapes=[pltpu.VMEM((2, 2, tm, D), F32), pltpu.SemaphoreType.DMA((2,))],
        compiler_params=_cparams(("arbitrary",)),
        name="combine_ple",
    )(dest0, dest1, dest0, dest1, y_pad, x, meta, p, w_gate, w_proj, g)


def _moe_layout(meta, counts, T, tm):
    bm = MOE_ROWS
    n_rows = 2 * T + N_EXPERTS * bm
    nb = n_rows // bm
    e = meta[:, 0:2].astype(jnp.int32)
    rank = meta[:, 4:6].astype(jnp.int32)
    cnt = counts[0, :N_EXPERTS].astype(jnp.int32)
    padded = ((cnt + bm - 1) // bm) * bm
    pad_ends = jnp.cumsum(padded)
    pad_starts = pad_ends - padded
    dest = pad_starts[e] + rank
    tok = jnp.broadcast_to(jnp.arange(T, dtype=jnp.int32)[:, None], (T, 2))
    row_tok = jnp.zeros((n_rows,), jnp.int32).at[dest.reshape(-1)].set(tok.reshape(-1))
    blk_start = jnp.arange(nb, dtype=jnp.int32) * bm
    blk_expert = jnp.minimum(jnp.sum(pad_ends[None, :] <= blk_start[:, None], axis=1), N_EXPERTS - 1)
    n_used = (pad_ends[-1:] // bm).astype(jnp.int32)
    dest0 = dest[:, 0].reshape(T // tm, 1, tm)
    dest1 = dest[:, 1].reshape(T // tm, 1, tm)
    return dest0, dest1, row_tok.reshape(nb, 1, bm), blk_expert.astype(jnp.int32), n_used


def kernel(x, p, ln1_g, w_in, sb_q_g, sb_k_g, sb_out_g, rw_mu, rw_w0, rw_w2, rw_a0, rw_a2, rw_g2, rw_k_k, rw_k_a, rw_r_k, rw_ln_g, rw_ln_b, w_out, ln2_g, router_g, router_g_b, router_e, router_e_b, w_up, w_down, ple_proj, ple_gate, ple_norm_g):
    B, S, D = x.shape
    depth = w_in.shape[0]
    T = B * S
    assert B == 1, "token shift and attention assume one sequence"
    sb_width = D // 2
    rw_width = D - sb_width
    n_sb_heads = sb_width // SB_HEAD_DIM
    lora = LORA_W + LORA_A + LORA_G
    col0 = 3 * sb_width
    row2 = lambda a: a.reshape(1, -1)

    xf = x.reshape(T, D)
    for l in range(depth):
        w_in_l = jnp.pad(w_in[l], ((0, 0), (0, LORA_PAD - lora))).astype(BF16)
        proj = _norm_matmul(xf, row2(ln1_g[l]), w_in_l, tm=min(1024, T), tn=512, out_dtype=F32, name="in_proj")
        sb = _sb_attention(proj, row2(sb_q_g[l]), row2(sb_k_g[l]), row2(sb_out_g[l]),
                           n_heads=n_sb_heads, tq=256)
        w2 = jnp.pad(rw_w2[l], ((0, LORA_A), (0, 0)))
        a2 = jnp.pad(rw_a2[l], ((LORA_W, 0), (0, 0)))
        g2 = jnp.pad(rw_g2[l], ((0, 2 * LANES - LORA_G), (0, 0)))
        r, lw, k, v, kk, a, g = _rw_prep(
            proj, row2(rw_mu[l]), row2(rw_w0[l]), w2, row2(rw_a0[l]), a2, g2,
            row2(rw_k_k[l]), row2(rw_k_a[l]), rw_width=rw_width, col0=col0, tm=256)
        rw = _rw_scan(r, lw, k, v, kk, a, g, row2(rw_r_k[l]), row2(rw_ln_g[l]), row2(rw_ln_b[l]), npairs=8)
        x1 = _out_proj(xf, sb, rw, w_out[l].astype(BF16), tm=256)

        w_r = jnp.zeros((D, LANES), F32)
        w_r = w_r.at[:, :N_EXPERTS].set(router_e[l]).at[:, N_EXPERTS:N_EXPERTS + N_GROUPS].set(router_g[l])
        b_r = jnp.zeros((1, LANES), F32)
        b_r = b_r.at[0, :N_EXPERTS].set(router_e_b[l]).at[0, N_EXPERTS:N_EXPERTS + N_GROUPS].set(router_g_b[l])
        hn, meta, counts = _router(x1, row2(ln2_g[l]), w_r, b_r, tm=256)
        dest0, dest1, row_tok, blk_expert, n_used = _moe_layout(meta, counts, T, 256)
        y_pad = _expert_mlp(hn, row_tok, blk_expert, n_used, w_up[l].astype(BF16), w_down[l].astype(BF16))
        xf = _combine_ple(x1, y_pad, dest0, dest1, meta, p[l].reshape(T, -1), ple_gate[l].astype(BF16),
                          ple_proj[l].astype(BF16), row2(ple_norm_g[l]))
    return xf.reshape(B, S, D)
```

```python
import functools
import math

import jax
import jax.numpy as jnp
from jax import lax
from jax.experimental import pallas as pl
from jax.experimental.pallas import tpu as pltpu

F32 = jnp.float32
BF16 = jnp.bfloat16

SB_HEAD_DIM = 128
RW_HEAD_DIM = 64
LORA_W = 64
LORA_A = 64
LORA_G = 160
LORA_PAD = 512
N_GROUPS = 4
EXPERTS_PER_GROUP = 8
N_EXPERTS = N_GROUPS * EXPERTS_PER_GROUP
RMS_EPS = 1e-6
GN_EPS = 64e-5
LANES = 128
SUBLANES = 8
VMEM_LIMIT = 56 * 1024 * 1024

EXP_UNDERFLOW = -104.0
RW_CHUNK = 64
MOE_ROWS = 256
TOKEN_TILE = 256
GATHER_UNROLL = 8


def _cparams(sem):
    return pltpu.CompilerParams(dimension_semantics=sem, vmem_limit_bytes=VMEM_LIMIT)


def _rms(x, eps=RMS_EPS):
    return x * lax.rsqrt(jnp.mean(x * x, axis=-1, keepdims=True) + eps)


def _softplus(y):
    return jnp.maximum(y, 0.0) + jnp.log(1.0 + jnp.exp(-jnp.abs(y)))


def _mm(a, b):
    return jnp.dot(a.astype(BF16), b.astype(BF16), preferred_element_type=F32)


def _mm_nt(a, b):
    return lax.dot_general(a.astype(BF16), b.astype(BF16), (((1,), (1,)), ((), ())),
                           preferred_element_type=F32)


def _mm_tn(a, b):
    return lax.dot_general(a.astype(BF16), b.astype(BF16), (((0,), (0,)), ((), ())),
                           preferred_element_type=F32)


def _mm_f32(a, b):
    return jnp.dot(a, b, preferred_element_type=F32, precision=lax.Precision.HIGHEST)


def _mm_split(a, b_exact):
    hi = a.astype(BF16)
    lo = (a - hi.astype(F32)).astype(BF16)
    return (jnp.dot(hi, b_exact, preferred_element_type=F32)
            + jnp.dot(lo, b_exact, preferred_element_type=F32))


def _layer_vec(l, width):
    return pl.BlockSpec((None, 1, width), lambda *_: (l, 0, 0))


def _layer_mat(l, shape):
    return pl.BlockSpec((None,) + tuple(shape), lambda *_: (l,) + (0,) * len(shape))


def _norm_matmul_kernel(x_ref, g_ref, w_ref, o_ref, xn_ref):
    @pl.when(pl.program_id(1) == 0)
    def _():
        xn_ref[...] = (_rms(x_ref[...]) * g_ref[...]).astype(BF16)

    o_ref[...] = jnp.dot(xn_ref[...], w_ref[...], preferred_element_type=F32).astype(o_ref.dtype)


def _norm_matmul(x, g, w, l, *, tm, tn, out_dtype, name):
    T, D = x.shape
    N = w.shape[2]
    return pl.pallas_call(
        _norm_matmul_kernel,
        grid=(T // tm, N // tn),
        in_specs=[
            pl.BlockSpec((tm, D), lambda i, j: (i, 0)),
            _layer_vec(l, D),
            pl.BlockSpec((None, D, tn), lambda i, j: (l, 0, j)),
        ],
        out_specs=pl.BlockSpec((tm, tn), lambda i, j: (i, j)),
        out_shape=jax.ShapeDtypeStruct((T, N), out_dtype),
        scratch_shapes=[pltpu.VMEM((tm, D), BF16)],
        compiler_params=_cparams(("parallel", "arbitrary")),
        name=name,
    )(x, g, w)


def _sb_attn_kernel(q_ref, k_ref, v_ref, qg_ref, kg_ref, og_ref, o_ref, kn_ref, vb_ref, *, tq, seq):
    i = pl.program_id(1)
    hd = SB_HEAD_DIM
    prep_rows = min(512, seq)

    @pl.when(i == 0)
    def _():
        def body(c, carry):
            rows = pl.ds(pl.multiple_of(c * prep_rows, prep_rows), prep_rows)
            kn_ref[rows, :] = (_rms(k_ref[rows, :]) * kg_ref[...]).astype(BF16)
            vb_ref[rows, :] = v_ref[rows, :].astype(BF16)
            return carry

        lax.fori_loop(0, seq // prep_rows, body, 0)

    qn = (_rms(q_ref[...]) * qg_ref[...] * (1.0 / math.sqrt(hd))).astype(BF16)
    row = lax.broadcasted_iota(jnp.int32, (tq, tq), 0)
    col = lax.broadcasted_iota(jnp.int32, (tq, tq), 1)
    tri = (row > col).astype(BF16)
    causal = col < row

    def tiles(js, masked, carry, acc):
        rows = [pl.ds(pl.multiple_of(j * tq, tq), tq) for j in js]
        z = [_mm_nt(qn, kn_ref[r, :]) for r in rows]
        sp = [_softplus(zt) for zt in z]
        log_keep = [jnp.where(causal, -s, 0.0) if m else -s for s, m in zip(sp, masked)]
        log_beta = [zt - s for zt, s in zip(z, sp)]
        later = [_mm_split(lk, tri) for lk in log_keep]
        for t in range(len(js)):
            w = jnp.exp(log_beta[t] + later[t] + carry)
            if masked[t]:
                w = jnp.where(causal, w, 0.0)
            acc = acc + jnp.dot(w.astype(BF16), vb_ref[rows[t], :], preferred_element_type=F32)
            carry = carry + later[t][:, 0:1] + log_keep[t][:, 0:1]
        return carry, acc

    def live(carry):
        return (jnp.max(carry) > EXP_UNDERFLOW).astype(jnp.int32)

    zeros = (jnp.zeros((tq, 1), F32), jnp.zeros((tq, hd), F32))
    carry, acc = lax.cond(i > 0,
                          lambda: tiles([i, i - 1], [True, False], *zeros),
                          lambda: tiles([i], [True], *zeros))

    def cond(st):
        return (st[0] < i) & (st[1] > 0)

    def body(st):
        n, _, carry, acc = st
        carry, acc = tiles([i - 1 - n], [False], carry, acc)
        return n + 1, live(carry), carry, acc

    _, _, carry, acc = lax.while_loop(cond, body, (jnp.int32(1), live(carry), carry, acc))
    o_ref[...] = (_rms(acc) * og_ref[...]).astype(o_ref.dtype)


def _sb_attention(proj, q_g, k_g, out_g, l, *, n_heads, tq):
    T = proj.shape[0]
    hd = SB_HEAD_DIM
    return pl.pallas_call(
        functools.partial(_sb_attn_kernel, tq=tq, seq=T),
        grid=(n_heads, T // tq),
        in_specs=[
            pl.BlockSpec((tq, hd), lambda h, i: (i, h)),
            pl.BlockSpec((T, hd), lambda h, i: (0, n_heads + h)),
            pl.BlockSpec((T, hd), lambda h, i: (0, 2 * n_heads + h)),
            _layer_vec(l, hd), _layer_vec(l, hd), _layer_vec(l, hd),
        ],
        out_specs=pl.BlockSpec((tq, hd), lambda h, i: (i, h)),
        out_shape=jax.ShapeDtypeStruct((T, n_heads * hd), BF16),
        scratch_shapes=[pltpu.VMEM((T, hd), BF16), pltpu.VMEM((T, hd), BF16)],
        compiler_params=_cparams(("parallel", "arbitrary")),
        name="sb_attention",
    )(proj, proj, proj, q_g, k_g, out_g)


def _rw_prep_kernel(r_ref, k_ref, v_ref, lo_ref, rp_ref, kp_ref, vp_ref, lop_ref,
                    mu_r_ref, mu_k_ref, mu_v_ref, mu_lo_ref, w0_ref, w2_ref, a0_ref, a2_ref, g2_ref,
                    kk_ref, ka_ref,
                    r_out, lw_out, k_out, v_out, kk_out, a_out, g_out):
    first = pl.program_id(0) == 0

    def shift(cur_ref, prev_ref, mu_ref):
        cur = cur_ref[...]
        prev_row = jnp.where(first, 0.0, prev_ref[SUBLANES - 1:SUBLANES, :])
        rolled = pltpu.roll(cur, 1, 0)
        rowi = lax.broadcasted_iota(jnp.int32, cur.shape, 0)
        prev = jnp.where(rowi == 0, prev_row, rolled)
        return cur + mu_ref[...] * (prev - cur)

    r = shift(r_ref, rp_ref, mu_r_ref)
    k = shift(k_ref, kp_ref, mu_k_ref)
    v = shift(v_ref, vp_ref, mu_v_ref)
    lo = shift(lo_ref, lop_ref, mu_lo_ref)
    wa_lo = lo[:, :LORA_W + LORA_A]
    g_lo = lo[:, LORA_W + LORA_A:LORA_W + LORA_A + g2_ref.shape[0]]
    w = -_softplus(-(w0_ref[...] + _mm_f32(jnp.tanh(wa_lo), w2_ref[...]))) - 0.5
    a = jax.nn.sigmoid(a0_ref[...] + _mm_f32(wa_lo, a2_ref[...]))
    g = _mm_f32(jax.nn.sigmoid(g_lo), g2_ref[...])
    r_out[...] = r
    lw_out[...] = -jnp.exp(w)
    k_out[...] = k * (1.0 + (a - 1.0) * ka_ref[...])
    v_out[...] = v
    kk_out[...] = k * kk_ref[...]
    a_out[...] = a
    g_out[...] = g


def _rw_prep(proj, mu_r, mu_k, mu_v, mu_lo, w0, w2, a0, a2, g2, k_k, k_a, l, *, rw_width, col0, tm):
    T = proj.shape[0]
    W = rw_width
    cb = col0 // W
    lb = (col0 + 3 * W) // LORA_PAD

    def cur(width, blk):
        return pl.BlockSpec((tm, width), lambda i: (i, blk))

    def prev(width, blk):
        return pl.BlockSpec((SUBLANES, width), lambda i: (jnp.maximum(i * (tm // SUBLANES) - 1, 0), blk))

    out = jax.ShapeDtypeStruct((T, W), F32)
    return pl.pallas_call(
        _rw_prep_kernel,
        grid=(T // tm,),
        in_specs=[cur(W, cb), cur(W, cb + 1), cur(W, cb + 2), cur(LORA_PAD, lb),
                  prev(W, cb), prev(W, cb + 1), prev(W, cb + 2), prev(LORA_PAD, lb),
                  _layer_vec(l, W), _layer_vec(l, W), _layer_vec(l, W), _layer_vec(l, LORA_PAD),
                  _layer_vec(l, W), _layer_mat(l, w2.shape[1:]), _layer_vec(l, W), _layer_mat(l, a2.shape[1:]),
                  _layer_mat(l, g2.shape[1:]), _layer_vec(l, W), _layer_vec(l, W)],
        out_specs=[pl.BlockSpec((tm, W), lambda i: (i, 0))] * 7,
        out_shape=[out] * 7,
        compiler_params=_cparams(("parallel",)),
        name="rw_prep",
    )(proj, proj, proj, proj, proj, proj, proj, proj,
      mu_r, mu_k, mu_v, mu_lo, w0, w2, a0, a2, g2, k_k, k_a)


def _cumsum_rows(x, tril_bf16):
    hi = x.astype(BF16)
    lo = (x - hi.astype(F32)).astype(BF16)
    return (jnp.dot(tril_bf16, hi, preferred_element_type=F32)
            + jnp.dot(tril_bf16, lo, preferred_element_type=F32))


def _rw_scan_kernel(r_ref, lw_ref, k_ref, v_ref, kk_ref, a_ref, g_ref, rk_ref, lng_ref, lnb_ref,
                    o_ref, state_ref, *, npairs):
    C = RW_CHUNK
    N = RW_HEAD_DIM
    C2 = 2 * C
    assert LANES == 2 * N and C2 == LANES

    @pl.when(pl.program_id(1) == 0)
    def _():
        state_ref[...] = jnp.zeros_like(state_ref)

    row = lax.broadcasted_iota(jnp.int32, (C2, C2), 0)
    col = lax.broadcasted_iota(jnp.int32, (C2, C2), 1)
    lower_incl = col <= row
    lower_strict = col < row
    eye = (row == col).astype(F32)
    seg_ones = ((row < N) == (col < N)).astype(BF16)
    tril = (lax.broadcasted_iota(jnp.int32, (C, C), 1) <= lax.broadcasted_iota(jnp.int32, (C, C), 0)).astype(BF16)
    head0 = lax.broadcasted_iota(jnp.int32, (C, LANES), 1) < N

    def stack(x):
        return jnp.concatenate([jnp.where(head0, x, 0.0), jnp.where(head0, 0.0, x)], axis=0)

    def seg_sum(x):
        return _mm(x, seg_ones)

    P = range(npairs)
    sl = [slice(p * LANES, (p + 1) * LANES) for p in P]
    r = [r_ref[:, sl[p]] for p in P]
    lw = [lw_ref[:, sl[p]] for p in P]
    k = [k_ref[:, sl[p]] for p in P]
    v = [v_ref[:, sl[p]] for p in P]
    kk = [kk_ref[:, sl[p]] for p in P]
    a = [a_ref[:, sl[p]] for p in P]
    kkn = [kk[p] * lax.rsqrt(seg_sum(kk[p] * kk[p]) + 1e-12) for p in P]
    cum = [_cumsum_rows(lw[p], tril) for p in P]
    inv = [jnp.exp(-cum[p]) for p in P]
    a2 = [stack(-kkn[p] * jnp.exp(cum[p] - lw[p])).astype(BF16) for p in P]
    r2 = [stack(r[p] * jnp.exp(cum[p])).astype(BF16) for p in P]
    b2 = [stack(kkn[p] * a[p] * inv[p]).astype(BF16) for p in P]
    k2 = [stack(k[p] * inv[p]).astype(BF16) for p in P]
    v2 = [stack(v[p]).astype(BF16) for p in P]
    l_ab = [jnp.where(lower_strict, _mm_nt(a2[p], b2[p]), 0.0) for p in P]
    l_ak = [jnp.where(lower_strict, _mm_nt(a2[p], k2[p]), 0.0).astype(BF16) for p in P]
    m_rb = [jnp.where(lower_incl, _mm_nt(r2[p], b2[p]), 0.0).astype(BF16) for p in P]
    m_rk = [jnp.where(lower_incl, _mm_nt(r2[p], k2[p]), 0.0).astype(BF16) for p in P]
    tinv = [eye + l_ab[p] for p in P]
    pw = [l_ab[p].astype(BF16) for p in P]
    for _ in range(int(math.log2(C)) - 1):
        pw = [_mm(pw[p], pw[p]).astype(BF16) for p in P]
        tinv = [tinv[p] + _mm(tinv[p], pw[p]) for p in P]
    s0 = [state_ref[p] for p in P]
    s0b = [s0[p].astype(BF16) for p in P]
    rhs = [_mm_nt(a2[p], s0b[p]) + _mm(l_ak[p], v2[p]) for p in P]
    u2 = [_mm(tinv[p], rhs[p]).astype(BF16) for p in P]
    y2 = [_mm_nt(r2[p], s0b[p]) + _mm(m_rb[p], u2[p]) + _mm(m_rk[p], v2[p]) for p in P]
    for p in P:
        upd = _mm_tn(jnp.concatenate([u2[p], v2[p]], axis=0), jnp.concatenate([b2[p], k2[p]], axis=0))
        state_ref[p] = (s0[p] + upd) * jnp.exp(cum[p][C - 1:C, :])
    for p in P:
        y = y2[p][:C] + y2[p][C:]
        yc = y - seg_sum(y) * (1.0 / N)
        var = seg_sum(yc * yc) * (1.0 / N)
        yn = yc * lax.rsqrt(var + GN_EPS) * lng_ref[:, sl[p]] + lnb_ref[:, sl[p]]
        bonus = seg_sum(r[p] * k[p] * rk_ref[:, sl[p]]) * v[p]
        o_ref[:, sl[p]] = ((yn + bonus) * g_ref[:, sl[p]]).astype(o_ref.dtype)


def _rw_scan(r, lw, k, v, kk, a, g, r_k, ln_g, ln_b, l, *, npairs):
    T, W = r.shape
    C = RW_CHUNK
    wb = npairs * LANES
    blk = pl.BlockSpec((C, wb), lambda p, c: (c, p))
    vec = pl.BlockSpec((None, 1, wb), lambda p, c: (l, 0, p))
    return pl.pallas_call(
        functools.partial(_rw_scan_kernel, npairs=npairs),
        grid=(W // wb, T // C),
        in_specs=[blk] * 7 + [vec] * 3,
        out_specs=blk,
        out_shape=jax.ShapeDtypeStruct((T, W), BF16),
        scratch_shapes=[pltpu.VMEM((npairs, LANES, LANES), F32)],
        compiler_params=_cparams(("parallel", "arbitrary")),
        name="rw_scan",
    )(r, lw, k, v, kk, a, g, r_k, ln_g, ln_b)


def _out_proj_kernel(x_ref, sb_ref, rw_ref, wa_ref, wb_ref, o_ref):
    o_ref[...] = (x_ref[...]
                  + jnp.dot(sb_ref[...], wa_ref[...], preferred_element_type=F32)
                  + jnp.dot(rw_ref[...], wb_ref[...], preferred_element_type=F32))


def _out_proj(x, sb, rw, w_out, l, *, tm):
    T, D = x.shape
    Wa = sb.shape[1]
    Wb = rw.shape[1]
    assert Wa == Wb
    return pl.pallas_call(
        _out_proj_kernel,
        grid=(T // tm,),
        in_specs=[
            pl.BlockSpec((tm, D), lambda i: (i, 0)),
            pl.BlockSpec((tm, Wa), lambda i: (i, 0)),
            pl.BlockSpec((tm, Wb), lambda i: (i, 0)),
            pl.BlockSpec((None, Wa, D), lambda i: (l, 0, 0)),
            pl.BlockSpec((None, Wb, D), lambda i: (l, 1, 0)),
        ],
        out_specs=pl.BlockSpec((tm, D), lambda i: (i, 0)),
        out_shape=jax.ShapeDtypeStruct((T, D), F32),
        compiler_params=_cparams(("parallel",)),
        name="out_proj",
    )(x, sb, rw, w_out, w_out)


def _router_kernel(x_ref, g_ref, w_ref, b_ref, hn_ref, meta_ref, cnt_ref, carry_ref, *, tm):
    i = pl.program_id(0)

    @pl.when(i == 0)
    def _():
        carry_ref[...] = jnp.zeros_like(carry_ref)

    hn = _rms(x_ref[...]) * g_ref[...]
    hn_ref[...] = hn.astype(hn_ref.dtype)
    logits = _mm_f32(hn, w_ref[...]) + b_ref[...]
    lane = lax.broadcasted_iota(jnp.int32, logits.shape, 1).astype(F32)
    big = float(LANES)
    neg = -jnp.inf
    is_grp = (lane >= N_EXPERTS) & (lane < N_EXPERTS + N_GROUPS)
    gl = jnp.where(is_grp, logits, neg)
    gmax = jnp.max(gl, axis=-1, keepdims=True)
    g_w = 1.0 / jnp.sum(jnp.where(is_grp, jnp.exp(gl - gmax), 0.0), axis=-1, keepdims=True)
    grp = jnp.min(jnp.where(gl == gmax, lane, big), axis=-1, keepdims=True) - N_EXPERTS
    lo = grp * EXPERTS_PER_GROUP
    in_grp = (lane >= lo) & (lane < lo + EXPERTS_PER_GROUP)
    es = jnp.where(in_grp, logits, neg)
    m1 = jnp.max(es, axis=-1, keepdims=True)
    i1 = jnp.min(jnp.where(es == m1, lane, big), axis=-1, keepdims=True)
    es2 = jnp.where(lane == i1, neg, es)
    m2 = jnp.max(es2, axis=-1, keepdims=True)
    i2 = jnp.min(jnp.where(es2 == m2, lane, big), axis=-1, keepdims=True)
    z = jnp.sum(jnp.where(in_grp, jnp.exp(es - m1), 0.0), axis=-1, keepdims=True)
    p1 = 1.0 / z
    p2 = jnp.exp(m2 - m1) / z
    w1 = g_w * (p1 / (p1 + p2))
    w2 = g_w * (p2 / (p1 + p2))
    oh1 = lane == i1
    oh2 = lane == i2
    oh = (oh1 | oh2).astype(BF16)
    row = lax.broadcasted_iota(jnp.int32, (tm, tm), 0)
    col = lax.broadcasted_iota(jnp.int32, (tm, tm), 1)
    before = jnp.dot((col < row).astype(BF16), oh, preferred_element_type=F32) + carry_ref[0:1, :]
    rank1 = jnp.sum(jnp.where(oh1, before, 0.0), axis=-1, keepdims=True)
    rank2 = jnp.sum(jnp.where(oh2, before, 0.0), axis=-1, keepdims=True)
    carry_ref[...] = carry_ref[...] + jnp.sum(oh.astype(F32), axis=0, keepdims=True)
    meta = jnp.where(lane == 0, i1, 0.0)
    meta = jnp.where(lane == 1, i2, meta)
    meta = jnp.where(lane == 2, w1, meta)
    meta = jnp.where(lane == 3, w2, meta)
    meta = jnp.where(lane == 4, rank1, meta)
    meta = jnp.where(lane == 5, rank2, meta)
    meta_ref[...] = meta
    cnt_ref[...] = carry_ref[...]


def _router(x, g, w, b, l, *, tm):
    T, D = x.shape
    return pl.pallas_call(
        functools.partial(_router_kernel, tm=tm),
        grid=(T // tm,),
        in_specs=[
            pl.BlockSpec((tm, D), lambda i: (i, 0)),
            _layer_vec(l, D),
            _layer_mat(l, (D, LANES)),
            _layer_vec(l, LANES),
        ],
        out_specs=[
            pl.BlockSpec((tm, D), lambda i: (i, 0)),
            pl.BlockSpec((tm, LANES), lambda i: (i, 0)),
            pl.BlockSpec((SUBLANES, LANES), lambda i: (0, 0)),
        ],
        out_shape=[
            jax.ShapeDtypeStruct((T, D), F32),
            jax.ShapeDtypeStruct((T, LANES), F32),
            jax.ShapeDtypeStruct((SUBLANES, LANES), F32),
        ],
        scratch_shapes=[pltpu.VMEM((SUBLANES, LANES), F32)],
        compiler_params=_cparams(("arbitrary",)),
        name="router",
    )(x, g, w, b)


def _gather_rows_start(idx_ref, src_hbm, dst, sem):
    n = dst.shape[0]

    def body(r, carry):
        t = idx_ref[0, 0, r]
        pltpu.make_async_copy(src_hbm.at[pl.ds(t, 1), :], dst.at[pl.ds(r, 1), :], sem).start()
        return carry

    lax.fori_loop(0, n, body, 0, unroll=GATHER_UNROLL)


def _gather_rows_wait(src_hbm, dst, sem):
    pltpu.make_async_copy(src_hbm.at[pl.ds(0, dst.shape[0]), :], dst, sem).wait()


def _expert_kernel(be_ref, nused_ref, tok_ref, tok_next_ref, hn_hbm, wu_ref, wd_ref, o_ref,
                   xbuf, sem, wu_bf, wd_bf):
    b = pl.program_id(0)
    n_used = nused_ref[0]
    slot = b % 2
    cast_rows = 256

    @pl.when(b == 0)
    def _():
        _gather_rows_start(tok_ref, hn_hbm, xbuf.at[0], sem.at[0])

    @pl.when(b + 1 < n_used)
    def _():
        _gather_rows_start(tok_next_ref, hn_hbm, xbuf.at[1 - slot], sem.at[1 - slot])

    @pl.when((b < n_used) & ((b == 0) | (be_ref[b] != be_ref[jnp.maximum(b - 1, 0)])))
    def _():
        def cast(ref, dst, c):
            rows = pl.ds(pl.multiple_of(c * cast_rows, cast_rows), cast_rows)
            dst[rows, :] = ref[rows, :].astype(BF16)

        def up(c, carry):
            cast(wu_ref, wu_bf, c)
            return carry

        def down(c, carry):
            cast(wd_ref, wd_bf, c)
            return carry

        lax.fori_loop(0, wu_bf.shape[0] // cast_rows, up, 0)
        lax.fori_loop(0, wd_bf.shape[0] // cast_rows, down, 0)

    @pl.when(b < n_used)
    def _():
        _gather_rows_wait(hn_hbm, xbuf.at[slot], sem.at[slot])
        h = jnp.dot(xbuf[slot].astype(BF16), wu_bf[...], preferred_element_type=F32)
        ff = h.shape[1] // 2
        gate = h[:, :ff]
        up = h[:, ff:]
        act = (gate * jax.nn.sigmoid(gate) * up).astype(BF16)
        o_ref[...] = jnp.dot(act, wd_bf[...], preferred_element_type=F32).astype(o_ref.dtype)

    @pl.when(b >= n_used)
    def _():
        o_ref[...] = jnp.zeros_like(o_ref)


def _expert_mlp(hn, row_tok, blk_expert, n_used, w_up, w_down, l):
    T, D = hn.shape
    nb, _, bm = row_tok.shape
    F2 = w_up.shape[3]
    grid_spec = pltpu.PrefetchScalarGridSpec(
        num_scalar_prefetch=2,
        grid=(nb,),
        in_specs=[
            pl.BlockSpec((1, 1, bm), lambda b, be, nu: (b, 0, 0), memory_space=pltpu.SMEM),
            pl.BlockSpec((1, 1, bm), lambda b, be, nu: (jnp.minimum(b + 1, nb - 1), 0, 0),
                         memory_space=pltpu.SMEM),
            pl.BlockSpec(memory_space=pl.ANY),
            pl.BlockSpec((None, None, D, F2), lambda b, be, nu: (l, be[b], 0, 0)),
            pl.BlockSpec((None, None, F2 // 2, D), lambda b, be, nu: (l, be[b], 0, 0)),
        ],
        out_specs=pl.BlockSpec((bm, D), lambda b, be, nu: (b, 0)),
        scratch_shapes=[pltpu.VMEM((2, bm, D), F32), pltpu.SemaphoreType.DMA((2,)),
                        pltpu.VMEM((D, F2), BF16), pltpu.VMEM((F2 // 2, D), BF16)],
    )
    return pl.pallas_call(
        _expert_kernel,
        grid_spec=grid_spec,
        out_shape=jax.ShapeDtypeStruct((nb * bm, D), F32),
        compiler_params=_cparams(("arbitrary",)),
        name="expert_mlp",
    )(blk_expert, n_used, row_tok, row_tok, hn, w_up, w_down)


def _combine_ple_kernel(d0_ref, d1_ref, d0n_ref, d1n_ref, y_hbm, x_ref, meta_ref, p_ref, wg_ref, wp_ref, g_ref,
                        o_ref, ybuf, sem):
    i = pl.program_id(0)
    n = pl.num_programs(0)
    slot = i % 2

    def start(s, a_ref, b_ref):
        _gather_rows_start(a_ref, y_hbm, ybuf.at[s, 0], sem.at[s])
        _gather_rows_start(b_ref, y_hbm, ybuf.at[s, 1], sem.at[s])

    @pl.when(i == 0)
    def _():
        start(0, d0_ref, d1_ref)

    @pl.when(i + 1 < n)
    def _():
        start(1 - slot, d0n_ref, d1n_ref)

    _gather_rows_wait(y_hbm, ybuf.at[slot, 0], sem.at[slot])
    _gather_rows_wait(y_hbm, ybuf.at[slot, 1], sem.at[slot])
    meta = meta_ref[...]
    x = x_ref[...] + meta[:, 2:3] * ybuf[slot, 0] + meta[:, 3:4] * ybuf[slot, 1]
    gate = jax.nn.sigmoid(jnp.dot(_rms(x).astype(BF16), wg_ref[...], preferred_element_type=F32))
    pe = jnp.dot(p_ref[...].astype(BF16), wp_ref[...], preferred_element_type=F32)
    o_ref[...] = x + _rms(pe * gate) * g_ref[...]


def _combine_ple(x, y_pad, dest0, dest1, meta, p, w_gate, w_proj, g, l):
    T, D = x.shape
    P = p.shape[2]
    nt, _, tm = dest0.shape
    row = lambda w: pl.BlockSpec((tm, w), lambda i: (i, 0))
    cur = pl.BlockSpec((1, 1, tm), lambda i: (i, 0, 0), memory_space=pltpu.SMEM)
    nxt = pl.BlockSpec((1, 1, tm), lambda i: (jnp.minimum(i + 1, nt - 1), 0, 0), memory_space=pltpu.SMEM)
    return pl.pallas_call(
        _combine_ple_kernel,
        grid=(nt,),
        in_specs=[cur, cur, nxt, nxt, pl.BlockSpec(memory_space=pl.ANY),
                  row(D), row(LANES),
                  pl.BlockSpec((None, tm, P), lambda i: (l, i, 0)),
                  _layer_mat(l, (D, D)), _layer_mat(l, (P, D)), _layer_vec(l, D)],
        out_specs=row(D),
        out_shape=jax.ShapeDtypeStruct((T, D), F32),
        scratch_shapes=[pltpu.VMEM((2, 2, tm, D), F32), pltpu.SemaphoreType.DMA((2,))],
        compiler_params=_cparams(("arbitrary",)),
        name="combine_ple",
    )(dest0, dest1, dest0, dest1, y_pad, x, meta, p, w_gate, w_proj, g)


def _moe_layout(meta, counts, T, tm):
    bm = MOE_ROWS
    n_rows = 2 * T + N_EXPERTS * bm
    nb = n_rows // bm
    e = meta[:, 0:2].astype(jnp.int32)
    rank = meta[:, 4:6].astype(jnp.int32)
    cnt = counts[0, :N_EXPERTS].astype(jnp.int32)
    padded = ((cnt + bm - 1) // bm) * bm
    pad_ends = jnp.cumsum(padded)
    pad_starts = pad_ends - padded
    dest = pad_starts[e] + rank
    tok = jnp.broadcast_to(jnp.arange(T, dtype=jnp.int32)[:, None], (T, 2))
    row_tok = jnp.zeros((n_rows,), jnp.int32).at[dest.reshape(-1)].set(tok.reshape(-1))
    blk_start = jnp.arange(nb, dtype=jnp.int32) * bm
    blk_expert = jnp.minimum(jnp.sum(pad_ends[None, :] <= blk_start[:, None], axis=1), N_EXPERTS - 1)
    n_used = (pad_ends[-1:] // bm).astype(jnp.int32)
    dest0 = dest[:, 0].reshape(T // tm, 1, tm)
    dest1 = dest[:, 1].reshape(T // tm, 1, tm)
    return dest0, dest1, row_tok.reshape(nb, 1, bm), blk_expert.astype(jnp.int32), n_used


def kernel(x, p, ln1_g, w_in, sb_q_g, sb_k_g, sb_out_g, rw_mu, rw_w0, rw_w2, rw_a0, rw_a2, rw_g2, rw_k_k, rw_k_a, rw_r_k, rw_ln_g, rw_ln_b, w_out, ln2_g, router_g, router_g_b, router_e, router_e_b, w_up, w_down, ple_proj, ple_gate, ple_norm_g):
    B, S, D = x.shape
    depth = w_in.shape[0]
    T = B * S
    assert B == 1, "token shift and attention assume one sequence"
    sb_width = D // 2
    rw_width = D - sb_width
    n_sb_heads = sb_width // SB_HEAD_DIM
    lora = LORA_W + LORA_A + LORA_G
    col0 = 3 * sb_width
    W = rw_width
    tm = min(TOKEN_TILE, T)
    vec = lambda a: a.reshape(depth, 1, -1)

    w_in_b = jnp.pad(w_in, ((0, 0), (0, 0), (0, LORA_PAD - lora))).astype(BF16)
    w_out_b = w_out.astype(BF16)
    ple_gate_b = ple_gate.astype(BF16)
    ple_proj_b = ple_proj.astype(BF16)
    mu = vec(rw_mu)
    mu_r, mu_k, mu_v = mu[:, :, :W], mu[:, :, W:2 * W], mu[:, :, 2 * W:3 * W]
    mu_lo = jnp.pad(mu[:, :, 3 * W:], ((0, 0), (0, 0), (0, LORA_PAD - lora)))
    w2 = jnp.pad(rw_w2, ((0, 0), (0, LORA_A), (0, 0)))
    a2 = jnp.pad(rw_a2, ((0, 0), (LORA_W, 0), (0, 0)))
    g2 = jnp.pad(rw_g2, ((0, 0), (0, 2 * LANES - LORA_G), (0, 0)))
    w_r = jnp.pad(jnp.concatenate([router_e, router_g], axis=2),
                  ((0, 0), (0, 0), (0, LANES - N_EXPERTS - N_GROUPS)))
    b_r = vec(jnp.pad(jnp.concatenate([router_e_b, router_g_b], axis=1),
                      ((0, 0), (0, LANES - N_EXPERTS - N_GROUPS))))
    p3 = p.reshape(depth, T, -1)

    xf = x.reshape(T, D)
    for l in range(depth):
        proj = _norm_matmul(xf, vec(ln1_g), w_in_b, l, tm=min(1024, T), tn=512, out_dtype=F32, name="in_proj")
        sb = _sb_attention(proj, vec(sb_q_g), vec(sb_k_g), vec(sb_out_g), l, n_heads=n_sb_heads, tq=256)
        r, lw, k, v, kk, a, g = _rw_prep(
            proj, mu_r, mu_k, mu_v, mu_lo, vec(rw_w0), w2, vec(rw_a0), a2, g2, vec(rw_k_k), vec(rw_k_a), l,
            rw_width=rw_width, col0=col0, tm=tm)
        rw = _rw_scan(r, lw, k, v, kk, a, g, vec(rw_r_k), vec(rw_ln_g), vec(rw_ln_b), l, npairs=8)
        x1 = _out_proj(xf, sb, rw, w_out_b, l, tm=tm)
        hn, meta, counts = _router(x1, vec(ln2_g), w_r, b_r, l, tm=tm)
        dest0, dest1, row_tok, blk_expert, n_used = _moe_layout(meta, counts, T, tm)
        y_pad = _expert_mlp(hn, row_tok, blk_expert, n_used, w_up, w_down, l)
        xf = _combine_ple(x1, y_pad, dest0, dest1, meta, p3, ple_gate_b, ple_proj_b, vec(ple_norm_g), l)
    return xf.reshape(B, S, D)
```

```python
import functools
import math

import jax
import jax.numpy as jnp
from jax import lax
from jax.experimental import pallas as pl
from jax.experimental.pallas import tpu as pltpu

F32 = jnp.float32
BF16 = jnp.bfloat16

SB_HEAD_DIM = 128
RW_HEAD_DIM = 64
LORA_W = 64
LORA_A = 64
LORA_G = 160
LORA_PAD = 512
N_GROUPS = 4
EXPERTS_PER_GROUP = 8
N_EXPERTS = N_GROUPS * EXPERTS_PER_GROUP
RMS_EPS = 1e-6
GN_EPS = 64e-5
LANES = 128
SUBLANES = 8
VMEM_LIMIT = 56 * 1024 * 1024

EXP_UNDERFLOW = -104.0
RW_CHUNK = 64
MOE_ROWS = 256
TOKEN_TILE = 256
GATHER_UNROLL = 8


def _cparams(sem):
    return pltpu.CompilerParams(dimension_semantics=sem, vmem_limit_bytes=VMEM_LIMIT)


def _rms(x, eps=RMS_EPS):
    return x * lax.rsqrt(jnp.mean(x * x, axis=-1, keepdims=True) + eps)


def _softplus(y):
    return jnp.maximum(y, 0.0) + jnp.log(1.0 + jnp.exp(-jnp.abs(y)))


def _mm(a, b):
    return jnp.dot(a.astype(BF16), b.astype(BF16), preferred_element_type=F32)


def _mm_nt(a, b):
    return lax.dot_general(a.astype(BF16), b.astype(BF16), (((1,), (1,)), ((), ())),
                           preferred_element_type=F32)


def _mm_tn(a, b):
    return lax.dot_general(a.astype(BF16), b.astype(BF16), (((0,), (0,)), ((), ())),
                           preferred_element_type=F32)


def _mm_f32(a, b):
    return jnp.dot(a, b, preferred_element_type=F32, precision=lax.Precision.HIGHEST)


def _mm_split(a, b_exact):
    hi = a.astype(BF16)
    lo = (a - hi.astype(F32)).astype(BF16)
    return (jnp.dot(hi, b_exact, preferred_element_type=F32)
            + jnp.dot(lo, b_exact, preferred_element_type=F32))


def _layer_vec(l, width):
    return pl.BlockSpec((None, 1, width), lambda *_: (l, 0, 0))


def _layer_mat(l, shape):
    return pl.BlockSpec((None,) + tuple(shape), lambda *_: (l,) + (0,) * len(shape))


def _norm_matmul_kernel(x_ref, g_ref, w_ref, o_ref, xn_ref):
    @pl.when(pl.program_id(1) == 0)
    def _():
        xn_ref[...] = (_rms(x_ref[...]) * g_ref[...]).astype(BF16)

    o_ref[...] = jnp.dot(xn_ref[...], w_ref[...], preferred_element_type=F32).astype(o_ref.dtype)


def _norm_matmul(x, g, w, l, *, tm, tn, out_dtype, name):
    T, D = x.shape
    N = w.shape[2]
    return pl.pallas_call(
        _norm_matmul_kernel,
        grid=(T // tm, N // tn),
        in_specs=[
            pl.BlockSpec((tm, D), lambda i, j: (i, 0)),
            _layer_vec(l, D),
            pl.BlockSpec((None, D, tn), lambda i, j: (l, 0, j)),
        ],
        out_specs=pl.BlockSpec((tm, tn), lambda i, j: (i, j)),
        out_shape=jax.ShapeDtypeStruct((T, N), out_dtype),
        scratch_shapes=[pltpu.VMEM((tm, D), BF16)],
        compiler_params=_cparams(("parallel", "arbitrary")),
        name=name,
    )(x, g, w)


def _sb_attn_kernel(q_ref, k_ref, v_ref, qg_ref, kg_ref, og_ref, o_ref, kn_ref, vb_ref, *, tq, seq):
    i = pl.program_id(1)
    hd = SB_HEAD_DIM
    prep_rows = min(512, seq)

    @pl.when(i == 0)
    def _():
        def body(c, carry):
            rows = pl.ds(pl.multiple_of(c * prep_rows, prep_rows), prep_rows)
            kn_ref[rows, :] = (_rms(k_ref[rows, :]) * kg_ref[...]).astype(BF16)
            vb_ref[rows, :] = v_ref[rows, :].astype(BF16)
            return carry

        lax.fori_loop(0, seq // prep_rows, body, 0)

    qn = (_rms(q_ref[...]) * qg_ref[...] * (1.0 / math.sqrt(hd))).astype(BF16)
    row = lax.broadcasted_iota(jnp.int32, (tq, tq), 0)
    col = lax.broadcasted_iota(jnp.int32, (tq, tq), 1)
    tri = (row > col).astype(BF16)
    causal = col < row

    def tiles(js, masked, carry, acc):
        rows = [pl.ds(pl.multiple_of(j * tq, tq), tq) for j in js]
        z = [_mm_nt(qn, kn_ref[r, :]) for r in rows]
        sp = [_softplus(zt) for zt in z]
        log_keep = [jnp.where(causal, -s, 0.0) if m else -s for s, m in zip(sp, masked)]
        log_beta = [zt - s for zt, s in zip(z, sp)]
        later = [_mm_split(lk, tri) for lk in log_keep]
        for t in range(len(js)):
            w = jnp.exp(log_beta[t] + later[t] + carry)
            if masked[t]:
                w = jnp.where(causal, w, 0.0)
            acc = acc + jnp.dot(w.astype(BF16), vb_ref[rows[t], :], preferred_element_type=F32)
            carry = carry + later[t][:, 0:1] + log_keep[t][:, 0:1]
        return carry, acc

    def live(carry):
        return (jnp.max(carry) > EXP_UNDERFLOW).astype(jnp.int32)

    zeros = (jnp.zeros((tq, 1), F32), jnp.zeros((tq, hd), F32))
    carry, acc = lax.cond(i > 0,
                          lambda: tiles([i, i - 1], [True, False], *zeros),
                          lambda: tiles([i], [True], *zeros))

    def cond(st):
        return (st[0] < i) & (st[1] > 0)

    def body(st):
        n, _, carry, acc = st
        carry, acc = tiles([i - 1 - n], [False], carry, acc)
        return n + 1, live(carry), carry, acc

    _, _, carry, acc = lax.while_loop(cond, body, (jnp.int32(1), live(carry), carry, acc))
    o_ref[...] = (_rms(acc) * og_ref[...]).astype(o_ref.dtype)


def _sb_attention(proj, q_g, k_g, out_g, l, *, n_heads, tq):
    T = proj.shape[0]
    hd = SB_HEAD_DIM
    return pl.pallas_call(
        functools.partial(_sb_attn_kernel, tq=tq, seq=T),
        grid=(n_heads, T // tq),
        in_specs=[
            pl.BlockSpec((tq, hd), lambda h, i: (i, h)),
            pl.BlockSpec((T, hd), lambda h, i: (0, n_heads + h)),
            pl.BlockSpec((T, hd), lambda h, i: (0, 2 * n_heads + h)),
            _layer_vec(l, hd), _layer_vec(l, hd), _layer_vec(l, hd),
        ],
        out_specs=pl.BlockSpec((tq, hd), lambda h, i: (i, h)),
        out_shape=jax.ShapeDtypeStruct((T, n_heads * hd), BF16),
        scratch_shapes=[pltpu.VMEM((T, hd), BF16), pltpu.VMEM((T, hd), BF16)],
        compiler_params=_cparams(("parallel", "arbitrary")),
        name="sb_attention",
    )(proj, proj, proj, q_g, k_g, out_g)


def _rw_prep_kernel(r_ref, k_ref, v_ref, lo_ref, rp_ref, kp_ref, vp_ref, lop_ref,
                    mu_r_ref, mu_k_ref, mu_v_ref, mu_lo_ref, w0_ref, w2_ref, a0_ref, a2_ref, g2_ref,
                    kk_ref, ka_ref,
                    r_out, lw_out, k_out, v_out, kk_out, a_out, g_out):
    first = pl.program_id(0) == 0

    def shift(cur_ref, prev_ref, mu_ref):
        cur = cur_ref[...]
        prev_row = jnp.where(first, 0.0, prev_ref[SUBLANES - 1:SUBLANES, :])
        rolled = pltpu.roll(cur, 1, 0)
        rowi = lax.broadcasted_iota(jnp.int32, cur.shape, 0)
        prev = jnp.where(rowi == 0, prev_row, rolled)
        return cur + mu_ref[...] * (prev - cur)

    r = shift(r_ref, rp_ref, mu_r_ref)
    k = shift(k_ref, kp_ref, mu_k_ref)
    v = shift(v_ref, vp_ref, mu_v_ref)
    lo = shift(lo_ref, lop_ref, mu_lo_ref)
    wa_lo = lo[:, :LORA_W + LORA_A]
    g_lo = lo[:, LORA_W + LORA_A:LORA_W + LORA_A + g2_ref.shape[0]]
    w = -_softplus(-(w0_ref[...] + _mm_f32(jnp.tanh(wa_lo), w2_ref[...]))) - 0.5
    a = jax.nn.sigmoid(a0_ref[...] + _mm_f32(wa_lo, a2_ref[...]))
    g = _mm_f32(jax.nn.sigmoid(g_lo), g2_ref[...])
    r_out[...] = r
    lw_out[...] = -jnp.exp(w)
    k_out[...] = k * (1.0 + (a - 1.0) * ka_ref[...])
    v_out[...] = v
    kk_out[...] = k * kk_ref[...]
    a_out[...] = a
    g_out[...] = g


def _rw_prep(proj, mu_r, mu_k, mu_v, mu_lo, w0, w2, a0, a2, g2, k_k, k_a, l, *, rw_width, col0, tm):
    T = proj.shape[0]
    W = rw_width
    cb = col0 // W
    lb = (col0 + 3 * W) // LORA_PAD

    def cur(width, blk):
        return pl.BlockSpec((tm, width), lambda i: (i, blk))

    def prev(width, blk):
        return pl.BlockSpec((SUBLANES, width), lambda i: (jnp.maximum(i * (tm // SUBLANES) - 1, 0), blk))

    out = jax.ShapeDtypeStruct((T, W), F32)
    return pl.pallas_call(
        _rw_prep_kernel,
        grid=(T // tm,),
        in_specs=[cur(W, cb), cur(W, cb + 1), cur(W, cb + 2), cur(LORA_PAD, lb),
                  prev(W, cb), prev(W, cb + 1), prev(W, cb + 2), prev(LORA_PAD, lb),
                  _layer_vec(l, W), _layer_vec(l, W), _layer_vec(l, W), _layer_vec(l, LORA_PAD),
                  _layer_vec(l, W), _layer_mat(l, w2.shape[1:]), _layer_vec(l, W), _layer_mat(l, a2.shape[1:]),
                  _layer_mat(l, g2.shape[1:]), _layer_vec(l, W), _layer_vec(l, W)],
        out_specs=[pl.BlockSpec((tm, W), lambda i: (i, 0))] * 7,
        out_shape=[out] * 7,
        compiler_params=_cparams(("parallel",)),
        name="rw_prep",
    )(proj, proj, proj, proj, proj, proj, proj, proj,
      mu_r, mu_k, mu_v, mu_lo, w0, w2, a0, a2, g2, k_k, k_a)


def _cumsum_rows(x, tril_bf16):
    hi = x.astype(BF16)
    lo = (x - hi.astype(F32)).astype(BF16)
    return (jnp.dot(tril_bf16, hi, preferred_element_type=F32)
            + jnp.dot(tril_bf16, lo, preferred_element_type=F32))


def _rw_scan_kernel(r_ref, lw_ref, k_ref, v_ref, kk_ref, a_ref, g_ref, rk_ref, lng_ref, lnb_ref,
                    o_ref, state_ref, *, npairs):
    C = RW_CHUNK
    N = RW_HEAD_DIM
    C2 = 2 * C
    assert LANES == 2 * N and C2 == LANES

    @pl.when(pl.program_id(1) == 0)
    def _():
        state_ref[...] = jnp.zeros_like(state_ref)

    row = lax.broadcasted_iota(jnp.int32, (C2, C2), 0)
    col = lax.broadcasted_iota(jnp.int32, (C2, C2), 1)
    lower_incl = col <= row
    lower_strict = col < row
    seg_ones = ((row < N) == (col < N)).astype(BF16)
    tril = (lax.broadcasted_iota(jnp.int32, (C, C), 1) <= lax.broadcasted_iota(jnp.int32, (C, C), 0)).astype(BF16)
    head0 = lax.broadcasted_iota(jnp.int32, (C, LANES), 1) < N

    def stack(x):
        return jnp.concatenate([jnp.where(head0, x, 0.0), jnp.where(head0, 0.0, x)], axis=0)

    def seg_sum(x):
        return _mm(x, seg_ones)

    P = range(npairs)
    sl = [slice(p * LANES, (p + 1) * LANES) for p in P]
    r = [r_ref[:, sl[p]] for p in P]
    lw = [lw_ref[:, sl[p]] for p in P]
    k = [k_ref[:, sl[p]] for p in P]
    v = [v_ref[:, sl[p]] for p in P]
    kk = [kk_ref[:, sl[p]] for p in P]
    a = [a_ref[:, sl[p]] for p in P]
    kkn = [kk[p] * lax.rsqrt(seg_sum(kk[p] * kk[p]) + 1e-12) for p in P]
    bonus = [seg_sum(r[p] * k[p] * rk_ref[:, sl[p]]) * v[p] for p in P]
    cum = [_cumsum_rows(lw[p], tril) for p in P]
    inv = [jnp.exp(-cum[p]) for p in P]
    a2 = [stack(-kkn[p] * jnp.exp(cum[p] - lw[p])).astype(BF16) for p in P]
    r2 = [stack(r[p] * jnp.exp(cum[p])).astype(BF16) for p in P]
    b2 = [stack(kkn[p] * a[p] * inv[p]).astype(BF16) for p in P]
    k2 = [stack(k[p] * inv[p]).astype(BF16) for p in P]
    v2 = [stack(v[p]).astype(BF16) for p in P]
    ar = [jnp.concatenate([a2[p], r2[p]], axis=0) for p in P]
    bk = [jnp.concatenate([b2[p], k2[p]], axis=0) for p in P]
    gram = [_mm_nt(ar[p], bk[p]) for p in P]
    l_ab = [jnp.where(lower_strict, gram[p][:C2, :C2], 0.0).astype(BF16) for p in P]
    l_ak = [jnp.where(lower_strict, gram[p][:C2, C2:], 0.0).astype(BF16) for p in P]
    m_r = [jnp.concatenate([jnp.where(lower_incl, gram[p][C2:, :C2], 0.0),
                            jnp.where(lower_incl, gram[p][C2:, C2:], 0.0)], axis=1).astype(BF16) for p in P]
    s0 = [state_ref[p] for p in P]
    ars = [_mm_nt(ar[p], s0[p]) for p in P]
    x = [ars[p][:C2] + _mm(l_ak[p], v2[p]) for p in P]
    pw = l_ab
    levels = int(math.log2(C))
    for lvl in range(levels):
        if lvl + 1 < levels:
            prod = [_mm(pw[p], jnp.concatenate([x[p].astype(BF16), pw[p]], axis=1)) for p in P]
            x = [x[p] + prod[p][:, :C2] for p in P]
            pw = [prod[p][:, C2:].astype(BF16) for p in P]
        else:
            x = [x[p] + _mm(pw[p], x[p]) for p in P]
    u2 = [x[p].astype(BF16) for p in P]
    y2 = [ars[p][C2:] + _mm(m_r[p], jnp.concatenate([u2[p], v2[p]], axis=0)) for p in P]
    upd = [_mm_tn(jnp.concatenate([u2[p], v2[p]], axis=0), bk[p]) for p in P]
    for p in P:
        state_ref[p] = (s0[p] + upd[p]) * jnp.exp(cum[p][C - 1:C, :])
    y = [y2[p][:C] + y2[p][C:] for p in P]
    yc = [y[p] - seg_sum(y[p]) * (1.0 / N) for p in P]
    var = [seg_sum(yc[p] * yc[p]) * (1.0 / N) for p in P]
    for p in P:
        yn = yc[p] * lax.rsqrt(var[p] + GN_EPS) * lng_ref[:, sl[p]] + lnb_ref[:, sl[p]]
        o_ref[:, sl[p]] = ((yn + bonus[p]) * g_ref[:, sl[p]]).astype(o_ref.dtype)


def _rw_scan(r, lw, k, v, kk, a, g, r_k, ln_g, ln_b, l, *, npairs):
    T, W = r.shape
    C = RW_CHUNK
    wb = npairs * LANES
    blk = pl.BlockSpec((C, wb), lambda p, c: (c, p))
    vec = pl.BlockSpec((None, 1, wb), lambda p, c: (l, 0, p))
    return pl.pallas_call(
        functools.partial(_rw_scan_kernel, npairs=npairs),
        grid=(W // wb, T // C),
        in_specs=[blk] * 7 + [vec] * 3,
        out_specs=blk,
        out_shape=jax.ShapeDtypeStruct((T, W), BF16),
        scratch_shapes=[pltpu.VMEM((npairs, LANES, LANES), F32)],
        compiler_params=_cparams(("parallel", "arbitrary")),
        name="rw_scan",
    )(r, lw, k, v, kk, a, g, r_k, ln_g, ln_b)


def _out_proj_kernel(x_ref, sb_ref, rw_ref, wa_ref, wb_ref, o_ref):
    o_ref[...] = (x_ref[...]
                  + jnp.dot(sb_ref[...], wa_ref[...], preferred_element_type=F32)
                  + jnp.dot(rw_ref[...], wb_ref[...], preferred_element_type=F32))


def _out_proj(x, sb, rw, w_out, l, *, tm):
    T, D = x.shape
    Wa = sb.shape[1]
    Wb = rw.shape[1]
    assert Wa == Wb
    return pl.pallas_call(
        _out_proj_kernel,
        grid=(T // tm,),
        in_specs=[
            pl.BlockSpec((tm, D), lambda i: (i, 0)),
            pl.BlockSpec((tm, Wa), lambda i: (i, 0)),
            pl.BlockSpec((tm, Wb), lambda i: (i, 0)),
            pl.BlockSpec((None, Wa, D), lambda i: (l, 0, 0)),
            pl.BlockSpec((None, Wb, D), lambda i: (l, 1, 0)),
        ],
        out_specs=pl.BlockSpec((tm, D), lambda i: (i, 0)),
        out_shape=jax.ShapeDtypeStruct((T, D), F32),
        compiler_params=_cparams(("parallel",)),
        name="out_proj",
    )(x, sb, rw, w_out, w_out)


def _router_kernel(x_ref, g_ref, w_ref, b_ref, meta_ref, cnt_ref, carry_ref, *, tm):
    i = pl.program_id(0)

    @pl.when(i == 0)
    def _():
        carry_ref[...] = jnp.zeros_like(carry_ref)

    hn = _rms(x_ref[...]) * g_ref[...]
    logits = _mm_f32(hn, w_ref[...]) + b_ref[...]
    lane = lax.broadcasted_iota(jnp.int32, logits.shape, 1).astype(F32)
    big = float(LANES)
    neg = -jnp.inf
    is_grp = (lane >= N_EXPERTS) & (lane < N_EXPERTS + N_GROUPS)
    gl = jnp.where(is_grp, logits, neg)
    gmax = jnp.max(gl, axis=-1, keepdims=True)
    g_w = 1.0 / jnp.sum(jnp.where(is_grp, jnp.exp(gl - gmax), 0.0), axis=-1, keepdims=True)
    grp = jnp.min(jnp.where(gl == gmax, lane, big), axis=-1, keepdims=True) - N_EXPERTS
    lo = grp * EXPERTS_PER_GROUP
    in_grp = (lane >= lo) & (lane < lo + EXPERTS_PER_GROUP)
    es = jnp.where(in_grp, logits, neg)
    m1 = jnp.max(es, axis=-1, keepdims=True)
    i1 = jnp.min(jnp.where(es == m1, lane, big), axis=-1, keepdims=True)
    es2 = jnp.where(lane == i1, neg, es)
    m2 = jnp.max(es2, axis=-1, keepdims=True)
    i2 = jnp.min(jnp.where(es2 == m2, lane, big), axis=-1, keepdims=True)
    z = jnp.sum(jnp.where(in_grp, jnp.exp(es - m1), 0.0), axis=-1, keepdims=True)
    p1 = 1.0 / z
    p2 = jnp.exp(m2 - m1) / z
    w1 = g_w * (p1 / (p1 + p2))
    w2 = g_w * (p2 / (p1 + p2))
    oh1 = lane == i1
    oh2 = lane == i2
    oh = (oh1 | oh2).astype(BF16)
    row = lax.broadcasted_iota(jnp.int32, (tm, tm), 0)
    col = lax.broadcasted_iota(jnp.int32, (tm, tm), 1)
    before = jnp.dot((col < row).astype(BF16), oh, preferred_element_type=F32) + carry_ref[0:1, :]
    rank1 = jnp.sum(jnp.where(oh1, before, 0.0), axis=-1, keepdims=True)
    rank2 = jnp.sum(jnp.where(oh2, before, 0.0), axis=-1, keepdims=True)
    carry_ref[...] = carry_ref[...] + jnp.sum(oh.astype(F32), axis=0, keepdims=True)
    meta = jnp.where(lane == 0, i1, 0.0)
    meta = jnp.where(lane == 1, i2, meta)
    meta = jnp.where(lane == 2, w1, meta)
    meta = jnp.where(lane == 3, w2, meta)
    meta = jnp.where(lane == 4, rank1, meta)
    meta = jnp.where(lane == 5, rank2, meta)
    meta_ref[...] = meta
    cnt_ref[...] = carry_ref[...]


def _router(x, g, w, b, l, *, tm):
    T, D = x.shape
    return pl.pallas_call(
        functools.partial(_router_kernel, tm=tm),
        grid=(T // tm,),
        in_specs=[
            pl.BlockSpec((tm, D), lambda i: (i, 0)),
            _layer_vec(l, D),
            _layer_mat(l, (D, LANES)),
            _layer_vec(l, LANES),
        ],
        out_specs=[
            pl.BlockSpec((tm, LANES), lambda i: (i, 0)),
            pl.BlockSpec((SUBLANES, LANES), lambda i: (0, 0)),
        ],
        out_shape=[
            jax.ShapeDtypeStruct((T, LANES), F32),
            jax.ShapeDtypeStruct((SUBLANES, LANES), F32),
        ],
        scratch_shapes=[pltpu.VMEM((SUBLANES, LANES), F32)],
        compiler_params=_cparams(("arbitrary",)),
        name="router",
    )(x, g, w, b)


def _gather_rows_start(idx_ref, src_hbm, dst, sem):
    n = dst.shape[0]

    def body(r, carry):
        t = idx_ref[0, 0, r]
        pltpu.make_async_copy(src_hbm.at[pl.ds(t, 1), :], dst.at[pl.ds(r, 1), :], sem).start()
        return carry

    lax.fori_loop(0, n, body, 0, unroll=GATHER_UNROLL)


def _gather_rows_wait(src_hbm, dst, sem):
    pltpu.make_async_copy(src_hbm.at[pl.ds(0, dst.shape[0]), :], dst, sem).wait()


def _scatter_rows_start(idx_ref, src, dst_hbm, sem):
    n = src.shape[0]

    def body(r, carry):
        t = idx_ref[0, 0, r]
        pltpu.make_async_copy(src.at[pl.ds(r, 1), :], dst_hbm.at[pl.ds(t, 1), :], sem).start()
        return carry

    lax.fori_loop(0, n, body, 0, unroll=GATHER_UNROLL)


def _scatter_rows_wait(src, dst_hbm, sem):
    pltpu.make_async_copy(src, dst_hbm.at[pl.ds(0, src.shape[0]), :], sem).wait()


def _dispatch_kernel(d0_ref, d1_ref, x_ref, g_ref, zero_hbm, o_hbm, sbuf, sem):
    del zero_hbm
    i = pl.program_id(0)
    n = pl.num_programs(0)
    slot = i % 2
    sbuf[slot] = _rms(x_ref[...]) * g_ref[...]
    _scatter_rows_start(d0_ref, sbuf.at[slot], o_hbm, sem.at[slot])
    _scatter_rows_start(d1_ref, sbuf.at[slot], o_hbm, sem.at[slot])

    def drain(s):
        _scatter_rows_wait(sbuf.at[s], o_hbm, sem.at[s])
        _scatter_rows_wait(sbuf.at[s], o_hbm, sem.at[s])

    @pl.when(i > 0)
    def _():
        drain(1 - slot)

    @pl.when(i == n - 1)
    def _():
        drain(slot)


def _dispatch(x, g, dest0, dest1, n_rows, l):
    T, D = x.shape
    nt, _, tm = dest0.shape
    cur = pl.BlockSpec((1, 1, tm), lambda i: (i, 0, 0), memory_space=pltpu.SMEM)
    zeros = jnp.zeros((n_rows, D), F32)
    return pl.pallas_call(
        _dispatch_kernel,
        grid=(nt,),
        in_specs=[cur, cur, pl.BlockSpec((tm, D), lambda i: (i, 0)), _layer_vec(l, D),
                  pl.BlockSpec(memory_space=pl.ANY)],
        out_specs=pl.BlockSpec(memory_space=pl.ANY),
        out_shape=jax.ShapeDtypeStruct((n_rows, D), F32),
        scratch_shapes=[pltpu.VMEM((2, tm, D), F32), pltpu.SemaphoreType.DMA((2,))],
        input_output_aliases={4: 0},
        compiler_params=_cparams(("arbitrary",)),
        name="dispatch",
    )(dest0, dest1, x, g, zeros)


def _expert_kernel(be_ref, nused_ref, x_ref, wu_ref, wd_ref, o_ref, wu_bf, wd_bf):
    b = pl.program_id(0)
    n_used = nused_ref[0]
    cast_rows = 256

    @pl.when((b < n_used) & ((b == 0) | (be_ref[b] != be_ref[jnp.maximum(b - 1, 0)])))
    def _():
        def cast(ref, dst, c):
            rows = pl.ds(pl.multiple_of(c * cast_rows, cast_rows), cast_rows)
            dst[rows, :] = ref[rows, :].astype(BF16)

        def up(c, carry):
            cast(wu_ref, wu_bf, c)
            return carry

        def down(c, carry):
            cast(wd_ref, wd_bf, c)
            return carry

        lax.fori_loop(0, wu_bf.shape[0] // cast_rows, up, 0)
        lax.fori_loop(0, wd_bf.shape[0] // cast_rows, down, 0)

    @pl.when(b < n_used)
    def _():
        h = jnp.dot(x_ref[...].astype(BF16), wu_bf[...], preferred_element_type=F32)
        ff = h.shape[1] // 2
        gate = h[:, :ff]
        up = h[:, ff:]
        act = (gate * jax.nn.sigmoid(gate) * up).astype(BF16)
        o_ref[...] = jnp.dot(act, wd_bf[...], preferred_element_type=F32).astype(o_ref.dtype)

    @pl.when(b >= n_used)
    def _():
        o_ref[...] = jnp.zeros_like(o_ref)


def _expert_mlp(x_pad, blk_expert, n_used, w_up, w_down, l):
    R, D = x_pad.shape
    bm = MOE_ROWS
    F2 = w_up.shape[3]
    grid_spec = pltpu.PrefetchScalarGridSpec(
        num_scalar_prefetch=2,
        grid=(R // bm,),
        in_specs=[
            pl.BlockSpec((bm, D), lambda b, be, nu: (b, 0)),
            pl.BlockSpec((None, None, D, F2), lambda b, be, nu: (l, be[b], 0, 0)),
            pl.BlockSpec((None, None, F2 // 2, D), lambda b, be, nu: (l, be[b], 0, 0)),
        ],
        out_specs=pl.BlockSpec((bm, D), lambda b, be, nu: (b, 0)),
        scratch_shapes=[pltpu.VMEM((D, F2), BF16), pltpu.VMEM((F2 // 2, D), BF16)],
    )
    return pl.pallas_call(
        _expert_kernel,
        grid_spec=grid_spec,
        out_shape=jax.ShapeDtypeStruct((R, D), F32),
        compiler_params=_cparams(("arbitrary",)),
        name="expert_mlp",
    )(blk_expert, n_used, x_pad, w_up, w_down)


def _combine_ple_kernel(d0_ref, d1_ref, d0n_ref, d1n_ref, y_hbm, x_ref, meta_ref, p_ref, wg_ref, wp_ref, g_ref,
                        o_ref, ybuf, sem):
    i = pl.program_id(0)
    n = pl.num_programs(0)
    slot = i % 2

    def start(s, a_ref, b_ref):
        _gather_rows_start(a_ref, y_hbm, ybuf.at[s, 0], sem.at[s])
        _gather_rows_start(b_ref, y_hbm, ybuf.at[s, 1], sem.at[s])

    @pl.when(i == 0)
    def _():
        start(0, d0_ref, d1_ref)

    @pl.when(i + 1 < n)
    def _():
        start(1 - slot, d0n_ref, d1n_ref)

    _gather_rows_wait(y_hbm, ybuf.at[slot, 0], sem.at[slot])
    _gather_rows_wait(y_hbm, ybuf.at[slot, 1], sem.at[slot])
    meta = meta_ref[...]
    x = x_ref[...] + meta[:, 2:3] * ybuf[slot, 0] + meta[:, 3:4] * ybuf[slot, 1]
    gate = jax.nn.sigmoid(jnp.dot(_rms(x).astype(BF16), wg_ref[...], preferred_element_type=F32))
    pe = jnp.dot(p_ref[...].astype(BF16), wp_ref[...], preferred_element_type=F32)
    o_ref[...] = x + _rms(pe * gate) * g_ref[...]


def _combine_ple(x, y_pad, dest0, dest1, meta, p, w_gate, w_proj, g, l):
    T, D = x.shape
    P = p.shape[2]
    nt, _, tm = dest0.shape
    row = lambda w: pl.BlockSpec((tm, w), lambda i: (i, 0))
    cur = pl.BlockSpec((1, 1, tm), lambda i: (i, 0, 0), memory_space=pltpu.SMEM)
    nxt = pl.BlockSpec((1, 1, tm), lambda i: (jnp.minimum(i + 1, nt - 1), 0, 0), memory_space=pltpu.SMEM)
    return pl.pallas_call(
        _combine_ple_kernel,
        grid=(nt,),
        in_specs=[cur, cur, nxt, nxt, pl.BlockSpec(memory_space=pl.ANY),
                  row(D), row(LANES),
                  pl.BlockSpec((None, tm, P), lambda i: (l, i, 0)),
                  _layer_mat(l, (D, D)), _layer_mat(l, (P, D)), _layer_vec(l, D)],
        out_specs=row(D),
        out_shape=jax.ShapeDtypeStruct((T, D), F32),
        scratch_shapes=[pltpu.VMEM((2, 2, tm, D), F32), pltpu.SemaphoreType.DMA((2,))],
        compiler_params=_cparams(("arbitrary",)),
        name="combine_ple",
    )(dest0, dest1, dest0, dest1, y_pad, x, meta, p, w_gate, w_proj, g)


def _moe_layout(meta, counts, T, tm):
    bm = MOE_ROWS
    n_rows = 2 * T + N_EXPERTS * bm
    nb = n_rows // bm
    e = meta[:, 0:2].astype(jnp.int32)
    rank = meta[:, 4:6].astype(jnp.int32)
    cnt = counts[0, :N_EXPERTS].astype(jnp.int32)
    padded = ((cnt + bm - 1) // bm) * bm
    pad_ends = jnp.cumsum(padded)
    pad_starts = pad_ends - padded
    dest = pad_starts[e] + rank
    blk_start = jnp.arange(nb, dtype=jnp.int32) * bm
    blk_expert = jnp.minimum(jnp.sum(pad_ends[None, :] <= blk_start[:, None], axis=1), N_EXPERTS - 1)
    n_used = (pad_ends[-1:] // bm).astype(jnp.int32)
    dest0 = dest[:, 0].reshape(T // tm, 1, tm)
    dest1 = dest[:, 1].reshape(T // tm, 1, tm)
    return dest0, dest1, n_rows, blk_expert.astype(jnp.int32), n_used


def kernel(x, p, ln1_g, w_in, sb_q_g, sb_k_g, sb_out_g, rw_mu, rw_w0, rw_w2, rw_a0, rw_a2, rw_g2, rw_k_k, rw_k_a, rw_r_k, rw_ln_g, rw_ln_b, w_out, ln2_g, router_g, router_g_b, router_e, router_e_b, w_up, w_down, ple_proj, ple_gate, ple_norm_g):
    B, S, D = x.shape
    depth = w_in.shape[0]
    T = B * S
    assert B == 1, "token shift and attention assume one sequence"
    sb_width = D // 2
    rw_width = D - sb_width
    n_sb_heads = sb_width // SB_HEAD_DIM
    lora = LORA_W + LORA_A + LORA_G
    col0 = 3 * sb_width
    W = rw_width
    tm = min(TOKEN_TILE, T)
    vec = lambda a: a.reshape(depth, 1, -1)

    w_in_b = jnp.pad(w_in, ((0, 0), (0, 0), (0, LORA_PAD - lora))).astype(BF16)
    w_out_b = w_out.astype(BF16)
    ple_gate_b = ple_gate.astype(BF16)
    ple_proj_b = ple_proj.astype(BF16)
    mu = vec(rw_mu)
    mu_r, mu_k, mu_v = mu[:, :, :W], mu[:, :, W:2 * W], mu[:, :, 2 * W:3 * W]
    mu_lo = jnp.pad(mu[:, :, 3 * W:], ((0, 0), (0, 0), (0, LORA_PAD - lora)))
    w2 = jnp.pad(rw_w2, ((0, 0), (0, LORA_A), (0, 0)))
    a2 = jnp.pad(rw_a2, ((0, 0), (LORA_W, 0), (0, 0)))
    g2 = jnp.pad(rw_g2, ((0, 0), (0, 2 * LANES - LORA_G), (0, 0)))
    w_r = jnp.pad(jnp.concatenate([router_e, router_g], axis=2),
                  ((0, 0), (0, 0), (0, LANES - N_EXPERTS - N_GROUPS)))
    b_r = vec(jnp.pad(jnp.concatenate([router_e_b, router_g_b], axis=1),
                      ((0, 0), (0, LANES - N_EXPERTS - N_GROUPS))))
    p3 = p.reshape(depth, T, -1)

    xf = x.reshape(T, D)
    for l in range(depth):
        proj = _norm_matmul(xf, vec(ln1_g), w_in_b, l, tm=min(1024, T), tn=512, out_dtype=F32, name="in_proj")
        sb = _sb_attention(proj, vec(sb_q_g), vec(sb_k_g), vec(sb_out_g), l, n_heads=n_sb_heads, tq=256)
        r, lw, k, v, kk, a, g = _rw_prep(
            proj, mu_r, mu_k, mu_v, mu_lo, vec(rw_w0), w2, vec(rw_a0), a2, g2, vec(rw_k_k), vec(rw_k_a), l,
            rw_width=rw_width, col0=col0, tm=tm)
        rw = _rw_scan(r, lw, k, v, kk, a, g, vec(rw_r_k), vec(rw_ln_g), vec(rw_ln_b), l, npairs=8)
        x1 = _out_proj(xf, sb, rw, w_out_b, l, tm=tm)
        meta, counts = _router(x1, vec(ln2_g), w_r, b_r, l, tm=tm)
        dest0, dest1, n_rows, blk_expert, n_used = _moe_layout(meta, counts, T, tm)
        x_pad = _dispatch(x1, vec(ln2_g), dest0, dest1, n_rows, l)
        y_pad = _expert_mlp(x_pad, blk_expert, n_used, w_up, w_down, l)
        xf = _combine_ple(x1, y_pad, dest0, dest1, meta, p3, ple_gate_b, ple_proj_b, vec(ple_norm_g), l)
    return xf.reshape(B, S, D)
```

```python
import functools
import math

import jax
import jax.numpy as jnp
from jax import lax
from jax.experimental import pallas as pl
from jax.experimental.pallas import tpu as pltpu

F32 = jnp.float32
BF16 = jnp.bfloat16

SB_HEAD_DIM = 128
RW_HEAD_DIM = 64
LORA_W = 64
LORA_A = 64
LORA_G = 160
LORA_PAD = 512
N_GROUPS = 4
EXPERTS_PER_GROUP = 8
N_EXPERTS = N_GROUPS * EXPERTS_PER_GROUP
RMS_EPS = 1e-6
GN_EPS = 64e-5
LANES = 128
SUBLANES = 8
VMEM_LIMIT = 56 * 1024 * 1024

EXP_UNDERFLOW = -104.0
RW_CHUNK = 64
MOE_ROWS = 256
TOKEN_TILE = 256
GATHER_UNROLL = 8


def _cparams(sem):
    return pltpu.CompilerParams(dimension_semantics=sem, vmem_limit_bytes=VMEM_LIMIT)


def _rms(x, eps=RMS_EPS):
    return x * lax.rsqrt(jnp.mean(x * x, axis=-1, keepdims=True) + eps)


def _softplus(y):
    return jnp.maximum(y, 0.0) + jnp.log(1.0 + jnp.exp(-jnp.abs(y)))


def _mm(a, b):
    return jnp.dot(a.astype(BF16), b.astype(BF16), preferred_element_type=F32)


def _mm_nt(a, b):
    return lax.dot_general(a.astype(BF16), b.astype(BF16), (((1,), (1,)), ((), ())),
                           preferred_element_type=F32)


def _mm_tn(a, b):
    return lax.dot_general(a.astype(BF16), b.astype(BF16), (((0,), (0,)), ((), ())),
                           preferred_element_type=F32)


def _mm_f32(a, b):
    return jnp.dot(a, b, preferred_element_type=F32, precision=lax.Precision.HIGHEST)


def _mm_split(a, b_exact):
    hi = a.astype(BF16)
    lo = (a - hi.astype(F32)).astype(BF16)
    return (jnp.dot(hi, b_exact, preferred_element_type=F32)
            + jnp.dot(lo, b_exact, preferred_element_type=F32))


def _layer_vec(l, width):
    return pl.BlockSpec((None, 1, width), lambda *_: (l, 0, 0))


def _layer_mat(l, shape):
    return pl.BlockSpec((None,) + tuple(shape), lambda *_: (l,) + (0,) * len(shape))


def _norm_matmul_kernel(x_ref, g_ref, w_ref, o_ref, xn_ref):
    @pl.when(pl.program_id(1) == 0)
    def _():
        xn_ref[...] = (_rms(x_ref[...]) * g_ref[...]).astype(BF16)

    o_ref[...] = jnp.dot(xn_ref[...], w_ref[...], preferred_element_type=F32).astype(o_ref.dtype)


def _norm_matmul(x, g, w, l, *, tm, tn, out_dtype, name):
    T, D = x.shape
    N = w.shape[2]
    return pl.pallas_call(
        _norm_matmul_kernel,
        grid=(T // tm, N // tn),
        in_specs=[
            pl.BlockSpec((tm, D), lambda i, j: (i, 0)),
            _layer_vec(l, D),
            pl.BlockSpec((None, D, tn), lambda i, j: (l, 0, j)),
        ],
        out_specs=pl.BlockSpec((tm, tn), lambda i, j: (i, j)),
        out_shape=jax.ShapeDtypeStruct((T, N), out_dtype),
        scratch_shapes=[pltpu.VMEM((tm, D), BF16)],
        compiler_params=_cparams(("parallel", "arbitrary")),
        name=name,
    )(x, g, w)


def _sb_attn_kernel(q_ref, k_ref, v_ref, qg_ref, kg_ref, og_ref, o_ref, kn_ref, vb_ref, *, tq, seq):
    i = pl.program_id(1)
    hd = SB_HEAD_DIM
    prep_rows = min(512, seq)

    @pl.when(i == 0)
    def _():
        def body(c, carry):
            rows = pl.ds(pl.multiple_of(c * prep_rows, prep_rows), prep_rows)
            kn_ref[rows, :] = (_rms(k_ref[rows, :]) * kg_ref[...]).astype(BF16)
            vb_ref[rows, :] = v_ref[rows, :].astype(BF16)
            return carry

        lax.fori_loop(0, seq // prep_rows, body, 0)

    nsub = q_ref.shape[0] // tq
    qn_all = (_rms(q_ref[...]) * qg_ref[...] * (1.0 / math.sqrt(hd))).astype(BF16)
    qn = [qn_all[s * tq:(s + 1) * tq] for s in range(nsub)]
    qi = [nsub * i + s for s in range(nsub)]
    row = lax.broadcasted_iota(jnp.int32, (tq, tq), 0)
    col = lax.broadcasted_iota(jnp.int32, (tq, tq), 1)
    tri = (row > col).astype(BF16)
    causal = col < row

    def scores(s, j, masked, valid=None):
        rows = pl.ds(pl.multiple_of(j * tq, tq), tq)
        z = _mm_nt(qn[s], kn_ref[rows, :])
        sp = _softplus(z)
        log_keep = -sp
        if masked:
            log_keep = jnp.where(causal, log_keep, 0.0)
        if valid is not None:
            log_keep = jnp.where(valid, log_keep, 0.0)
        later = _mm(log_keep, tri)
        return rows, z - sp, log_keep, later, masked, valid

    def accumulate(tile, carry, acc):
        rows, log_beta, log_keep, later, masked, valid = tile
        w = jnp.exp(log_beta + later + carry)
        if masked:
            w = jnp.where(causal, w, 0.0)
        if valid is not None:
            w = jnp.where(valid, w, 0.0)
        acc = acc + jnp.dot(w.astype(BF16), vb_ref[rows, :], preferred_element_type=F32)
        return carry + later[:, 0:1] + log_keep[:, 0:1], acc

    def live(carry):
        return (jnp.max(carry) > EXP_UNDERFLOW).astype(jnp.int32)

    diag = [scores(s, qi[s], True) for s in range(nsub)]
    prev = [scores(s, jnp.maximum(qi[s] - 1, 0), False, None if s > 0 else i > 0) for s in range(nsub)]
    state = []
    for s in range(nsub):
        carry, acc = accumulate(diag[s], jnp.zeros((tq, 1), F32), jnp.zeros((tq, hd), F32))
        state.append(accumulate(prev[s], carry, acc))
    for s in range(nsub):
        carry, acc = state[s]

        def cond(st, s=s):
            return (st[0] < qi[s]) & (st[1] > 0)

        def body(st, s=s):
            n, _, carry, acc = st
            carry, acc = accumulate(scores(s, qi[s] - 1 - n, False), carry, acc)
            return n + 1, live(carry), carry, acc

        _, _, carry, acc = lax.while_loop(cond, body, (jnp.int32(1), live(carry), carry, acc))
        o_ref[s * tq:(s + 1) * tq, :] = (_rms(acc) * og_ref[...]).astype(o_ref.dtype)


def _sb_attention(proj, q_g, k_g, out_g, l, *, n_heads, tq, nsub):
    T = proj.shape[0]
    hd = SB_HEAD_DIM
    tb = nsub * tq
    return pl.pallas_call(
        functools.partial(_sb_attn_kernel, tq=tq, seq=T),
        grid=(n_heads, T // tb),
        in_specs=[
            pl.BlockSpec((tb, hd), lambda h, i: (i, h)),
            pl.BlockSpec((T, hd), lambda h, i: (0, n_heads + h)),
            pl.BlockSpec((T, hd), lambda h, i: (0, 2 * n_heads + h)),
            _layer_vec(l, hd), _layer_vec(l, hd), _layer_vec(l, hd),
        ],
        out_specs=pl.BlockSpec((tb, hd), lambda h, i: (i, h)),
        out_shape=jax.ShapeDtypeStruct((T, n_heads * hd), BF16),
        scratch_shapes=[pltpu.VMEM((T, hd), BF16), pltpu.VMEM((T, hd), BF16)],
        compiler_params=_cparams(("parallel", "arbitrary")),
        name="sb_attention",
    )(proj, proj, proj, q_g, k_g, out_g)


def _rw_prep_kernel(r_ref, k_ref, v_ref, lo_ref, rp_ref, kp_ref, vp_ref, lop_ref,
                    mu_r_ref, mu_k_ref, mu_v_ref, mu_lo_ref, w0_ref, w2_ref, a0_ref, a2_ref, g2_ref,
                    kk_ref, ka_ref,
                    r_out, lw_out, k_out, v_out, kk_out, a_out, g_out):
    first = pl.program_id(0) == 0

    def shift(cur_ref, prev_ref, mu_ref):
        cur = cur_ref[...]
        prev_row = jnp.where(first, 0.0, prev_ref[SUBLANES - 1:SUBLANES, :])
        rolled = pltpu.roll(cur, 1, 0)
        rowi = lax.broadcasted_iota(jnp.int32, cur.shape, 0)
        prev = jnp.where(rowi == 0, prev_row, rolled)
        return cur + mu_ref[...] * (prev - cur)

    r = shift(r_ref, rp_ref, mu_r_ref)
    k = shift(k_ref, kp_ref, mu_k_ref)
    v = shift(v_ref, vp_ref, mu_v_ref)
    lo = shift(lo_ref, lop_ref, mu_lo_ref)
    wa_lo = lo[:, :LORA_W + LORA_A]
    g_lo = lo[:, LORA_W + LORA_A:LORA_W + LORA_A + g2_ref.shape[0]]
    w = -_softplus(-(w0_ref[...] + _mm_f32(jnp.tanh(wa_lo), w2_ref[...]))) - 0.5
    a = jax.nn.sigmoid(a0_ref[...] + _mm_f32(wa_lo, a2_ref[...]))
    g = _mm_f32(jax.nn.sigmoid(g_lo), g2_ref[...])
    r_out[...] = r
    lw_out[...] = -jnp.exp(w)
    k_out[...] = k * (1.0 + (a - 1.0) * ka_ref[...])
    v_out[...] = v
    kk_out[...] = k * kk_ref[...]
    a_out[...] = a
    g_out[...] = g


def _rw_prep(proj, mu_r, mu_k, mu_v, mu_lo, w0, w2, a0, a2, g2, k_k, k_a, l, *, rw_width, col0, tm):
    T = proj.shape[0]
    W = rw_width
    cb = col0 // W
    lb = (col0 + 3 * W) // LORA_PAD

    def cur(width, blk):
        return pl.BlockSpec((tm, width), lambda i: (i, blk))

    def prev(width, blk):
        return pl.BlockSpec((SUBLANES, width), lambda i: (jnp.maximum(i * (tm // SUBLANES) - 1, 0), blk))

    out = jax.ShapeDtypeStruct((T, W), F32)
    return pl.pallas_call(
        _rw_prep_kernel,
        grid=(T // tm,),
        in_specs=[cur(W, cb), cur(W, cb + 1), cur(W, cb + 2), cur(LORA_PAD, lb),
                  prev(W, cb), prev(W, cb + 1), prev(W, cb + 2), prev(LORA_PAD, lb),
                  _layer_vec(l, W), _layer_vec(l, W), _layer_vec(l, W), _layer_vec(l, LORA_PAD),
                  _layer_vec(l, W), _layer_mat(l, w2.shape[1:]), _layer_vec(l, W), _layer_mat(l, a2.shape[1:]),
                  _layer_mat(l, g2.shape[1:]), _layer_vec(l, W), _layer_vec(l, W)],
        out_specs=[pl.BlockSpec((tm, W), lambda i: (i, 0))] * 7,
        out_shape=[out] * 7,
        compiler_params=_cparams(("parallel",)),
        name="rw_prep",
    )(proj, proj, proj, proj, proj, proj, proj, proj,
      mu_r, mu_k, mu_v, mu_lo, w0, w2, a0, a2, g2, k_k, k_a)


def _cumsum_rows(x, tril_bf16):
    hi = x.astype(BF16)
    lo = (x - hi.astype(F32)).astype(BF16)
    return (jnp.dot(tril_bf16, hi, preferred_element_type=F32)
            + jnp.dot(tril_bf16, lo, preferred_element_type=F32))


def _rw_scan_kernel(r_ref, lw_ref, k_ref, v_ref, kk_ref, a_ref, g_ref, rk_ref, lng_ref, lnb_ref,
                    o_ref, state_ref, *, npairs):
    C = RW_CHUNK
    N = RW_HEAD_DIM
    C2 = 2 * C
    assert LANES == 2 * N and C2 == LANES

    @pl.when(pl.program_id(1) == 0)
    def _():
        state_ref[...] = jnp.zeros_like(state_ref)

    row = lax.broadcasted_iota(jnp.int32, (C2, C2), 0)
    col = lax.broadcasted_iota(jnp.int32, (C2, C2), 1)
    lower_incl = col <= row
    lower_strict = col < row
    seg_ones = ((row < N) == (col < N)).astype(BF16)
    tril = (lax.broadcasted_iota(jnp.int32, (C, C), 1) <= lax.broadcasted_iota(jnp.int32, (C, C), 0)).astype(BF16)
    head0 = lax.broadcasted_iota(jnp.int32, (C, LANES), 1) < N

    def stack(x):
        return jnp.concatenate([jnp.where(head0, x, 0.0), jnp.where(head0, 0.0, x)], axis=0)

    def seg_sum(x):
        return _mm(x, seg_ones)

    P = range(npairs)
    sl = [slice(p * LANES, (p + 1) * LANES) for p in P]
    r = [r_ref[:, sl[p]] for p in P]
    lw = [lw_ref[:, sl[p]] for p in P]
    k = [k_ref[:, sl[p]] for p in P]
    v = [v_ref[:, sl[p]] for p in P]
    kk = [kk_ref[:, sl[p]] for p in P]
    a = [a_ref[:, sl[p]] for p in P]
    kkn = [kk[p] * lax.rsqrt(seg_sum(kk[p] * kk[p]) + 1e-12) for p in P]
    bonus = [seg_sum(r[p] * k[p] * rk_ref[:, sl[p]]) * v[p] for p in P]
    cum = [_cumsum_rows(lw[p], tril) for p in P]
    inv = [jnp.exp(-cum[p]) for p in P]
    a2 = [stack(-kkn[p] * jnp.exp(cum[p] - lw[p])).astype(BF16) for p in P]
    r2 = [stack(r[p] * jnp.exp(cum[p])).astype(BF16) for p in P]
    b2 = [stack(kkn[p] * a[p] * inv[p]).astype(BF16) for p in P]
    k2 = [stack(k[p] * inv[p]).astype(BF16) for p in P]
    v2 = [stack(v[p]).astype(BF16) for p in P]
    ar = [jnp.concatenate([a2[p], r2[p]], axis=0) for p in P]
    bk = [jnp.concatenate([b2[p], k2[p]], axis=0) for p in P]
    gram = [_mm_nt(ar[p], bk[p]) for p in P]
    l_ab = [jnp.where(lower_strict, gram[p][:C2, :C2], 0.0).astype(BF16) for p in P]
    l_ak = [jnp.where(lower_strict, gram[p][:C2, C2:], 0.0).astype(BF16) for p in P]
    m_r = [jnp.concatenate([jnp.where(lower_incl, gram[p][C2:, :C2], 0.0),
                            jnp.where(lower_incl, gram[p][C2:, C2:], 0.0)], axis=1).astype(BF16) for p in P]
    s0 = [state_ref[p] for p in P]
    ars = [_mm_nt(ar[p], s0[p]) for p in P]
    x = [ars[p][:C2] + _mm(l_ak[p], v2[p]) for p in P]
    pw = l_ab
    levels = int(math.log2(C))
    for lvl in range(levels):
        if lvl + 1 < levels:
            prod = [_mm(pw[p], jnp.concatenate([x[p].astype(BF16), pw[p]], axis=1)) for p in P]
            x = [x[p] + prod[p][:, :C2] for p in P]
            pw = [prod[p][:, C2:].astype(BF16) for p in P]
        else:
            x = [x[p] + _mm(pw[p], x[p]) for p in P]
    u2 = [x[p].astype(BF16) for p in P]
    y2 = [ars[p][C2:] + _mm(m_r[p], jnp.concatenate([u2[p], v2[p]], axis=0)) for p in P]
    upd = [_mm_tn(jnp.concatenate([u2[p], v2[p]], axis=0), bk[p]) for p in P]
    for p in P:
        state_ref[p] = (s0[p] + upd[p]) * jnp.exp(cum[p][C - 1:C, :])
    y = [y2[p][:C] + y2[p][C:] for p in P]
    yc = [y[p] - seg_sum(y[p]) * (1.0 / N) for p in P]
    var = [seg_sum(yc[p] * yc[p]) * (1.0 / N) for p in P]
    for p in P:
        yn = yc[p] * lax.rsqrt(var[p] + GN_EPS) * lng_ref[:, sl[p]] + lnb_ref[:, sl[p]]
        o_ref[:, sl[p]] = ((yn + bonus[p]) * g_ref[:, sl[p]]).astype(o_ref.dtype)


def _rw_scan(r, lw, k, v, kk, a, g, r_k, ln_g, ln_b, l, *, npairs):
    T, W = r.shape
    C = RW_CHUNK
    wb = npairs * LANES
    blk = pl.BlockSpec((C, wb), lambda p, c: (c, p))
    vec = pl.BlockSpec((None, 1, wb), lambda p, c: (l, 0, p))
    return pl.pallas_call(
        functools.partial(_rw_scan_kernel, npairs=npairs),
        grid=(W // wb, T // C),
        in_specs=[blk] * 7 + [vec] * 3,
        out_specs=blk,
        out_shape=jax.ShapeDtypeStruct((T, W), BF16),
        scratch_shapes=[pltpu.VMEM((npairs, LANES, LANES), F32)],
        compiler_params=_cparams(("parallel", "arbitrary")),
        name="rw_scan",
    )(r, lw, k, v, kk, a, g, r_k, ln_g, ln_b)


def _out_proj_kernel(x_ref, sb_ref, rw_ref, wa_ref, wb_ref, o_ref):
    o_ref[...] = (x_ref[...]
                  + jnp.dot(sb_ref[...], wa_ref[...], preferred_element_type=F32)
                  + jnp.dot(rw_ref[...], wb_ref[...], preferred_element_type=F32))


def _out_proj(x, sb, rw, w_out, l, *, tm):
    T, D = x.shape
    Wa = sb.shape[1]
    Wb = rw.shape[1]
    assert Wa == Wb
    return pl.pallas_call(
        _out_proj_kernel,
        grid=(T // tm,),
        in_specs=[
            pl.BlockSpec((tm, D), lambda i: (i, 0)),
            pl.BlockSpec((tm, Wa), lambda i: (i, 0)),
            pl.BlockSpec((tm, Wb), lambda i: (i, 0)),
            pl.BlockSpec((None, Wa, D), lambda i: (l, 0, 0)),
            pl.BlockSpec((None, Wb, D), lambda i: (l, 1, 0)),
        ],
        out_specs=pl.BlockSpec((tm, D), lambda i: (i, 0)),
        out_shape=jax.ShapeDtypeStruct((T, D), F32),
        compiler_params=_cparams(("parallel",)),
        name="out_proj",
    )(x, sb, rw, w_out, w_out)


def _router_kernel(x_ref, g_ref, w_ref, b_ref, meta_ref, cnt_ref, carry_ref, *, tm):
    i = pl.program_id(0)

    @pl.when(i == 0)
    def _():
        carry_ref[...] = jnp.zeros_like(carry_ref)

    hn = _rms(x_ref[...]) * g_ref[...]
    logits = _mm_f32(hn, w_ref[...]) + b_ref[...]
    lane = lax.broadcasted_iota(jnp.int32, logits.shape, 1).astype(F32)
    big = float(LANES)
    neg = -jnp.inf
    is_grp = (lane >= N_EXPERTS) & (lane < N_EXPERTS + N_GROUPS)
    gl = jnp.where(is_grp, logits, neg)
    gmax = jnp.max(gl, axis=-1, keepdims=True)
    g_w = 1.0 / jnp.sum(jnp.where(is_grp, jnp.exp(gl - gmax), 0.0), axis=-1, keepdims=True)
    grp = jnp.min(jnp.where(gl == gmax, lane, big), axis=-1, keepdims=True) - N_EXPERTS
    lo = grp * EXPERTS_PER_GROUP
    in_grp = (lane >= lo) & (lane < lo + EXPERTS_PER_GROUP)
    es = jnp.where(in_grp, logits, neg)
    m1 = jnp.max(es, axis=-1, keepdims=True)
    i1 = jnp.min(jnp.where(es == m1, lane, big), axis=-1, keepdims=True)
    es2 = jnp.where(lane == i1, neg, es)
    m2 = jnp.max(es2, axis=-1, keepdims=True)
    i2 = jnp.min(jnp.where(es2 == m2, lane, big), axis=-1, keepdims=True)
    z = jnp.sum(jnp.where(in_grp, jnp.exp(es - m1), 0.0), axis=-1, keepdims=True)
    p1 = 1.0 / z
    p2 = jnp.exp(m2 - m1) / z
    w1 = g_w * (p1 / (p1 + p2))
    w2 = g_w * (p2 / (p1 + p2))
    oh1 = lane == i1
    oh2 = lane == i2
    oh = (oh1 | oh2).astype(BF16)
    row = lax.broadcasted_iota(jnp.int32, (tm, tm), 0)
    col = lax.broadcasted_iota(jnp.int32, (tm, tm), 1)
    before = jnp.dot((col < row).astype(BF16), oh, preferred_element_type=F32) + carry_ref[0:1, :]
    rank1 = jnp.sum(jnp.where(oh1, before, 0.0), axis=-1, keepdims=True)
    rank2 = jnp.sum(jnp.where(oh2, before, 0.0), axis=-1, keepdims=True)
    carry_ref[...] = carry_ref[...] + jnp.sum(oh.astype(F32), axis=0, keepdims=True)
    meta = jnp.where(lane == 0, i1, 0.0)
    meta = jnp.where(lane == 1, i2, meta)
    meta = jnp.where(lane == 2, w1, meta)
    meta = jnp.where(lane == 3, w2, meta)
    meta = jnp.where(lane == 4, rank1, meta)
    meta = jnp.where(lane == 5, rank2, meta)
    meta_ref[...] = meta
    cnt_ref[...] = carry_ref[...]


def _router(x, g, w, b, l, *, tm):
    T, D = x.shape
    return pl.pallas_call(
        functools.partial(_router_kernel, tm=tm),
        grid=(T // tm,),
        in_specs=[
            pl.BlockSpec((tm, D), lambda i: (i, 0)),
            _layer_vec(l, D),
            _layer_mat(l, (D, LANES)),
            _layer_vec(l, LANES),
        ],
        out_specs=[
            pl.BlockSpec((tm, LANES), lambda i: (i, 0)),
            pl.BlockSpec((SUBLANES, LANES), lambda i: (0, 0)),
        ],
        out_shape=[
            jax.ShapeDtypeStruct((T, LANES), F32),
            jax.ShapeDtypeStruct((SUBLANES, LANES), F32),
        ],
        scratch_shapes=[pltpu.VMEM((SUBLANES, LANES), F32)],
        compiler_params=_cparams(("arbitrary",)),
        name="router",
    )(x, g, w, b)


def _gather_rows_start(idx_ref, src_hbm, dst, sem):
    n = dst.shape[0]

    def body(grp, carry):
        base = pl.multiple_of(grp * GATHER_UNROLL, GATHER_UNROLL)
        for u in range(GATHER_UNROLL):
            t = idx_ref[0, 0, base + u]
            pltpu.make_async_copy(src_hbm.at[pl.ds(t, 1), :], dst.at[pl.ds(base + u, 1), :], sem).start()
        return carry

    lax.fori_loop(0, n // GATHER_UNROLL, body, 0)


def _gather_rows_wait(src_hbm, dst, sem):
    pltpu.make_async_copy(src_hbm.at[pl.ds(0, dst.shape[0]), :], dst, sem).wait()


def _scatter_rows_start(idx_ref, src, dst_hbm, sem):
    n = src.shape[0]

    def body(grp, carry):
        base = pl.multiple_of(grp * GATHER_UNROLL, GATHER_UNROLL)
        for u in range(GATHER_UNROLL):
            t = idx_ref[0, 0, base + u]
            pltpu.make_async_copy(src.at[pl.ds(base + u, 1), :], dst_hbm.at[pl.ds(t, 1), :], sem).start()
        return carry

    lax.fori_loop(0, n // GATHER_UNROLL, body, 0)


def _scatter_rows_wait(src, dst_hbm, sem):
    pltpu.make_async_copy(src, dst_hbm.at[pl.ds(0, src.shape[0]), :], sem).wait()


def _dispatch_kernel(d0_ref, d1_ref, x_ref, g_ref, zero_hbm, o_hbm, sbuf, sem):
    del zero_hbm
    i = pl.program_id(0)
    n = pl.num_programs(0)
    slot = i % 2
    sbuf[slot] = _rms(x_ref[...]) * g_ref[...]
    _scatter_rows_start(d0_ref, sbuf.at[slot], o_hbm, sem.at[slot])
    _scatter_rows_start(d1_ref, sbuf.at[slot], o_hbm, sem.at[slot])

    def drain(s):
        _scatter_rows_wait(sbuf.at[s], o_hbm, sem.at[s])
        _scatter_rows_wait(sbuf.at[s], o_hbm, sem.at[s])

    @pl.when(i > 0)
    def _():
        drain(1 - slot)

    @pl.when(i == n - 1)
    def _():
        drain(slot)


def _dispatch(x, g, dest0, dest1, n_rows, l):
    T, D = x.shape
    nt, _, tm = dest0.shape
    cur = pl.BlockSpec((1, 1, tm), lambda i: (i, 0, 0), memory_space=pltpu.SMEM)
    zeros = jnp.zeros((n_rows, D), F32)
    return pl.pallas_call(
        _dispatch_kernel,
        grid=(nt,),
        in_specs=[cur, cur, pl.BlockSpec((tm, D), lambda i: (i, 0)), _layer_vec(l, D),
                  pl.BlockSpec(memory_space=pl.ANY)],
        out_specs=pl.BlockSpec(memory_space=pl.ANY),
        out_shape=jax.ShapeDtypeStruct((n_rows, D), F32),
        scratch_shapes=[pltpu.VMEM((2, tm, D), F32), pltpu.SemaphoreType.DMA((2,))],
        input_output_aliases={4: 0},
        compiler_params=_cparams(("arbitrary",)),
        name="dispatch",
    )(dest0, dest1, x, g, zeros)


def _expert_kernel(be_ref, nused_ref, x_ref, wu_ref, wd_ref, o_ref, wu_bf, wd_bf):
    b = pl.program_id(0)
    n_used = nused_ref[0]
    cast_rows = 256

    @pl.when((b < n_used) & ((b == 0) | (be_ref[b] != be_ref[jnp.maximum(b - 1, 0)])))
    def _():
        def cast(ref, dst, c):
            rows = pl.ds(pl.multiple_of(c * cast_rows, cast_rows), cast_rows)
            dst[rows, :] = ref[rows, :].astype(BF16)

        def up(c, carry):
            cast(wu_ref, wu_bf, c)
            return carry

        def down(c, carry):
            cast(wd_ref, wd_bf, c)
            return carry

        lax.fori_loop(0, wu_bf.shape[0] // cast_rows, up, 0)
        lax.fori_loop(0, wd_bf.shape[0] // cast_rows, down, 0)

    @pl.when(b < n_used)
    def _():
        h = jnp.dot(x_ref[...].astype(BF16), wu_bf[...], preferred_element_type=F32)
        ff = h.shape[1] // 2
        gate = h[:, :ff]
        up = h[:, ff:]
        act = (gate * jax.nn.sigmoid(gate) * up).astype(BF16)
        o_ref[...] = jnp.dot(act, wd_bf[...], preferred_element_type=F32).astype(o_ref.dtype)

    @pl.when(b >= n_used)
    def _():
        o_ref[...] = jnp.zeros_like(o_ref)


def _expert_mlp(x_pad, blk_expert, n_used, w_up, w_down, l):
    R, D = x_pad.shape
    bm = MOE_ROWS
    F2 = w_up.shape[3]
    grid_spec = pltpu.PrefetchScalarGridSpec(
        num_scalar_prefetch=2,
        grid=(R // bm,),
        in_specs=[
            pl.BlockSpec((bm, D), lambda b, be, nu: (b, 0)),
            pl.BlockSpec((None, None, D, F2), lambda b, be, nu: (l, be[b], 0, 0)),
            pl.BlockSpec((None, None, F2 // 2, D), lambda b, be, nu: (l, be[b], 0, 0)),
        ],
        out_specs=pl.BlockSpec((bm, D), lambda b, be, nu: (b, 0)),
        scratch_shapes=[pltpu.VMEM((D, F2), BF16), pltpu.VMEM((F2 // 2, D), BF16)],
    )
    return pl.pallas_call(
        _expert_kernel,
        grid_spec=grid_spec,
        out_shape=jax.ShapeDtypeStruct((R, D), F32),
        compiler_params=_cparams(("arbitrary",)),
        name="expert_mlp",
    )(blk_expert, n_used, x_pad, w_up, w_down)


def _combine_ple_kernel(d0_ref, d1_ref, d0n_ref, d1n_ref, y_hbm, x_ref, meta_ref, p_ref, wg_ref, wp_ref, g_ref,
                        o_ref, ybuf, sem):
    i = pl.program_id(0)
    n = pl.num_programs(0)
    slot = i % 2

    def start(s, a_ref, b_ref):
        _gather_rows_start(a_ref, y_hbm, ybuf.at[s, 0], sem.at[s])
        _gather_rows_start(b_ref, y_hbm, ybuf.at[s, 1], sem.at[s])

    @pl.when(i == 0)
    def _():
        start(0, d0_ref, d1_ref)

    @pl.when(i + 1 < n)
    def _():
        start(1 - slot, d0n_ref, d1n_ref)

    _gather_rows_wait(y_hbm, ybuf.at[slot, 0], sem.at[slot])
    _gather_rows_wait(y_hbm, ybuf.at[slot, 1], sem.at[slot])
    meta = meta_ref[...]
    x = x_ref[...] + meta[:, 2:3] * ybuf[slot, 0] + meta[:, 3:4] * ybuf[slot, 1]
    gate = jax.nn.sigmoid(jnp.dot(_rms(x).astype(BF16), wg_ref[...], preferred_element_type=F32))
    pe = jnp.dot(p_ref[...].astype(BF16), wp_ref[...], preferred_element_type=F32)
    o_ref[...] = x + _rms(pe * gate) * g_ref[...]


def _combine_ple(x, y_pad, dest0, dest1, meta, p, w_gate, w_proj, g, l):
    T, D = x.shape
    P = p.shape[2]
    nt, _, tm = dest0.shape
    row = lambda w: pl.BlockSpec((tm, w), lambda i: (i, 0))
    cur = pl.BlockSpec((1, 1, tm), lambda i: (i, 0, 0), memory_space=pltpu.SMEM)
    nxt = pl.BlockSpec((1, 1, tm), lambda i: (jnp.minimum(i + 1, nt - 1), 0, 0), memory_space=pltpu.SMEM)
    return pl.pallas_call(
        _combine_ple_kernel,
        grid=(nt,),
        in_specs=[cur, cur, nxt, nxt, pl.BlockSpec(memory_space=pl.ANY),
                  row(D), row(LANES),
                  pl.BlockSpec((None, tm, P), lambda i: (l, i, 0)),
                  _layer_mat(l, (D, D)), _layer_mat(l, (P, D)), _layer_vec(l, D)],
        out_specs=row(D),
        out_shape=jax.ShapeDtypeStruct((T, D), F32),
        scratch_shapes=[pltpu.VMEM((2, 2, tm, D), F32), pltpu.SemaphoreType.DMA((2,))],
        compiler_params=_cparams(("arbitrary",)),
        name="combine_ple",
    )(dest0, dest1, dest0, dest1, y_pad, x, meta, p, w_gate, w_proj, g)


def _moe_layout(meta, counts, T, tm):
    bm = MOE_ROWS
    n_rows = 2 * T + N_EXPERTS * bm
    nb = n_rows // bm
    e = meta[:, 0:2].astype(jnp.int32)
    rank = meta[:, 4:6].astype(jnp.int32)
    cnt = counts[0, :N_EXPERTS].astype(jnp.int32)
    padded = ((cnt + bm - 1) // bm) * bm
    pad_ends = jnp.cumsum(padded)
    pad_starts = pad_ends - padded
    start_of = jnp.sum(jnp.where(e[..., None] == jnp.arange(N_EXPERTS, dtype=jnp.int32), pad_starts, 0), axis=-1)
    dest = start_of + rank
    blk_start = jnp.arange(nb, dtype=jnp.int32) * bm
    blk_expert = jnp.minimum(jnp.sum(pad_ends[None, :] <= blk_start[:, None], axis=1), N_EXPERTS - 1)
    n_used = (pad_ends[-1:] // bm).astype(jnp.int32)
    dest0 = dest[:, 0].reshape(T // tm, 1, tm)
    dest1 = dest[:, 1].reshape(T // tm, 1, tm)
    return dest0, dest1, n_rows, blk_expert.astype(jnp.int32), n_used


def kernel(x, p, ln1_g, w_in, sb_q_g, sb_k_g, sb_out_g, rw_mu, rw_w0, rw_w2, rw_a0, rw_a2, rw_g2, rw_k_k, rw_k_a, rw_r_k, rw_ln_g, rw_ln_b, w_out, ln2_g, router_g, router_g_b, router_e, router_e_b, w_up, w_down, ple_proj, ple_gate, ple_norm_g):
    B, S, D = x.shape
    depth = w_in.shape[0]
    T = B * S
    assert B == 1, "token shift and attention assume one sequence"
    sb_width = D // 2
    rw_width = D - sb_width
    n_sb_heads = sb_width // SB_HEAD_DIM
    lora = LORA_W + LORA_A + LORA_G
    col0 = 3 * sb_width
    W = rw_width
    tm = min(TOKEN_TILE, T)
    vec = lambda a: a.reshape(depth, 1, -1)

    w_in_b = jnp.pad(w_in, ((0, 0), (0, 0), (0, LORA_PAD - lora))).astype(BF16)
    w_out_b = w_out.astype(BF16)
    ple_gate_b = ple_gate.astype(BF16)
    ple_proj_b = ple_proj.astype(BF16)
    mu = vec(rw_mu)
    mu_r, mu_k, mu_v = mu[:, :, :W], mu[:, :, W:2 * W], mu[:, :, 2 * W:3 * W]
    mu_lo = jnp.pad(mu[:, :, 3 * W:], ((0, 0), (0, 0), (0, LORA_PAD - lora)))
    w2 = jnp.pad(rw_w2, ((0, 0), (0, LORA_A), (0, 0)))
    a2 = jnp.pad(rw_a2, ((0, 0), (LORA_W, 0), (0, 0)))
    g2 = jnp.pad(rw_g2, ((0, 0), (0, 2 * LANES - LORA_G), (0, 0)))
    w_r = jnp.pad(jnp.concatenate([router_e, router_g], axis=2),
                  ((0, 0), (0, 0), (0, LANES - N_EXPERTS - N_GROUPS)))
    b_r = vec(jnp.pad(jnp.concatenate([router_e_b, router_g_b], axis=1),
                      ((0, 0), (0, LANES - N_EXPERTS - N_GROUPS))))
    p3 = p.reshape(depth, T, -1)

    xf = x.reshape(T, D)
    for l in range(depth):
        proj = _norm_matmul(xf, vec(ln1_g), w_in_b, l, tm=min(1024, T), tn=512, out_dtype=F32, name="in_proj")
        sb = _sb_attention(proj, vec(sb_q_g), vec(sb_k_g), vec(sb_out_g), l, n_heads=n_sb_heads, tq=256, nsub=2)
        r, lw, k, v, kk, a, g = _rw_prep(
            proj, mu_r, mu_k, mu_v, mu_lo, vec(rw_w0), w2, vec(rw_a0), a2, g2, vec(rw_k_k), vec(rw_k_a), l,
            rw_width=rw_width, col0=col0, tm=tm)
        rw = _rw_scan(r, lw, k, v, kk, a, g, vec(rw_r_k), vec(rw_ln_g), vec(rw_ln_b), l, npairs=8)
        x1 = _out_proj(xf, sb, rw, w_out_b, l, tm=tm)
        meta, counts = _router(x1, vec(ln2_g), w_r, b_r, l, tm=tm)
        dest0, dest1, n_rows, blk_expert, n_used = _moe_layout(meta, counts, T, tm)
        x_pad = _dispatch(x1, vec(ln2_g), dest0, dest1, n_rows, l)
        y_pad = _expert_mlp(x_pad, blk_expert, n_used, w_up, w_down, l)
        xf = _combine_ple(x1, y_pad, dest0, dest1, meta, p3, ple_gate_b, ple_proj_b, vec(ple_norm_g), l)
    return xf.reshape(B, S, D)
```

```python
import functools
import math

import jax
import jax.numpy as jnp
from jax import lax
from jax.experimental import pallas as pl
from jax.experimental.pallas import tpu as pltpu

F32 = jnp.float32
BF16 = jnp.bfloat16

SB_HEAD_DIM = 128
RW_HEAD_DIM = 64
LORA_W = 64
LORA_A = 64
LORA_G = 160
LORA_PAD = 512
N_GROUPS = 4
EXPERTS_PER_GROUP = 8
N_EXPERTS = N_GROUPS * EXPERTS_PER_GROUP
RMS_EPS = 1e-6
GN_EPS = 64e-5
LANES = 128
SUBLANES = 8
VMEM_LIMIT = 56 * 1024 * 1024

EXP_UNDERFLOW = -104.0
RW_CHUNK = 64
MOE_ROWS = 256
TOKEN_TILE = 256
GATHER_UNROLL = 8


def _cparams(sem):
    return pltpu.CompilerParams(dimension_semantics=sem, vmem_limit_bytes=VMEM_LIMIT)


def _rms(x, eps=RMS_EPS):
    return x * lax.rsqrt(jnp.mean(x * x, axis=-1, keepdims=True) + eps)


def _softplus(y):
    return jnp.maximum(y, 0.0) + jnp.log(1.0 + jnp.exp(-jnp.abs(y)))


def _mm(a, b):
    return jnp.dot(a.astype(BF16), b.astype(BF16), preferred_element_type=F32)


def _mm_nt(a, b):
    return lax.dot_general(a.astype(BF16), b.astype(BF16), (((1,), (1,)), ((), ())),
                           preferred_element_type=F32)


def _mm_tn(a, b):
    return lax.dot_general(a.astype(BF16), b.astype(BF16), (((0,), (0,)), ((), ())),
                           preferred_element_type=F32)


def _mm_f32(a, b):
    return jnp.dot(a, b, preferred_element_type=F32, precision=lax.Precision.HIGHEST)


def _mm_split(a, b_exact):
    hi = a.astype(BF16)
    lo = (a - hi.astype(F32)).astype(BF16)
    return (jnp.dot(hi, b_exact, preferred_element_type=F32)
            + jnp.dot(lo, b_exact, preferred_element_type=F32))


def _layer_vec(l, width):
    return pl.BlockSpec((None, 1, width), lambda *_: (l, 0, 0))


def _layer_mat(l, shape):
    return pl.BlockSpec((None,) + tuple(shape), lambda *_: (l,) + (0,) * len(shape))


def _norm_matmul_kernel(x_ref, g_ref, w_ref, wt_ref, o_ref, xn_ref, *, n_main):
    j = pl.program_id(1)

    @pl.when(j == 0)
    def _():
        xn_ref[...] = (_rms(x_ref[...]) * g_ref[...]).astype(BF16)

    @pl.when(j < n_main)
    def _():
        o_ref[...] = jnp.dot(xn_ref[...], w_ref[...], preferred_element_type=F32).astype(o_ref.dtype)

    @pl.when(j >= n_main)
    def _():
        o_ref[...] = jnp.dot(xn_ref[...], wt_ref[...], preferred_element_type=F32).astype(o_ref.dtype)


def _norm_matmul(x, g, w, w_tail, l, *, tm, tn, out_dtype, name):
    T, D = x.shape
    n_main = w.shape[2] // tn
    assert w.shape[2] == n_main * tn and w_tail.shape[2] == tn
    return pl.pallas_call(
        functools.partial(_norm_matmul_kernel, n_main=n_main),
        grid=(T // tm, n_main + 1),
        in_specs=[
            pl.BlockSpec((tm, D), lambda i, j: (i, 0)),
            _layer_vec(l, D),
            pl.BlockSpec((None, D, tn), lambda i, j: (l, 0, jnp.minimum(j, n_main - 1))),
            _layer_mat(l, (D, tn)),
        ],
        out_specs=pl.BlockSpec((tm, tn), lambda i, j: (i, j)),
        out_shape=jax.ShapeDtypeStruct((T, (n_main + 1) * tn), out_dtype),
        scratch_shapes=[pltpu.VMEM((tm, D), BF16)],
        compiler_params=_cparams(("parallel", "arbitrary")),
        name=name,
    )(x, g, w, w_tail)


def _sb_attn_kernel(q_ref, k_ref, v_ref, qg_ref, kg_ref, og_ref, o_ref, kn_ref, vb_ref, *, tq, seq):
    i = pl.program_id(1)
    hd = SB_HEAD_DIM
    prep_rows = min(512, seq)

    @pl.when(i == 0)
    def _():
        def body(c, carry):
            rows = pl.ds(pl.multiple_of(c * prep_rows, prep_rows), prep_rows)
            kn_ref[rows, :] = (_rms(k_ref[rows, :]) * kg_ref[...]).astype(BF16)
            vb_ref[rows, :] = v_ref[rows, :].astype(BF16)
            return carry

        lax.fori_loop(0, seq // prep_rows, body, 0)

    nsub = q_ref.shape[0] // tq
    qn_all = (_rms(q_ref[...]) * qg_ref[...] * (1.0 / math.sqrt(hd))).astype(BF16)
    qn = [qn_all[s * tq:(s + 1) * tq] for s in range(nsub)]
    qi = [nsub * i + s for s in range(nsub)]
    row = lax.broadcasted_iota(jnp.int32, (tq, tq), 0)
    col = lax.broadcasted_iota(jnp.int32, (tq, tq), 1)
    tri = (row > col).astype(BF16)
    causal = col < row

    def scores(s, j, masked, valid=None):
        rows = pl.ds(pl.multiple_of(j * tq, tq), tq)
        z = _mm_nt(qn[s], kn_ref[rows, :])
        sp = _softplus(z)
        log_keep = -sp
        if masked:
            log_keep = jnp.where(causal, log_keep, 0.0)
        if valid is not None:
            log_keep = jnp.where(valid, log_keep, 0.0)
        later = _mm(log_keep, tri)
        return rows, z - sp, log_keep, later, masked, valid

    def accumulate(tile, carry, acc):
        rows, log_beta, log_keep, later, masked, valid = tile
        w = jnp.exp(log_beta + later + carry)
        if masked:
            w = jnp.where(causal, w, 0.0)
        if valid is not None:
            w = jnp.where(valid, w, 0.0)
        acc = acc + jnp.dot(w.astype(BF16), vb_ref[rows, :], preferred_element_type=F32)
        return carry + later[:, 0:1] + log_keep[:, 0:1], acc

    def live(carry):
        return (jnp.max(carry) > EXP_UNDERFLOW).astype(jnp.int32)

    diag = [scores(s, qi[s], True) for s in range(nsub)]
    prev = [scores(s, jnp.maximum(qi[s] - 1, 0), False, None if s > 0 else i > 0) for s in range(nsub)]
    state = []
    for s in range(nsub):
        carry, acc = accumulate(diag[s], jnp.zeros((tq, 1), F32), jnp.zeros((tq, hd), F32))
        state.append(accumulate(prev[s], carry, acc))
    for s in range(nsub):
        carry, acc = state[s]

        def cond(st, s=s):
            return (st[0] < qi[s]) & (st[1] > 0)

        def body(st, s=s):
            n, _, carry, acc = st
            carry, acc = accumulate(scores(s, qi[s] - 1 - n, False), carry, acc)
            return n + 1, live(carry), carry, acc

        _, _, carry, acc = lax.while_loop(cond, body, (jnp.int32(1), live(carry), carry, acc))
        o_ref[s * tq:(s + 1) * tq, :] = (_rms(acc) * og_ref[...]).astype(o_ref.dtype)


def _sb_attention(proj, q_g, k_g, out_g, l, *, n_heads, tq, nsub):
    T = proj.shape[0]
    hd = SB_HEAD_DIM
    tb = nsub * tq
    return pl.pallas_call(
        functools.partial(_sb_attn_kernel, tq=tq, seq=T),
        grid=(n_heads, T // tb),
        in_specs=[
            pl.BlockSpec((tb, hd), lambda h, i: (i, h)),
            pl.BlockSpec((T, hd), lambda h, i: (0, n_heads + h)),
            pl.BlockSpec((T, hd), lambda h, i: (0, 2 * n_heads + h)),
            _layer_vec(l, hd), _layer_vec(l, hd), _layer_vec(l, hd),
        ],
        out_specs=pl.BlockSpec((tb, hd), lambda h, i: (i, h)),
        out_shape=jax.ShapeDtypeStruct((T, n_heads * hd), BF16),
        scratch_shapes=[pltpu.VMEM((T, hd), BF16), pltpu.VMEM((T, hd), BF16)],
        compiler_params=_cparams(("parallel", "arbitrary")),
        name="sb_attention",
    )(proj, proj, proj, q_g, k_g, out_g)


def _rw_prep_kernel(r_ref, k_ref, v_ref, lo_ref, rp_ref, kp_ref, vp_ref, lop_ref,
                    mu_r_ref, mu_k_ref, mu_v_ref, mu_lo_ref, w0_ref, w2_ref, a0_ref, a2_ref, g2_ref,
                    kk_ref, ka_ref,
                    r_out, lw_out, k_out, v_out, kk_out, a_out, g_out):
    first = pl.program_id(0) == 0

    def shift(cur_ref, prev_ref, mu_ref):
        cur = cur_ref[...]
        prev_row = jnp.where(first, 0.0, prev_ref[SUBLANES - 1:SUBLANES, :])
        rolled = pltpu.roll(cur, 1, 0)
        rowi = lax.broadcasted_iota(jnp.int32, cur.shape, 0)
        prev = jnp.where(rowi == 0, prev_row, rolled)
        return cur + mu_ref[...] * (prev - cur)

    r = shift(r_ref, rp_ref, mu_r_ref)
    k = shift(k_ref, kp_ref, mu_k_ref)
    v = shift(v_ref, vp_ref, mu_v_ref)
    lo = shift(lo_ref, lop_ref, mu_lo_ref)
    wa_lo = lo[:, :LORA_W + LORA_A]
    g_lo = lo[:, LORA_W + LORA_A:LORA_W + LORA_A + g2_ref.shape[0]]
    w = -_softplus(-(w0_ref[...] + _mm_f32(jnp.tanh(wa_lo), w2_ref[...]))) - 0.5
    a = jax.nn.sigmoid(a0_ref[...] + _mm_f32(wa_lo, a2_ref[...]))
    g = _mm_f32(jax.nn.sigmoid(g_lo), g2_ref[...])
    r_out[...] = r
    lw_out[...] = -jnp.exp(w)
    k_out[...] = k * (1.0 + (a - 1.0) * ka_ref[...])
    v_out[...] = v
    kk_out[...] = k * kk_ref[...]
    a_out[...] = a
    g_out[...] = g


def _rw_prep(proj, mu_r, mu_k, mu_v, mu_lo, w0, w2, a0, a2, g2, k_k, k_a, l, *, rw_width, col0, tm):
    T = proj.shape[0]
    W = rw_width
    cb = col0 // W
    lb = (col0 + 3 * W) // LORA_PAD

    def cur(width, blk):
        return pl.BlockSpec((tm, width), lambda i: (i, blk))

    def prev(width, blk):
        return pl.BlockSpec((SUBLANES, width), lambda i: (jnp.maximum(i * (tm // SUBLANES) - 1, 0), blk))

    out = jax.ShapeDtypeStruct((T, W), F32)
    return pl.pallas_call(
        _rw_prep_kernel,
        grid=(T // tm,),
        in_specs=[cur(W, cb), cur(W, cb + 1), cur(W, cb + 2), cur(LORA_PAD, lb),
                  prev(W, cb), prev(W, cb + 1), prev(W, cb + 2), prev(LORA_PAD, lb),
                  _layer_vec(l, W), _layer_vec(l, W), _layer_vec(l, W), _layer_vec(l, LORA_PAD),
                  _layer_vec(l, W), _layer_mat(l, w2.shape[1:]), _layer_vec(l, W), _layer_mat(l, a2.shape[1:]),
                  _layer_mat(l, g2.shape[1:]), _layer_vec(l, W), _layer_vec(l, W)],
        out_specs=[pl.BlockSpec((tm, W), lambda i: (i, 0))] * 7,
        out_shape=[out] * 7,
        compiler_params=_cparams(("parallel",)),
        name="rw_prep",
    )(proj, proj, proj, proj, proj, proj, proj, proj,
      mu_r, mu_k, mu_v, mu_lo, w0, w2, a0, a2, g2, k_k, k_a)


def _cumsum_rows(x, tril_bf16):
    hi = x.astype(BF16)
    lo = (x - hi.astype(F32)).astype(BF16)
    return (jnp.dot(tril_bf16, hi, preferred_element_type=F32)
            + jnp.dot(tril_bf16, lo, preferred_element_type=F32))


def _rw_scan_kernel(r_ref, lw_ref, k_ref, v_ref, kk_ref, a_ref, g_ref, rk_ref, lng_ref, lnb_ref,
                    o_ref, state_ref, *, npairs):
    C = RW_CHUNK
    N = RW_HEAD_DIM
    C2 = 2 * C
    assert LANES == 2 * N and C2 == LANES

    @pl.when(pl.program_id(1) == 0)
    def _():
        state_ref[...] = jnp.zeros_like(state_ref)

    row = lax.broadcasted_iota(jnp.int32, (C2, C2), 0)
    col = lax.broadcasted_iota(jnp.int32, (C2, C2), 1)
    lower_incl = col <= row
    lower_strict = col < row
    seg_ones = ((row < N) == (col < N)).astype(BF16)
    tril = (lax.broadcasted_iota(jnp.int32, (C, C), 1) <= lax.broadcasted_iota(jnp.int32, (C, C), 0)).astype(BF16)
    head0 = lax.broadcasted_iota(jnp.int32, (C, LANES), 1) < N

    def stack(x):
        return jnp.concatenate([jnp.where(head0, x, 0.0), jnp.where(head0, 0.0, x)], axis=0)

    def seg_sum(x):
        return _mm(x, seg_ones)

    P = range(npairs)
    sl = [slice(p * LANES, (p + 1) * LANES) for p in P]
    r = [r_ref[:, sl[p]] for p in P]
    lw = [lw_ref[:, sl[p]] for p in P]
    k = [k_ref[:, sl[p]] for p in P]
    v = [v_ref[:, sl[p]] for p in P]
    kk = [kk_ref[:, sl[p]] for p in P]
    a = [a_ref[:, sl[p]] for p in P]
    kkn = [kk[p] * lax.rsqrt(seg_sum(kk[p] * kk[p]) + 1e-12) for p in P]
    bonus = [seg_sum(r[p] * k[p] * rk_ref[:, sl[p]]) * v[p] for p in P]
    cum = [_cumsum_rows(lw[p], tril) for p in P]
    inv = [jnp.exp(-cum[p]) for p in P]
    a2 = [stack(-kkn[p] * jnp.exp(cum[p] - lw[p])).astype(BF16) for p in P]
    r2 = [stack(r[p] * jnp.exp(cum[p])).astype(BF16) for p in P]
    b2 = [stack(kkn[p] * a[p] * inv[p]).astype(BF16) for p in P]
    k2 = [stack(k[p] * inv[p]).astype(BF16) for p in P]
    v2 = [stack(v[p]).astype(BF16) for p in P]
    ar = [jnp.concatenate([a2[p], r2[p]], axis=0) for p in P]
    bk = [jnp.concatenate([b2[p], k2[p]], axis=0) for p in P]
    gram = [_mm_nt(ar[p], bk[p]) for p in P]
    l_ab = [jnp.where(lower_strict, gram[p][:C2, :C2], 0.0).astype(BF16) for p in P]
    l_ak = [jnp.where(lower_strict, gram[p][:C2, C2:], 0.0).astype(BF16) for p in P]
    m_r = [jnp.concatenate([jnp.where(lower_incl, gram[p][C2:, :C2], 0.0),
                            jnp.where(lower_incl, gram[p][C2:, C2:], 0.0)], axis=1).astype(BF16) for p in P]
    s0 = [state_ref[p] for p in P]
    ars = [_mm_nt(ar[p], s0[p]) for p in P]
    x = [ars[p][:C2] + _mm(l_ak[p], v2[p]) for p in P]
    pw = l_ab
    levels = int(math.log2(C))
    for lvl in range(levels):
        if lvl + 1 < levels:
            prod = [_mm(pw[p], jnp.concatenate([x[p].astype(BF16), pw[p]], axis=1)) for p in P]
            x = [x[p] + prod[p][:, :C2] for p in P]
            pw = [prod[p][:, C2:].astype(BF16) for p in P]
        else:
            x = [x[p] + _mm(pw[p], x[p]) for p in P]
    u2 = [x[p].astype(BF16) for p in P]
    y2 = [ars[p][C2:] + _mm(m_r[p], jnp.concatenate([u2[p], v2[p]], axis=0)) for p in P]
    upd = [_mm_tn(jnp.concatenate([u2[p], v2[p]], axis=0), bk[p]) for p in P]
    for p in P:
        state_ref[p] = (s0[p] + upd[p]) * jnp.exp(cum[p][C - 1:C, :])
    y = [y2[p][:C] + y2[p][C:] for p in P]
    yc = [y[p] - seg_sum(y[p]) * (1.0 / N) for p in P]
    var = [seg_sum(yc[p] * yc[p]) * (1.0 / N) for p in P]
    for p in P:
        yn = yc[p] * lax.rsqrt(var[p] + GN_EPS) * lng_ref[:, sl[p]] + lnb_ref[:, sl[p]]
        o_ref[:, sl[p]] = ((yn + bonus[p]) * g_ref[:, sl[p]]).astype(o_ref.dtype)


def _rw_scan(r, lw, k, v, kk, a, g, r_k, ln_g, ln_b, l, *, npairs):
    T, W = r.shape
    C = RW_CHUNK
    wb = npairs * LANES
    blk = pl.BlockSpec((C, wb), lambda p, c: (c, p))
    vec = pl.BlockSpec((None, 1, wb), lambda p, c: (l, 0, p))
    return pl.pallas_call(
        functools.partial(_rw_scan_kernel, npairs=npairs),
        grid=(W // wb, T // C),
        in_specs=[blk] * 7 + [vec] * 3,
        out_specs=blk,
        out_shape=jax.ShapeDtypeStruct((T, W), BF16),
        scratch_shapes=[pltpu.VMEM((npairs, LANES, LANES), F32)],
        compiler_params=_cparams(("parallel", "arbitrary")),
        name="rw_scan",
    )(r, lw, k, v, kk, a, g, r_k, ln_g, ln_b)


def _out_proj_router_kernel(x_ref, sb_ref, rw_ref, wa_ref, wb_ref, g_ref, w_ref, b_ref,
                            x1_ref, meta_ref, cnt_ref, carry_ref, *, tm):
    i = pl.program_id(0)

    @pl.when(i == 0)
    def _():
        carry_ref[...] = jnp.zeros_like(carry_ref)

    x1 = (x_ref[...]
          + jnp.dot(sb_ref[...], wa_ref[...], preferred_element_type=F32)
          + jnp.dot(rw_ref[...], wb_ref[...], preferred_element_type=F32))
    x1_ref[...] = x1
    hn = _rms(x1) * g_ref[...]
    logits = _mm_f32(hn, w_ref[...]) + b_ref[...]
    lane = lax.broadcasted_iota(jnp.int32, logits.shape, 1).astype(F32)
    big = float(LANES)
    neg = -jnp.inf
    is_grp = (lane >= N_EXPERTS) & (lane < N_EXPERTS + N_GROUPS)
    gl = jnp.where(is_grp, logits, neg)
    gmax = jnp.max(gl, axis=-1, keepdims=True)
    g_w = 1.0 / jnp.sum(jnp.where(is_grp, jnp.exp(gl - gmax), 0.0), axis=-1, keepdims=True)
    grp = jnp.min(jnp.where(gl == gmax, lane, big), axis=-1, keepdims=True) - N_EXPERTS
    lo = grp * EXPERTS_PER_GROUP
    in_grp = (lane >= lo) & (lane < lo + EXPERTS_PER_GROUP)
    es = jnp.where(in_grp, logits, neg)
    m1 = jnp.max(es, axis=-1, keepdims=True)
    i1 = jnp.min(jnp.where(es == m1, lane, big), axis=-1, keepdims=True)
    es2 = jnp.where(lane == i1, neg, es)
    m2 = jnp.max(es2, axis=-1, keepdims=True)
    i2 = jnp.min(jnp.where(es2 == m2, lane, big), axis=-1, keepdims=True)
    z = jnp.sum(jnp.where(in_grp, jnp.exp(es - m1), 0.0), axis=-1, keepdims=True)
    p1 = 1.0 / z
    p2 = jnp.exp(m2 - m1) / z
    w1 = g_w * (p1 / (p1 + p2))
    w2 = g_w * (p2 / (p1 + p2))
    oh1 = lane == i1
    oh2 = lane == i2
    oh = (oh1 | oh2).astype(BF16)
    row = lax.broadcasted_iota(jnp.int32, (tm, tm), 0)
    col = lax.broadcasted_iota(jnp.int32, (tm, tm), 1)
    before = jnp.dot((col < row).astype(BF16), oh, preferred_element_type=F32) + carry_ref[0:1, :]
    rank1 = jnp.sum(jnp.where(oh1, before, 0.0), axis=-1, keepdims=True)
    rank2 = jnp.sum(jnp.where(oh2, before, 0.0), axis=-1, keepdims=True)
    carry_ref[...] = carry_ref[...] + jnp.sum(oh.astype(F32), axis=0, keepdims=True)
    meta = jnp.where(lane == 0, i1, 0.0)
    meta = jnp.where(lane == 1, i2, meta)
    meta = jnp.where(lane == 2, w1, meta)
    meta = jnp.where(lane == 3, w2, meta)
    meta = jnp.where(lane == 4, rank1, meta)
    meta = jnp.where(lane == 5, rank2, meta)
    meta_ref[...] = meta
    cnt_ref[...] = carry_ref[...]


def _out_proj_router(x, sb, rw, w_out, g, w, b, l, *, tm):
    T, D = x.shape
    Wa = sb.shape[1]
    Wb = rw.shape[1]
    assert Wa == Wb
    row = lambda width: pl.BlockSpec((tm, width), lambda i: (i, 0))
    return pl.pallas_call(
        functools.partial(_out_proj_router_kernel, tm=tm),
        grid=(T // tm,),
        in_specs=[
            row(D), row(Wa), row(Wb),
            pl.BlockSpec((None, Wa, D), lambda i: (l, 0, 0)),
            pl.BlockSpec((None, Wb, D), lambda i: (l, 1, 0)),
            _layer_vec(l, D),
            _layer_mat(l, (D, LANES)),
            _layer_vec(l, LANES),
        ],
        out_specs=[
            row(D), row(LANES),
            pl.BlockSpec((SUBLANES, LANES), lambda i: (0, 0)),
        ],
        out_shape=[
            jax.ShapeDtypeStruct((T, D), F32),
            jax.ShapeDtypeStruct((T, LANES), F32),
            jax.ShapeDtypeStruct((SUBLANES, LANES), F32),
        ],
        scratch_shapes=[pltpu.VMEM((SUBLANES, LANES), F32)],
        compiler_params=_cparams(("arbitrary",)),
        name="out_proj_router",
    )(x, sb, rw, w_out, w_out, g, w, b)


def _gather_rows_start(idx_ref, src_hbm, dst, sem):
    n = dst.shape[0]

    def body(grp, carry):
        base = pl.multiple_of(grp * GATHER_UNROLL, GATHER_UNROLL)
        for u in range(GATHER_UNROLL):
            t = idx_ref[0, 0, base + u]
            pltpu.make_async_copy(src_hbm.at[pl.ds(t, 1), :], dst.at[pl.ds(base + u, 1), :], sem).start()
        return carry

    lax.fori_loop(0, n // GATHER_UNROLL, body, 0)


def _gather_rows_wait(src_hbm, dst, sem):
    pltpu.make_async_copy(src_hbm.at[pl.ds(0, dst.shape[0]), :], dst, sem).wait()


def _scatter_rows_start(idx_ref, src, dst_hbm, sem):
    n = src.shape[0]

    def body(grp, carry):
        base = pl.multiple_of(grp * GATHER_UNROLL, GATHER_UNROLL)
        for u in range(GATHER_UNROLL):
            t = idx_ref[0, 0, base + u]
            pltpu.make_async_copy(src.at[pl.ds(base + u, 1), :], dst_hbm.at[pl.ds(t, 1), :], sem).start()
        return carry

    lax.fori_loop(0, n // GATHER_UNROLL, body, 0)


def _scatter_rows_wait(src, dst_hbm, sem):
    pltpu.make_async_copy(src, dst_hbm.at[pl.ds(0, src.shape[0]), :], sem).wait()


def _dispatch_kernel(d0_ref, d1_ref, x_ref, g_ref, zero_hbm, o_hbm, sbuf, sem):
    del zero_hbm
    i = pl.program_id(0)
    n = pl.num_programs(0)
    slot = i % 2
    sbuf[slot] = _rms(x_ref[...]) * g_ref[...]
    _scatter_rows_start(d0_ref, sbuf.at[slot], o_hbm, sem.at[slot])
    _scatter_rows_start(d1_ref, sbuf.at[slot], o_hbm, sem.at[slot])

    def drain(s):
        _scatter_rows_wait(sbuf.at[s], o_hbm, sem.at[s])
        _scatter_rows_wait(sbuf.at[s], o_hbm, sem.at[s])

    @pl.when(i > 0)
    def _():
        drain(1 - slot)

    @pl.when(i == n - 1)
    def _():
        drain(slot)


def _dispatch(x, g, dest0, dest1, n_rows, l):
    T, D = x.shape
    nt, _, tm = dest0.shape
    cur = pl.BlockSpec((1, 1, tm), lambda i: (i, 0, 0), memory_space=pltpu.SMEM)
    zeros = jnp.zeros((n_rows, D), F32)
    return pl.pallas_call(
        _dispatch_kernel,
        grid=(nt,),
        in_specs=[cur, cur, pl.BlockSpec((tm, D), lambda i: (i, 0)), _layer_vec(l, D),
                  pl.BlockSpec(memory_space=pl.ANY)],
        out_specs=pl.BlockSpec(memory_space=pl.ANY),
        out_shape=jax.ShapeDtypeStruct((n_rows, D), F32),
        scratch_shapes=[pltpu.VMEM((2, tm, D), F32), pltpu.SemaphoreType.DMA((2,))],
        input_output_aliases={4: 0},
        compiler_params=_cparams(("arbitrary",)),
        name="dispatch",
    )(dest0, dest1, x, g, zeros)


def _expert_kernel(be_ref, nused_ref, x_ref, wu_ref, wd_ref, o_ref, wu_bf, wd_bf):
    b = pl.program_id(0)
    n_used = nused_ref[0]
    cast_rows = 256

    @pl.when((b < n_used) & ((b == 0) | (be_ref[b] != be_ref[jnp.maximum(b - 1, 0)])))
    def _():
        def cast(ref, dst, c):
            rows = pl.ds(pl.multiple_of(c * cast_rows, cast_rows), cast_rows)
            dst[rows, :] = ref[rows, :].astype(BF16)

        def up(c, carry):
            cast(wu_ref, wu_bf, c)
            return carry

        def down(c, carry):
            cast(wd_ref, wd_bf, c)
            return carry

        lax.fori_loop(0, wu_bf.shape[0] // cast_rows, up, 0)
        lax.fori_loop(0, wd_bf.shape[0] // cast_rows, down, 0)

    @pl.when(b < n_used)
    def _():
        h = jnp.dot(x_ref[...].astype(BF16), wu_bf[...], preferred_element_type=F32)
        ff = h.shape[1] // 2
        gate = h[:, :ff]
        up = h[:, ff:]
        act = (gate * jax.nn.sigmoid(gate) * up).astype(BF16)
        o_ref[...] = jnp.dot(act, wd_bf[...], preferred_element_type=F32).astype(o_ref.dtype)

    @pl.when(b >= n_used)
    def _():
        o_ref[...] = jnp.zeros_like(o_ref)


def _expert_mlp(x_pad, blk_expert, n_used, w_up, w_down, l):
    R, D = x_pad.shape
    bm = MOE_ROWS
    F2 = w_up.shape[3]
    grid_spec = pltpu.PrefetchScalarGridSpec(
        num_scalar_prefetch=2,
        grid=(R // bm,),
        in_specs=[
            pl.BlockSpec((bm, D), lambda b, be, nu: (b, 0)),
            pl.BlockSpec((None, None, D, F2), lambda b, be, nu: (l, be[b], 0, 0)),
            pl.BlockSpec((None, None, F2 // 2, D), lambda b, be, nu: (l, be[b], 0, 0)),
        ],
        out_specs=pl.BlockSpec((bm, D), lambda b, be, nu: (b, 0)),
        scratch_shapes=[pltpu.VMEM((D, F2), BF16), pltpu.VMEM((F2 // 2, D), BF16)],
    )
    return pl.pallas_call(
        _expert_kernel,
        grid_spec=grid_spec,
        out_shape=jax.ShapeDtypeStruct((R, D), F32),
        compiler_params=_cparams(("arbitrary",)),
        name="expert_mlp",
    )(blk_expert, n_used, x_pad, w_up, w_down)


def _combine_ple_kernel(d0_ref, d1_ref, d0n_ref, d1n_ref, y_hbm, x_ref, meta_ref, p_ref, wg_ref, wp_ref, g_ref,
                        o_ref, ybuf, sem):
    i = pl.program_id(0)
    n = pl.num_programs(0)
    slot = i % 2

    def start(s, a_ref, b_ref):
        _gather_rows_start(a_ref, y_hbm, ybuf.at[s, 0], sem.at[s])
        _gather_rows_start(b_ref, y_hbm, ybuf.at[s, 1], sem.at[s])

    @pl.when(i == 0)
    def _():
        start(0, d0_ref, d1_ref)

    @pl.when(i + 1 < n)
    def _():
        start(1 - slot, d0n_ref, d1n_ref)

    _gather_rows_wait(y_hbm, ybuf.at[slot, 0], sem.at[slot])
    _gather_rows_wait(y_hbm, ybuf.at[slot, 1], sem.at[slot])
    meta = meta_ref[...]
    x = x_ref[...] + meta[:, 2:3] * ybuf[slot, 0] + meta[:, 3:4] * ybuf[slot, 1]
    gate = jax.nn.sigmoid(jnp.dot(_rms(x).astype(BF16), wg_ref[...], preferred_element_type=F32))
    pe = jnp.dot(p_ref[...].astype(BF16), wp_ref[...], preferred_element_type=F32)
    o_ref[...] = x + _rms(pe * gate) * g_ref[...]


def _combine_ple(x, y_pad, dest0, dest1, meta, p, w_gate, w_proj, g, l):
    T, D = x.shape
    P = p.shape[2]
    nt, _, tm = dest0.shape
    row = lambda w: pl.BlockSpec((tm, w), lambda i: (i, 0))
    cur = pl.BlockSpec((1, 1, tm), lambda i: (i, 0, 0), memory_space=pltpu.SMEM)
    nxt = pl.BlockSpec((1, 1, tm), lambda i: (jnp.minimum(i + 1, nt - 1), 0, 0), memory_space=pltpu.SMEM)
    return pl.pallas_call(
        _combine_ple_kernel,
        grid=(nt,),
        in_specs=[cur, cur, nxt, nxt, pl.BlockSpec(memory_space=pl.ANY),
                  row(D), row(LANES),
                  pl.BlockSpec((None, tm, P), lambda i: (l, i, 0)),
                  _layer_mat(l, (D, D)), _layer_mat(l, (P, D)), _layer_vec(l, D)],
        out_specs=row(D),
        out_shape=jax.ShapeDtypeStruct((T, D), F32),
        scratch_shapes=[pltpu.VMEM((2, 2, tm, D), F32), pltpu.SemaphoreType.DMA((2,))],
        compiler_params=_cparams(("arbitrary",)),
        name="combine_ple",
    )(dest0, dest1, dest0, dest1, y_pad, x, meta, p, w_gate, w_proj, g)


def _moe_layout(meta, counts, T, tm):
    bm = MOE_ROWS
    n_rows = 2 * T + N_EXPERTS * bm
    nb = n_rows // bm
    e = meta[:, 0:2].astype(jnp.int32)
    rank = meta[:, 4:6].astype(jnp.int32)
    cnt = counts[0, :N_EXPERTS].astype(jnp.int32)
    padded = ((cnt + bm - 1) // bm) * bm
    pad_ends = jnp.cumsum(padded)
    pad_starts = pad_ends - padded
    start_of = jnp.sum(jnp.where(e[..., None] == jnp.arange(N_EXPERTS, dtype=jnp.int32), pad_starts, 0), axis=-1)
    dest = start_of + rank
    blk_start = jnp.arange(nb, dtype=jnp.int32) * bm
    blk_expert = jnp.minimum(jnp.sum(pad_ends[None, :] <= blk_start[:, None], axis=1), N_EXPERTS - 1)
    n_used = (pad_ends[-1:] // bm).astype(jnp.int32)
    dest0 = dest[:, 0].reshape(T // tm, 1, tm)
    dest1 = dest[:, 1].reshape(T // tm, 1, tm)
    return dest0, dest1, n_rows, blk_expert.astype(jnp.int32), n_used


def kernel(x, p, ln1_g, w_in, sb_q_g, sb_k_g, sb_out_g, rw_mu, rw_w0, rw_w2, rw_a0, rw_a2, rw_g2, rw_k_k, rw_k_a, rw_r_k, rw_ln_g, rw_ln_b, w_out, ln2_g, router_g, router_g_b, router_e, router_e_b, w_up, w_down, ple_proj, ple_gate, ple_norm_g):
    B, S, D = x.shape
    depth = w_in.shape[0]
    T = B * S
    assert B == 1, "token shift and attention assume one sequence"
    sb_width = D // 2
    rw_width = D - sb_width
    n_sb_heads = sb_width // SB_HEAD_DIM
    lora = LORA_W + LORA_A + LORA_G
    col0 = 3 * sb_width
    W = rw_width
    tm = min(TOKEN_TILE, T)
    vec = lambda a: a.reshape(depth, 1, -1)

    n_main = col0 + 3 * W
    w_in_b = w_in[:, :, :n_main].astype(BF16)
    w_lora_b = jnp.pad(w_in[:, :, n_main:], ((0, 0), (0, 0), (0, LORA_PAD - lora))).astype(BF16)
    w_out_b = w_out.astype(BF16)
    ple_gate_b = ple_gate.astype(BF16)
    ple_proj_b = ple_proj.astype(BF16)
    mu = vec(rw_mu)
    mu_r, mu_k, mu_v = mu[:, :, :W], mu[:, :, W:2 * W], mu[:, :, 2 * W:3 * W]
    mu_lo = jnp.pad(mu[:, :, 3 * W:], ((0, 0), (0, 0), (0, LORA_PAD - lora)))
    w2 = jnp.pad(rw_w2, ((0, 0), (0, LORA_A), (0, 0)))
    a2 = jnp.pad(rw_a2, ((0, 0), (LORA_W, 0), (0, 0)))
    g2 = jnp.pad(rw_g2, ((0, 0), (0, 2 * LANES - LORA_G), (0, 0)))
    w_r = jnp.pad(jnp.concatenate([router_e, router_g], axis=2),
                  ((0, 0), (0, 0), (0, LANES - N_EXPERTS - N_GROUPS)))
    b_r = vec(jnp.pad(jnp.concatenate([router_e_b, router_g_b], axis=1),
                      ((0, 0), (0, LANES - N_EXPERTS - N_GROUPS))))
    p3 = p.reshape(depth, T, -1)

    xf = x.reshape(T, D)
    for l in range(depth):
        proj = _norm_matmul(xf, vec(ln1_g), w_in_b, w_lora_b, l, tm=min(1024, T), tn=LORA_PAD, out_dtype=F32,
                            name="in_proj")
        sb = _sb_attention(proj, vec(sb_q_g), vec(sb_k_g), vec(sb_out_g), l, n_heads=n_sb_heads, tq=256,
                           nsub=min(4, T // 256))
        r, lw, k, v, kk, a, g = _rw_prep(
            proj, mu_r, mu_k, mu_v, mu_lo, vec(rw_w0), w2, vec(rw_a0), a2, g2, vec(rw_k_k), vec(rw_k_a), l,
            rw_width=rw_width, col0=col0, tm=tm)
        rw = _rw_scan(r, lw, k, v, kk, a, g, vec(rw_r_k), vec(rw_ln_g), vec(rw_ln_b), l, npairs=8)
        x1, meta, counts = _out_proj_router(xf, sb, rw, w_out_b, vec(ln2_g), w_r, b_r, l, tm=tm)
        dest0, dest1, n_rows, blk_expert, n_used = _moe_layout(meta, counts, T, tm)
        x_pad = _dispatch(x1, vec(ln2_g), dest0, dest1, n_rows, l)
        y_pad = _expert_mlp(x_pad, blk_expert, n_used, w_up, w_down, l)
        xf = _combine_ple(x1, y_pad, dest0, dest1, meta, p3, ple_gate_b, ple_proj_b, vec(ple_norm_g), l)
    return xf.reshape(B, S, D)
```

```python
import functools
import math

import jax
import jax.numpy as jnp
from jax import lax
from jax.experimental import pallas as pl
from jax.experimental.pallas import tpu as pltpu

F32 = jnp.float32
BF16 = jnp.bfloat16

SB_HEAD_DIM = 128
RW_HEAD_DIM = 64
LORA_W = 64
LORA_A = 64
LORA_G = 160
LORA_PAD = 512
N_GROUPS = 4
EXPERTS_PER_GROUP = 8
N_EXPERTS = N_GROUPS * EXPERTS_PER_GROUP
RMS_EPS = 1e-6
GN_EPS = 64e-5
LANES = 128
SUBLANES = 8
VMEM_LIMIT = 56 * 1024 * 1024

EXP_UNDERFLOW = -104.0
RW_CHUNK = 64
MOE_ROWS = 256
TOKEN_TILE = 256
GATHER_UNROLL = 8


def _cparams(sem):
    return pltpu.CompilerParams(dimension_semantics=sem, vmem_limit_bytes=VMEM_LIMIT)


def _rms(x, eps=RMS_EPS):
    return x * lax.rsqrt(jnp.mean(x * x, axis=-1, keepdims=True) + eps)


def _softplus(y):
    return jnp.maximum(y, 0.0) + jnp.log(1.0 + jnp.exp(-jnp.abs(y)))


def _mm(a, b):
    return jnp.dot(a.astype(BF16), b.astype(BF16), preferred_element_type=F32)


def _mm_nt(a, b):
    return lax.dot_general(a.astype(BF16), b.astype(BF16), (((1,), (1,)), ((), ())),
                           preferred_element_type=F32)


def _mm_tn(a, b):
    return lax.dot_general(a.astype(BF16), b.astype(BF16), (((0,), (0,)), ((), ())),
                           preferred_element_type=F32)


def _mm_f32(a, b):
    return jnp.dot(a, b, preferred_element_type=F32, precision=lax.Precision.HIGHEST)


def _mm_split(a, b_exact):
    hi = a.astype(BF16)
    lo = (a - hi.astype(F32)).astype(BF16)
    return (jnp.dot(hi, b_exact, preferred_element_type=F32)
            + jnp.dot(lo, b_exact, preferred_element_type=F32))


def _layer_vec(l, width):
    return pl.BlockSpec((None, 1, width), lambda *_: (l, 0, 0))


def _layer_mat(l, shape):
    return pl.BlockSpec((None,) + tuple(shape), lambda *_: (l,) + (0,) * len(shape))


def _norm_matmul_kernel(x_ref, g_ref, w_ref, wt_ref, o_ref, xn_ref, *, n_main):
    j = pl.program_id(1)

    @pl.when(j == 0)
    def _():
        xn_ref[...] = (_rms(x_ref[...]) * g_ref[...]).astype(BF16)

    @pl.when(j < n_main)
    def _():
        o_ref[...] = jnp.dot(xn_ref[...], w_ref[...].astype(BF16), preferred_element_type=F32).astype(o_ref.dtype)

    @pl.when(j >= n_main)
    def _():
        o_ref[...] = jnp.dot(xn_ref[...], wt_ref[...], preferred_element_type=F32).astype(o_ref.dtype)


def _norm_matmul(x, g, w, w_tail, l, *, n_cols, tm, tn, out_dtype, name):
    T, D = x.shape
    n_main = n_cols // tn
    assert n_cols == n_main * tn and w.shape[2] >= n_cols and w_tail.shape[2] == tn
    return pl.pallas_call(
        functools.partial(_norm_matmul_kernel, n_main=n_main),
        grid=(T // tm, n_main + 1),
        in_specs=[
            pl.BlockSpec((tm, D), lambda i, j: (i, 0)),
            _layer_vec(l, D),
            pl.BlockSpec((None, D, tn), lambda i, j: (l, 0, jnp.minimum(j, n_main - 1))),
            _layer_mat(l, (D, tn)),
        ],
        out_specs=pl.BlockSpec((tm, tn), lambda i, j: (i, j)),
        out_shape=jax.ShapeDtypeStruct((T, (n_main + 1) * tn), out_dtype),
        scratch_shapes=[pltpu.VMEM((tm, D), BF16)],
        compiler_params=_cparams(("parallel", "arbitrary")),
        name=name,
    )(x, g, w, w_tail)


def _sb_attn_kernel(q_ref, k_ref, v_ref, qg_ref, kg_ref, og_ref, o_ref, kn_ref, vb_ref, *, tq, seq):
    i = pl.program_id(1)
    hd = SB_HEAD_DIM
    prep_rows = min(512, seq)

    @pl.when(i == 0)
    def _():
        def body(c, carry):
            rows = pl.ds(pl.multiple_of(c * prep_rows, prep_rows), prep_rows)
            kn_ref[rows, :] = (_rms(k_ref[rows, :]) * kg_ref[...]).astype(BF16)
            vb_ref[rows, :] = v_ref[rows, :].astype(BF16)
            return carry

        lax.fori_loop(0, seq // prep_rows, body, 0)

    nsub = q_ref.shape[0] // tq
    qn_all = (_rms(q_ref[...]) * qg_ref[...] * (1.0 / math.sqrt(hd))).astype(BF16)
    qn = [qn_all[s * tq:(s + 1) * tq] for s in range(nsub)]
    qi = [nsub * i + s for s in range(nsub)]
    row = lax.broadcasted_iota(jnp.int32, (tq, tq), 0)
    col = lax.broadcasted_iota(jnp.int32, (tq, tq), 1)
    tri = (row > col).astype(BF16)
    causal = col < row

    def scores(s, j, masked, valid=None):
        rows = pl.ds(pl.multiple_of(j * tq, tq), tq)
        z = _mm_nt(qn[s], kn_ref[rows, :])
        sp = _softplus(z)
        log_keep = -sp
        if masked:
            log_keep = jnp.where(causal, log_keep, 0.0)
        if valid is not None:
            log_keep = jnp.where(valid, log_keep, 0.0)
        later = _mm(log_keep, tri)
        return rows, z - sp, log_keep, later, masked, valid

    def accumulate(tile, carry, acc):
        rows, log_beta, log_keep, later, masked, valid = tile
        w = jnp.exp(log_beta + later + carry)
        if masked:
            w = jnp.where(causal, w, 0.0)
        if valid is not None:
            w = jnp.where(valid, w, 0.0)
        acc = acc + jnp.dot(w.astype(BF16), vb_ref[rows, :], preferred_element_type=F32)
        return carry + later[:, 0:1] + log_keep[:, 0:1], acc

    def live(carry):
        return (jnp.max(carry) > EXP_UNDERFLOW).astype(jnp.int32)

    diag = [scores(s, qi[s], True) for s in range(nsub)]
    prev = [scores(s, jnp.maximum(qi[s] - 1, 0), False, None if s > 0 else i > 0) for s in range(nsub)]
    state = []
    for s in range(nsub):
        carry, acc = accumulate(diag[s], jnp.zeros((tq, 1), F32), jnp.zeros((tq, hd), F32))
        state.append(accumulate(prev[s], carry, acc))
    for s in range(nsub):
        carry, acc = state[s]

        def cond(st, s=s):
            return (st[0] < qi[s]) & (st[1] > 0)

        def body(st, s=s):
            n, _, carry, acc = st
            carry, acc = accumulate(scores(s, qi[s] - 1 - n, False), carry, acc)
            return n + 1, live(carry), carry, acc

        _, _, carry, acc = lax.while_loop(cond, body, (jnp.int32(1), live(carry), carry, acc))
        o_ref[s * tq:(s + 1) * tq, :] = (_rms(acc) * og_ref[...]).astype(o_ref.dtype)


def _sb_attention(proj, q_g, k_g, out_g, l, *, n_heads, tq, nsub):
    T = proj.shape[0]
    hd = SB_HEAD_DIM
    tb = nsub * tq
    return pl.pallas_call(
        functools.partial(_sb_attn_kernel, tq=tq, seq=T),
        grid=(n_heads, T // tb),
        in_specs=[
            pl.BlockSpec((tb, hd), lambda h, i: (i, h)),
            pl.BlockSpec((T, hd), lambda h, i: (0, n_heads + h)),
            pl.BlockSpec((T, hd), lambda h, i: (0, 2 * n_heads + h)),
            _layer_vec(l, hd), _layer_vec(l, hd), _layer_vec(l, hd),
        ],
        out_specs=pl.BlockSpec((tb, hd), lambda h, i: (i, h)),
        out_shape=jax.ShapeDtypeStruct((T, n_heads * hd), BF16),
        scratch_shapes=[pltpu.VMEM((T, hd), BF16), pltpu.VMEM((T, hd), BF16)],
        compiler_params=_cparams(("parallel", "arbitrary")),
        name="sb_attention",
    )(proj, proj, proj, q_g, k_g, out_g)


def _rw_prep_kernel(r_ref, k_ref, v_ref, lo_ref, rp_ref, kp_ref, vp_ref, lop_ref,
                    mu_r_ref, mu_k_ref, mu_v_ref, mu_lo_ref, w0_ref, w2_ref, a0_ref, a2_ref, g2_ref,
                    kk_ref, ka_ref,
                    r_out, lw_out, k_out, v_out, kk_out, a_out, g_out):
    first = pl.program_id(0) == 0

    def shift(cur_ref, prev_ref, mu_ref):
        cur = cur_ref[...]
        prev_row = jnp.where(first, 0.0, prev_ref[SUBLANES - 1:SUBLANES, :])
        rolled = pltpu.roll(cur, 1, 0)
        rowi = lax.broadcasted_iota(jnp.int32, cur.shape, 0)
        prev = jnp.where(rowi == 0, prev_row, rolled)
        return cur + mu_ref[...] * (prev - cur)

    r = shift(r_ref, rp_ref, mu_r_ref)
    k = shift(k_ref, kp_ref, mu_k_ref)
    v = shift(v_ref, vp_ref, mu_v_ref)
    lo = shift(lo_ref, lop_ref, mu_lo_ref)
    wa_lo = lo[:, :LORA_W + LORA_A]
    g_lo = lo[:, LORA_W + LORA_A:LORA_W + LORA_A + g2_ref.shape[0]]
    w = -_softplus(-(w0_ref[...] + _mm_f32(jnp.tanh(wa_lo), w2_ref[...]))) - 0.5
    a = jax.nn.sigmoid(a0_ref[...] + _mm_f32(wa_lo, a2_ref[...]))
    g = _mm_f32(jax.nn.sigmoid(g_lo), g2_ref[...])
    r_out[...] = r
    lw_out[...] = -jnp.exp(w)
    k_out[...] = k * (1.0 + (a - 1.0) * ka_ref[...])
    v_out[...] = v
    kk_out[...] = k * kk_ref[...]
    a_out[...] = a
    g_out[...] = g


def _rw_prep(proj, mu_r, mu_k, mu_v, mu_lo, w0, w2, a0, a2, g2, k_k, k_a, l, *, rw_width, col0, tm):
    T = proj.shape[0]
    W = rw_width
    cb = col0 // W
    lb = (col0 + 3 * W) // LORA_PAD

    def cur(width, blk):
        return pl.BlockSpec((tm, width), lambda i: (i, blk))

    def prev(width, blk):
        return pl.BlockSpec((SUBLANES, width), lambda i: (jnp.maximum(i * (tm // SUBLANES) - 1, 0), blk))

    out = jax.ShapeDtypeStruct((T, W), F32)
    return pl.pallas_call(
        _rw_prep_kernel,
        grid=(T // tm,),
        in_specs=[cur(W, cb), cur(W, cb + 1), cur(W, cb + 2), cur(LORA_PAD, lb),
                  prev(W, cb), prev(W, cb + 1), prev(W, cb + 2), prev(LORA_PAD, lb),
                  _layer_vec(l, W), _layer_vec(l, W), _layer_vec(l, W), _layer_vec(l, LORA_PAD),
                  _layer_vec(l, W), _layer_mat(l, w2.shape[1:]), _layer_vec(l, W), _layer_mat(l, a2.shape[1:]),
                  _layer_mat(l, g2.shape[1:]), _layer_vec(l, W), _layer_vec(l, W)],
        out_specs=[pl.BlockSpec((tm, W), lambda i: (i, 0))] * 7,
        out_shape=[out] * 7,
        compiler_params=_cparams(("parallel",)),
        name="rw_prep",
    )(proj, proj, proj, proj, proj, proj, proj, proj,
      mu_r, mu_k, mu_v, mu_lo, w0, w2, a0, a2, g2, k_k, k_a)


def _cumsum_rows(x, tril_bf16):
    hi = x.astype(BF16)
    lo = (x - hi.astype(F32)).astype(BF16)
    return (jnp.dot(tril_bf16, hi, preferred_element_type=F32)
            + jnp.dot(tril_bf16, lo, preferred_element_type=F32))


def _rw_scan_kernel(r_ref, lw_ref, k_ref, v_ref, kk_ref, a_ref, g_ref, rk_ref, lng_ref, lnb_ref,
                    o_ref, state_ref, *, npairs):
    C = RW_CHUNK
    N = RW_HEAD_DIM
    C2 = 2 * C
    assert LANES == 2 * N and C2 == LANES

    @pl.when(pl.program_id(1) == 0)
    def _():
        state_ref[...] = jnp.zeros_like(state_ref)

    row = lax.broadcasted_iota(jnp.int32, (C2, C2), 0)
    col = lax.broadcasted_iota(jnp.int32, (C2, C2), 1)
    lower_incl = col <= row
    lower_strict = col < row
    seg_ones = ((row < N) == (col < N)).astype(BF16)
    tril = (lax.broadcasted_iota(jnp.int32, (C, C), 1) <= lax.broadcasted_iota(jnp.int32, (C, C), 0)).astype(BF16)
    head0 = lax.broadcasted_iota(jnp.int32, (C, LANES), 1) < N

    def stack(x):
        return jnp.concatenate([jnp.where(head0, x, 0.0), jnp.where(head0, 0.0, x)], axis=0)

    def seg_sum(x):
        return _mm(x, seg_ones)

    P = range(npairs)
    sl = [slice(p * LANES, (p + 1) * LANES) for p in P]
    r = [r_ref[:, sl[p]] for p in P]
    lw = [lw_ref[:, sl[p]] for p in P]
    k = [k_ref[:, sl[p]] for p in P]
    v = [v_ref[:, sl[p]] for p in P]
    kk = [kk_ref[:, sl[p]] for p in P]
    a = [a_ref[:, sl[p]] for p in P]
    kkn = [kk[p] * lax.rsqrt(seg_sum(kk[p] * kk[p]) + 1e-12) for p in P]
    bonus = [seg_sum(r[p] * k[p] * rk_ref[:, sl[p]]) * v[p] for p in P]
    cum = [_cumsum_rows(lw[p], tril) for p in P]
    inv = [jnp.exp(-cum[p]) for p in P]
    a2 = [stack(-kkn[p] * jnp.exp(cum[p] - lw[p])).astype(BF16) for p in P]
    r2 = [stack(r[p] * jnp.exp(cum[p])).astype(BF16) for p in P]
    b2 = [stack(kkn[p] * a[p] * inv[p]).astype(BF16) for p in P]
    k2 = [stack(k[p] * inv[p]).astype(BF16) for p in P]
    v2 = [stack(v[p]).astype(BF16) for p in P]
    ar = [jnp.concatenate([a2[p], r2[p]], axis=0) for p in P]
    bk = [jnp.concatenate([b2[p], k2[p]], axis=0) for p in P]
    gram = [_mm_nt(ar[p], bk[p]) for p in P]
    l_ab = [jnp.where(lower_strict, gram[p][:C2, :C2], 0.0).astype(BF16) for p in P]
    l_ak = [jnp.where(lower_strict, gram[p][:C2, C2:], 0.0).astype(BF16) for p in P]
    m_r = [jnp.concatenate([jnp.where(lower_incl, gram[p][C2:, :C2], 0.0),
                            jnp.where(lower_incl, gram[p][C2:, C2:], 0.0)], axis=1).astype(BF16) for p in P]
    s0 = [state_ref[p] for p in P]
    ars = [_mm_nt(ar[p], s0[p]) for p in P]
    x = [ars[p][:C2] + _mm(l_ak[p], v2[p]) for p in P]
    pw = l_ab
    levels = int(math.log2(C))
    for lvl in range(levels):
        if lvl + 1 < levels:
            prod = [_mm(pw[p], jnp.concatenate([x[p].astype(BF16), pw[p]], axis=1)) for p in P]
            x = [x[p] + prod[p][:, :C2] for p in P]
            pw = [prod[p][:, C2:].astype(BF16) for p in P]
        else:
            x = [x[p] + _mm(pw[p], x[p]) for p in P]
    u2 = [x[p].astype(BF16) for p in P]
    y2 = [ars[p][C2:] + _mm(m_r[p], jnp.concatenate([u2[p], v2[p]], axis=0)) for p in P]
    upd = [_mm_tn(jnp.concatenate([u2[p], v2[p]], axis=0), bk[p]) for p in P]
    for p in P:
        state_ref[p] = (s0[p] + upd[p]) * jnp.exp(cum[p][C - 1:C, :])
    y = [y2[p][:C] + y2[p][C:] for p in P]
    yc = [y[p] - seg_sum(y[p]) * (1.0 / N) for p in P]
    var = [seg_sum(yc[p] * yc[p]) * (1.0 / N) for p in P]
    for p in P:
        yn = yc[p] * lax.rsqrt(var[p] + GN_EPS) * lng_ref[:, sl[p]] + lnb_ref[:, sl[p]]
        o_ref[:, sl[p]] = ((yn + bonus[p]) * g_ref[:, sl[p]]).astype(o_ref.dtype)


def _rw_scan(r, lw, k, v, kk, a, g, r_k, ln_g, ln_b, l, *, npairs):
    T, W = r.shape
    C = RW_CHUNK
    wb = npairs * LANES
    blk = pl.BlockSpec((C, wb), lambda p, c: (c, p))
    vec = pl.BlockSpec((None, 1, wb), lambda p, c: (l, 0, p))
    return pl.pallas_call(
        functools.partial(_rw_scan_kernel, npairs=npairs),
        grid=(W // wb, T // C),
        in_specs=[blk] * 7 + [vec] * 3,
        out_specs=blk,
        out_shape=jax.ShapeDtypeStruct((T, W), BF16),
        scratch_shapes=[pltpu.VMEM((npairs, LANES, LANES), F32)],
        compiler_params=_cparams(("parallel", "arbitrary")),
        name="rw_scan",
    )(r, lw, k, v, kk, a, g, r_k, ln_g, ln_b)


def _out_proj_kernel(x_ref, sb_ref, rw_ref, wa_ref, wb_ref, o_ref):
    o_ref[...] = (x_ref[...]
                  + jnp.dot(sb_ref[...], wa_ref[...], preferred_element_type=F32)
                  + jnp.dot(rw_ref[...], wb_ref[...], preferred_element_type=F32))


def _out_proj(x, sb, rw, w_out, l, *, tm):
    T, D = x.shape
    Wa = sb.shape[1]
    Wb = rw.shape[1]
    assert Wa == Wb
    return pl.pallas_call(
        _out_proj_kernel,
        grid=(T // tm,),
        in_specs=[
            pl.BlockSpec((tm, D), lambda i: (i, 0)),
            pl.BlockSpec((tm, Wa), lambda i: (i, 0)),
            pl.BlockSpec((tm, Wb), lambda i: (i, 0)),
            pl.BlockSpec((None, Wa, D), lambda i: (l, 0, 0)),
            pl.BlockSpec((None, Wb, D), lambda i: (l, 1, 0)),
        ],
        out_specs=pl.BlockSpec((tm, D), lambda i: (i, 0)),
        out_shape=jax.ShapeDtypeStruct((T, D), F32),
        compiler_params=_cparams(("parallel",)),
        name="out_proj",
    )(x, sb, rw, w_out, w_out)


def _router_kernel(x_ref, g_ref, w_ref, b_ref, meta_ref, cnt_ref, carry_ref, *, tm):
    i = pl.program_id(0)

    @pl.when(i == 0)
    def _():
        carry_ref[...] = jnp.zeros_like(carry_ref)

    hn = _rms(x_ref[...]) * g_ref[...]
    logits = _mm_f32(hn, w_ref[...]) + b_ref[...]
    lane = lax.broadcasted_iota(jnp.int32, logits.shape, 1).astype(F32)
    big = float(LANES)
    neg = -jnp.inf
    is_grp = (lane >= N_EXPERTS) & (lane < N_EXPERTS + N_GROUPS)
    gl = jnp.where(is_grp, logits, neg)
    gmax = jnp.max(gl, axis=-1, keepdims=True)
    g_w = 1.0 / jnp.sum(jnp.where(is_grp, jnp.exp(gl - gmax), 0.0), axis=-1, keepdims=True)
    grp = jnp.min(jnp.where(gl == gmax, lane, big), axis=-1, keepdims=True) - N_EXPERTS
    lo = grp * EXPERTS_PER_GROUP
    in_grp = (lane >= lo) & (lane < lo + EXPERTS_PER_GROUP)
    es = jnp.where(in_grp, logits, neg)
    m1 = jnp.max(es, axis=-1, keepdims=True)
    i1 = jnp.min(jnp.where(es == m1, lane, big), axis=-1, keepdims=True)
    es2 = jnp.where(lane == i1, neg, es)
    m2 = jnp.max(es2, axis=-1, keepdims=True)
    i2 = jnp.min(jnp.where(es2 == m2, lane, big), axis=-1, keepdims=True)
    z = jnp.sum(jnp.where(in_grp, jnp.exp(es - m1), 0.0), axis=-1, keepdims=True)
    p1 = 1.0 / z
    p2 = jnp.exp(m2 - m1) / z
    w1 = g_w * (p1 / (p1 + p2))
    w2 = g_w * (p2 / (p1 + p2))
    oh1 = lane == i1
    oh2 = lane == i2
    oh = (oh1 | oh2).astype(BF16)
    row = lax.broadcasted_iota(jnp.int32, (tm, tm), 0)
    col = lax.broadcasted_iota(jnp.int32, (tm, tm), 1)
    before = jnp.dot((col < row).astype(BF16), oh, preferred_element_type=F32) + carry_ref[0:1, :]
    rank1 = jnp.sum(jnp.where(oh1, before, 0.0), axis=-1, keepdims=True)
    rank2 = jnp.sum(jnp.where(oh2, before, 0.0), axis=-1, keepdims=True)
    carry_ref[...] = carry_ref[...] + jnp.sum(oh.astype(F32), axis=0, keepdims=True)
    meta = jnp.where(lane == 0, i1, 0.0)
    meta = jnp.where(lane == 1, i2, meta)
    meta = jnp.where(lane == 2, w1, meta)
    meta = jnp.where(lane == 3, w2, meta)
    meta = jnp.where(lane == 4, rank1, meta)
    meta = jnp.where(lane == 5, rank2, meta)
    meta_ref[...] = meta
    cnt_ref[...] = carry_ref[...]


def _router(x, g, w, b, l, *, tm):
    T, D = x.shape
    return pl.pallas_call(
        functools.partial(_router_kernel, tm=tm),
        grid=(T // tm,),
        in_specs=[
            pl.BlockSpec((tm, D), lambda i: (i, 0)),
            _layer_vec(l, D),
            _layer_mat(l, (D, LANES)),
            _layer_vec(l, LANES),
        ],
        out_specs=[
            pl.BlockSpec((tm, LANES), lambda i: (i, 0)),
            pl.BlockSpec((SUBLANES, LANES), lambda i: (0, 0)),
        ],
        out_shape=[
            jax.ShapeDtypeStruct((T, LANES), F32),
            jax.ShapeDtypeStruct((SUBLANES, LANES), F32),
        ],
        scratch_shapes=[pltpu.VMEM((SUBLANES, LANES), F32)],
        compiler_params=_cparams(("arbitrary",)),
        name="router",
    )(x, g, w, b)


def _gather_rows_start(idx_ref, src_hbm, dst, sem):
    n = dst.shape[0]

    def body(grp, carry):
        base = pl.multiple_of(grp * GATHER_UNROLL, GATHER_UNROLL)
        for u in range(GATHER_UNROLL):
            t = idx_ref[0, 0, base + u]
            pltpu.make_async_copy(src_hbm.at[pl.ds(t, 1), :], dst.at[pl.ds(base + u, 1), :], sem).start()
        return carry

    lax.fori_loop(0, n // GATHER_UNROLL, body, 0)


def _gather_rows_wait(src_hbm, dst, sem):
    pltpu.make_async_copy(src_hbm.at[pl.ds(0, dst.shape[0]), :], dst, sem).wait()


def _scatter_rows_start(idx_ref, src, dst_hbm, sem):
    n = src.shape[0]

    def body(grp, carry):
        base = pl.multiple_of(grp * GATHER_UNROLL, GATHER_UNROLL)
        for u in range(GATHER_UNROLL):
            t = idx_ref[0, 0, base + u]
            pltpu.make_async_copy(src.at[pl.ds(base + u, 1), :], dst_hbm.at[pl.ds(t, 1), :], sem).start()
        return carry

    lax.fori_loop(0, n // GATHER_UNROLL, body, 0)


def _scatter_rows_wait(src, dst_hbm, sem):
    pltpu.make_async_copy(src, dst_hbm.at[pl.ds(0, src.shape[0]), :], sem).wait()


def _dispatch_kernel(d0_ref, d1_ref, x_ref, g_ref, zero_hbm, o_hbm, sbuf, sem):
    del zero_hbm
    i = pl.program_id(0)
    n = pl.num_programs(0)
    slot = i % 2
    sbuf[slot] = _rms(x_ref[...]) * g_ref[...]
    _scatter_rows_start(d0_ref, sbuf.at[slot], o_hbm, sem.at[slot])
    _scatter_rows_start(d1_ref, sbuf.at[slot], o_hbm, sem.at[slot])

    def drain(s):
        _scatter_rows_wait(sbuf.at[s], o_hbm, sem.at[s])
        _scatter_rows_wait(sbuf.at[s], o_hbm, sem.at[s])

    @pl.when(i > 0)
    def _():
        drain(1 - slot)

    @pl.when(i == n - 1)
    def _():
        drain(slot)


def _dispatch(x, g, dest0, dest1, n_rows, l):
    T, D = x.shape
    nt, _, tm = dest0.shape
    cur = pl.BlockSpec((1, 1, tm), lambda i: (i, 0, 0), memory_space=pltpu.SMEM)
    zeros = jnp.zeros((n_rows, D), F32)
    return pl.pallas_call(
        _dispatch_kernel,
        grid=(nt,),
        in_specs=[cur, cur, pl.BlockSpec((tm, D), lambda i: (i, 0)), _layer_vec(l, D),
                  pl.BlockSpec(memory_space=pl.ANY)],
        out_specs=pl.BlockSpec(memory_space=pl.ANY),
        out_shape=jax.ShapeDtypeStruct((n_rows, D), F32),
        scratch_shapes=[pltpu.VMEM((2, tm, D), F32), pltpu.SemaphoreType.DMA((2,))],
        input_output_aliases={4: 0},
        compiler_params=_cparams(("arbitrary",)),
        name="dispatch",
    )(dest0, dest1, x, g, zeros)


def _expert_kernel(be_ref, nused_ref, x_ref, wu_ref, wd_ref, o_ref, wu_bf, wd_bf):
    b = pl.program_id(0)
    n_used = nused_ref[0]
    cast_rows = 256

    @pl.when((b < n_used) & ((b == 0) | (be_ref[b] != be_ref[jnp.maximum(b - 1, 0)])))
    def _():
        def cast(ref, dst, c):
            rows = pl.ds(pl.multiple_of(c * cast_rows, cast_rows), cast_rows)
            dst[rows, :] = ref[rows, :].astype(BF16)

        def up(c, carry):
            cast(wu_ref, wu_bf, c)
            return carry

        def down(c, carry):
            cast(wd_ref, wd_bf, c)
            return carry

        lax.fori_loop(0, wu_bf.shape[0] // cast_rows, up, 0)
        lax.fori_loop(0, wd_bf.shape[0] // cast_rows, down, 0)

    @pl.when(b < n_used)
    def _():
        h = jnp.dot(x_ref[...].astype(BF16), wu_bf[...], preferred_element_type=F32)
        ff = h.shape[1] // 2
        gate = h[:, :ff]
        up = h[:, ff:]
        act = (gate * jax.nn.sigmoid(gate) * up).astype(BF16)
        o_ref[...] = jnp.dot(act, wd_bf[...], preferred_element_type=F32).astype(o_ref.dtype)

    @pl.when(b >= n_used)
    def _():
        o_ref[...] = jnp.zeros_like(o_ref)


def _expert_mlp(x_pad, blk_expert, n_used, w_up, w_down, l):
    R, D = x_pad.shape
    bm = MOE_ROWS
    F2 = w_up.shape[3]
    grid_spec = pltpu.PrefetchScalarGridSpec(
        num_scalar_prefetch=2,
        grid=(R // bm,),
        in_specs=[
            pl.BlockSpec((bm, D), lambda b, be, nu: (b, 0)),
            pl.BlockSpec((None, None, D, F2), lambda b, be, nu: (l, be[b], 0, 0)),
            pl.BlockSpec((None, None, F2 // 2, D), lambda b, be, nu: (l, be[b], 0, 0)),
        ],
        out_specs=pl.BlockSpec((bm, D), lambda b, be, nu: (b, 0)),
        scratch_shapes=[pltpu.VMEM((D, F2), BF16), pltpu.VMEM((F2 // 2, D), BF16)],
    )
    return pl.pallas_call(
        _expert_kernel,
        grid_spec=grid_spec,
        out_shape=jax.ShapeDtypeStruct((R, D), F32),
        compiler_params=_cparams(("arbitrary",)),
        name="expert_mlp",
    )(blk_expert, n_used, x_pad, w_up, w_down)


def _combine_ple_kernel(d0_ref, d1_ref, d0n_ref, d1n_ref, y_hbm, x_ref, meta_ref, p_ref, wg_ref, wp_ref, g_ref,
                        o_ref, ybuf, sem):
    i = pl.program_id(0)
    n = pl.num_programs(0)
    slot = i % 2

    def start(s, a_ref, b_ref):
        _gather_rows_start(a_ref, y_hbm, ybuf.at[s, 0], sem.at[s])
        _gather_rows_start(b_ref, y_hbm, ybuf.at[s, 1], sem.at[s])

    @pl.when(i == 0)
    def _():
        start(0, d0_ref, d1_ref)

    @pl.when(i + 1 < n)
    def _():
        start(1 - slot, d0n_ref, d1n_ref)

    _gather_rows_wait(y_hbm, ybuf.at[slot, 0], sem.at[slot])
    _gather_rows_wait(y_hbm, ybuf.at[slot, 1], sem.at[slot])
    meta = meta_ref[...]
    x = x_ref[...] + meta[:, 2:3] * ybuf[slot, 0] + meta[:, 3:4] * ybuf[slot, 1]
    gate = jax.nn.sigmoid(jnp.dot(_rms(x).astype(BF16), wg_ref[...], preferred_element_type=F32))
    pe = jnp.dot(p_ref[...].astype(BF16), wp_ref[...], preferred_element_type=F32)
    o_ref[...] = x + _rms(pe * gate) * g_ref[...]


def _combine_ple(x, y_pad, dest0, dest1, meta, p, w_gate, w_proj, g, l):
    T, D = x.shape
    P = p.shape[2]
    nt, _, tm = dest0.shape
    row = lambda w: pl.BlockSpec((tm, w), lambda i: (i, 0))
    cur = pl.BlockSpec((1, 1, tm), lambda i: (i, 0, 0), memory_space=pltpu.SMEM)
    nxt = pl.BlockSpec((1, 1, tm), lambda i: (jnp.minimum(i + 1, nt - 1), 0, 0), memory_space=pltpu.SMEM)
    return pl.pallas_call(
        _combine_ple_kernel,
        grid=(nt,),
        in_specs=[cur, cur, nxt, nxt, pl.BlockSpec(memory_space=pl.ANY),
                  row(D), row(LANES),
                  pl.BlockSpec((None, tm, P), lambda i: (l, i, 0)),
                  _layer_mat(l, (D, D)), _layer_mat(l, (P, D)), _layer_vec(l, D)],
        out_specs=row(D),
        out_shape=jax.ShapeDtypeStruct((T, D), F32),
        scratch_shapes=[pltpu.VMEM((2, 2, tm, D), F32), pltpu.SemaphoreType.DMA((2,))],
        compiler_params=_cparams(("arbitrary",)),
        name="combine_ple",
    )(dest0, dest1, dest0, dest1, y_pad, x, meta, p, w_gate, w_proj, g)


def _moe_layout(meta, counts, T, tm):
    bm = MOE_ROWS
    n_rows = 2 * T + N_EXPERTS * bm
    nb = n_rows // bm
    e = meta[:, 0:2].astype(jnp.int32)
    rank = meta[:, 4:6].astype(jnp.int32)
    cnt = counts[0, :N_EXPERTS].astype(jnp.int32)
    padded = ((cnt + bm - 1) // bm) * bm
    pad_ends = jnp.cumsum(padded)
    pad_starts = pad_ends - padded
    start_of = jnp.sum(jnp.where(e[..., None] == jnp.arange(N_EXPERTS, dtype=jnp.int32), pad_starts, 0), axis=-1)
    dest = start_of + rank
    blk_start = jnp.arange(nb, dtype=jnp.int32) * bm
    blk_expert = jnp.minimum(jnp.sum(pad_ends[None, :] <= blk_start[:, None], axis=1), N_EXPERTS - 1)
    n_used = (pad_ends[-1:] // bm).astype(jnp.int32)
    dest0 = dest[:, 0].reshape(T // tm, 1, tm)
    dest1 = dest[:, 1].reshape(T // tm, 1, tm)
    return dest0, dest1, n_rows, blk_expert.astype(jnp.int32), n_used


def kernel(x, p, ln1_g, w_in, sb_q_g, sb_k_g, sb_out_g, rw_mu, rw_w0, rw_w2, rw_a0, rw_a2, rw_g2, rw_k_k, rw_k_a, rw_r_k, rw_ln_g, rw_ln_b, w_out, ln2_g, router_g, router_g_b, router_e, router_e_b, w_up, w_down, ple_proj, ple_gate, ple_norm_g):
    B, S, D = x.shape
    depth = w_in.shape[0]
    T = B * S
    assert B == 1, "token shift and attention assume one sequence"
    sb_width = D // 2
    rw_width = D - sb_width
    n_sb_heads = sb_width // SB_HEAD_DIM
    lora = LORA_W + LORA_A + LORA_G
    col0 = 3 * sb_width
    W = rw_width
    tm = min(TOKEN_TILE, T)
    vec = lambda a: a.reshape(depth, 1, -1)

    n_main = col0 + 3 * W
    w_lora_b = jnp.pad(w_in[:, :, n_main:], ((0, 0), (0, 0), (0, LORA_PAD - lora))).astype(BF16)
    w_out_b = w_out.astype(BF16)
    ple_gate_b = ple_gate.astype(BF16)
    ple_proj_b = ple_proj.astype(BF16)
    mu = vec(rw_mu)
    mu_r, mu_k, mu_v = mu[:, :, :W], mu[:, :, W:2 * W], mu[:, :, 2 * W:3 * W]
    mu_lo = jnp.pad(mu[:, :, 3 * W:], ((0, 0), (0, 0), (0, LORA_PAD - lora)))
    w2 = jnp.pad(rw_w2, ((0, 0), (0, LORA_A), (0, 0)))
    a2 = jnp.pad(rw_a2, ((0, 0), (LORA_W, 0), (0, 0)))
    g2 = jnp.pad(rw_g2, ((0, 0), (0, 2 * LANES - LORA_G), (0, 0)))
    w_r = jnp.pad(jnp.concatenate([router_e, router_g], axis=2),
                  ((0, 0), (0, 0), (0, LANES - N_EXPERTS - N_GROUPS)))
    b_r = vec(jnp.pad(jnp.concatenate([router_e_b, router_g_b], axis=1),
                      ((0, 0), (0, LANES - N_EXPERTS - N_GROUPS))))
    p3 = p.reshape(depth, T, -1)

    xf = x.reshape(T, D)
    for l in range(depth):
        proj = _norm_matmul(xf, vec(ln1_g), w_in, w_lora_b, l, n_cols=n_main, tm=min(1024, T), tn=LORA_PAD,
                            out_dtype=F32, name="in_proj")
        sb = _sb_attention(proj, vec(sb_q_g), vec(sb_k_g), vec(sb_out_g), l, n_heads=n_sb_heads, tq=256,
                           nsub=min(4, T // 256))
        r, lw, k, v, kk, a, g = _rw_prep(
            proj, mu_r, mu_k, mu_v, mu_lo, vec(rw_w0), w2, vec(rw_a0), a2, g2, vec(rw_k_k), vec(rw_k_a), l,
            rw_width=rw_width, col0=col0, tm=tm)
        rw = _rw_scan(r, lw, k, v, kk, a, g, vec(rw_r_k), vec(rw_ln_g), vec(rw_ln_b), l, npairs=8)
        x1 = _out_proj(xf, sb, rw, w_out_b, l, tm=tm)
        meta, counts = _router(x1, vec(ln2_g), w_r, b_r, l, tm=tm)
        dest0, dest1, n_rows, blk_expert, n_used = _moe_layout(meta, counts, T, tm)
        x_pad = _dispatch(x1, vec(ln2_g), dest0, dest1, n_rows, l)
        y_pad = _expert_mlp(x_pad, blk_expert, n_used, w_up, w_down, l)
        xf = _combine_ple(x1, y_pad, dest0, dest1, meta, p3, ple_gate_b, ple_proj_b, vec(ple_norm_g), l)
    return xf.reshape(B, S, D)
```

```python
import functools
import math

import jax
import jax.numpy as jnp
from jax import lax
from jax.experimental import pallas as pl
from jax.experimental.pallas import tpu as pltpu

F32 = jnp.float32
BF16 = jnp.bfloat16

SB_HEAD_DIM = 128
RW_HEAD_DIM = 64
LORA_W = 64
LORA_A = 64
LORA_G = 160
LORA_PAD = 512
N_GROUPS = 4
EXPERTS_PER_GROUP = 8
N_EXPERTS = N_GROUPS * EXPERTS_PER_GROUP
RMS_EPS = 1e-6
GN_EPS = 64e-5
LANES = 128
SUBLANES = 8
VMEM_LIMIT = 56 * 1024 * 1024

EXP_UNDERFLOW = -104.0
RW_CHUNK = 64
MOE_ROWS = 256
TOKEN_TILE = 256
WIDE_TOKEN_TILE = 512
GATHER_UNROLL = 8


def _cparams(sem):
    return pltpu.CompilerParams(dimension_semantics=sem, vmem_limit_bytes=VMEM_LIMIT)


def _rms(x, eps=RMS_EPS):
    return x * lax.rsqrt(jnp.mean(x * x, axis=-1, keepdims=True) + eps)


def _softplus(y):
    return jnp.maximum(y, 0.0) + jnp.log(1.0 + jnp.exp(-jnp.abs(y)))


def _mm(a, b):
    return jnp.dot(a.astype(BF16), b.astype(BF16), preferred_element_type=F32)


def _mm_nt(a, b):
    return lax.dot_general(a.astype(BF16), b.astype(BF16), (((1,), (1,)), ((), ())),
                           preferred_element_type=F32)


def _mm_tn(a, b):
    return lax.dot_general(a.astype(BF16), b.astype(BF16), (((0,), (0,)), ((), ())),
                           preferred_element_type=F32)


def _mm_f32(a, b):
    return jnp.dot(a, b, preferred_element_type=F32, precision=lax.Precision.HIGHEST)


def _mm_split(a, b_exact):
    hi = a.astype(BF16)
    lo = (a - hi.astype(F32)).astype(BF16)
    return (jnp.dot(hi, b_exact, preferred_element_type=F32)
            + jnp.dot(lo, b_exact, preferred_element_type=F32))


def _layer_vec(l, width):
    return pl.BlockSpec((None, 1, width), lambda *_: (l, 0, 0))


def _layer_mat(l, shape):
    return pl.BlockSpec((None,) + tuple(shape), lambda *_: (l,) + (0,) * len(shape))


def _norm_matmul_kernel(x_ref, g_ref, w_ref, o_ref, xn_ref):
    @pl.when(pl.program_id(1) == 0)
    def _():
        xn_ref[...] = (_rms(x_ref[...]) * g_ref[...]).astype(BF16)

    o_ref[...] = jnp.dot(xn_ref[...], w_ref[...], preferred_element_type=F32).astype(o_ref.dtype)


def _norm_matmul(x, g, w, l, *, tm, tn, out_dtype, name):
    T, D = x.shape
    N = w.shape[2]
    return pl.pallas_call(
        _norm_matmul_kernel,
        grid=(T // tm, N // tn),
        in_specs=[
            pl.BlockSpec((tm, D), lambda i, j: (i, 0)),
            _layer_vec(l, D),
            pl.BlockSpec((None, D, tn), lambda i, j: (l, 0, j)),
        ],
        out_specs=pl.BlockSpec((tm, tn), lambda i, j: (i, j)),
        out_shape=jax.ShapeDtypeStruct((T, N), out_dtype),
        scratch_shapes=[pltpu.VMEM((tm, D), BF16)],
        compiler_params=_cparams(("parallel", "arbitrary")),
        name=name,
    )(x, g, w)


def _sb_attn_kernel(q_ref, k_ref, v_ref, qg_ref, kg_ref, og_ref, o_ref, kn_ref, vb_ref, *, tq, seq):
    i = pl.program_id(1)
    hd = SB_HEAD_DIM
    prep_rows = min(512, seq)

    @pl.when(i == 0)
    def _():
        def body(c, carry):
            rows = pl.ds(pl.multiple_of(c * prep_rows, prep_rows), prep_rows)
            kn_ref[rows, :] = (_rms(k_ref[rows, :]) * kg_ref[...]).astype(BF16)
            vb_ref[rows, :] = v_ref[rows, :].astype(BF16)
            return carry

        lax.fori_loop(0, seq // prep_rows, body, 0)

    nsub = q_ref.shape[0] // tq
    qn_all = (_rms(q_ref[...]) * qg_ref[...] * (1.0 / math.sqrt(hd))).astype(BF16)
    qn = [qn_all[s * tq:(s + 1) * tq] for s in range(nsub)]
    qi = [nsub * i + s for s in range(nsub)]
    row = lax.broadcasted_iota(jnp.int32, (tq, tq), 0)
    col = lax.broadcasted_iota(jnp.int32, (tq, tq), 1)
    tri = (row > col).astype(BF16)
    causal = col < row

    def scores(s, j, masked, valid=None):
        rows = pl.ds(pl.multiple_of(j * tq, tq), tq)
        z = _mm_nt(qn[s], kn_ref[rows, :])
        sp = _softplus(z)
        log_keep = -sp
        if masked:
            log_keep = jnp.where(causal, log_keep, 0.0)
        if valid is not None:
            log_keep = jnp.where(valid, log_keep, 0.0)
        later = _mm(log_keep, tri)
        return rows, z - sp, log_keep, later, masked, valid

    def accumulate(tile, carry, acc):
        rows, log_beta, log_keep, later, masked, valid = tile
        w = jnp.exp(log_beta + later + carry)
        if masked:
            w = jnp.where(causal, w, 0.0)
        if valid is not None:
            w = jnp.where(valid, w, 0.0)
        acc = acc + jnp.dot(w.astype(BF16), vb_ref[rows, :], preferred_element_type=F32)
        return carry + later[:, 0:1] + log_keep[:, 0:1], acc

    def live(carry):
        return (jnp.max(carry) > EXP_UNDERFLOW).astype(jnp.int32)

    diag = [scores(s, qi[s], True) for s in range(nsub)]
    prev = [scores(s, jnp.maximum(qi[s] - 1, 0), False, None if s > 0 else i > 0) for s in range(nsub)]
    state = []
    for s in range(nsub):
        carry, acc = accumulate(diag[s], jnp.zeros((tq, 1), F32), jnp.zeros((tq, hd), F32))
        state.append(accumulate(prev[s], carry, acc))
    for s in range(nsub):
        carry, acc = state[s]

        def cond(st, s=s):
            return (st[0] < qi[s]) & (st[1] > 0)

        def body(st, s=s):
            n, _, carry, acc = st
            carry, acc = accumulate(scores(s, qi[s] - 1 - n, False), carry, acc)
            return n + 1, live(carry), carry, acc

        _, _, carry, acc = lax.while_loop(cond, body, (jnp.int32(1), live(carry), carry, acc))
        o_ref[s * tq:(s + 1) * tq, :] = (_rms(acc) * og_ref[...]).astype(o_ref.dtype)


def _sb_attention(proj, q_g, k_g, out_g, l, *, n_heads, tq, nsub):
    T = proj.shape[0]
    hd = SB_HEAD_DIM
    tb = nsub * tq
    return pl.pallas_call(
        functools.partial(_sb_attn_kernel, tq=tq, seq=T),
        grid=(n_heads, T // tb),
        in_specs=[
            pl.BlockSpec((tb, hd), lambda h, i: (i, h)),
            pl.BlockSpec((T, hd), lambda h, i: (0, n_heads + h)),
            pl.BlockSpec((T, hd), lambda h, i: (0, 2 * n_heads + h)),
            _layer_vec(l, hd), _layer_vec(l, hd), _layer_vec(l, hd),
        ],
        out_specs=pl.BlockSpec((tb, hd), lambda h, i: (i, h)),
        out_shape=jax.ShapeDtypeStruct((T, n_heads * hd), BF16),
        scratch_shapes=[pltpu.VMEM((T, hd), BF16), pltpu.VMEM((T, hd), BF16)],
        compiler_params=_cparams(("parallel", "arbitrary")),
        name="sb_attention",
    )(proj, proj, proj, q_g, k_g, out_g)


def _rw_prep_kernel(r_ref, k_ref, v_ref, lo_ref, rp_ref, kp_ref, vp_ref, lop_ref,
                    mu_r_ref, mu_k_ref, mu_v_ref, mu_lo_ref, w0_ref, w2_ref, a0_ref, a2_ref, g2_ref,
                    kk_ref, ka_ref,
                    r_out, lw_out, k_out, v_out, kk_out, a_out, g_out):
    first = pl.program_id(0) == 0

    def shift(cur_ref, prev_ref, mu_ref):
        cur = cur_ref[...]
        prev_row = jnp.where(first, 0.0, prev_ref[SUBLANES - 1:SUBLANES, :])
        rolled = pltpu.roll(cur, 1, 0)
        rowi = lax.broadcasted_iota(jnp.int32, cur.shape, 0)
        prev = jnp.where(rowi == 0, prev_row, rolled)
        return cur + mu_ref[...] * (prev - cur)

    r = shift(r_ref, rp_ref, mu_r_ref)
    k = shift(k_ref, kp_ref, mu_k_ref)
    v = shift(v_ref, vp_ref, mu_v_ref)
    lo = shift(lo_ref, lop_ref, mu_lo_ref)
    wa_lo = lo[:, :LORA_W + LORA_A]
    g_lo = lo[:, LORA_W + LORA_A:LORA_W + LORA_A + g2_ref.shape[0]]
    w = -_softplus(-(w0_ref[...] + _mm_f32(jnp.tanh(wa_lo), w2_ref[...]))) - 0.5
    a = jax.nn.sigmoid(a0_ref[...] + _mm_f32(wa_lo, a2_ref[...]))
    g = _mm_f32(jax.nn.sigmoid(g_lo), g2_ref[...])
    r_out[...] = r
    lw_out[...] = -jnp.exp(w)
    k_out[...] = k * (1.0 + (a - 1.0) * ka_ref[...])
    v_out[...] = v
    kk_out[...] = k * kk_ref[...]
    a_out[...] = a
    g_out[...] = g


def _rw_prep(proj, mu_r, mu_k, mu_v, mu_lo, w0, w2, a0, a2, g2, k_k, k_a, l, *, rw_width, col0, tm):
    T = proj.shape[0]
    W = rw_width
    cb = col0 // W
    lb = (col0 + 3 * W) // LORA_PAD

    def cur(width, blk):
        return pl.BlockSpec((tm, width), lambda i: (i, blk))

    def prev(width, blk):
        return pl.BlockSpec((SUBLANES, width), lambda i: (jnp.maximum(i * (tm // SUBLANES) - 1, 0), blk))

    out = jax.ShapeDtypeStruct((T, W), F32)
    return pl.pallas_call(
        _rw_prep_kernel,
        grid=(T // tm,),
        in_specs=[cur(W, cb), cur(W, cb + 1), cur(W, cb + 2), cur(LORA_PAD, lb),
                  prev(W, cb), prev(W, cb + 1), prev(W, cb + 2), prev(LORA_PAD, lb),
                  _layer_vec(l, W), _layer_vec(l, W), _layer_vec(l, W), _layer_vec(l, LORA_PAD),
                  _layer_vec(l, W), _layer_mat(l, w2.shape[1:]), _layer_vec(l, W), _layer_mat(l, a2.shape[1:]),
                  _layer_mat(l, g2.shape[1:]), _layer_vec(l, W), _layer_vec(l, W)],
        out_specs=[pl.BlockSpec((tm, W), lambda i: (i, 0))] * 7,
        out_shape=[out] * 7,
        compiler_params=_cparams(("parallel",)),
        name="rw_prep",
    )(proj, proj, proj, proj, proj, proj, proj, proj,
      mu_r, mu_k, mu_v, mu_lo, w0, w2, a0, a2, g2, k_k, k_a)


def _cumsum_rows(x, tril_bf16):
    hi = x.astype(BF16)
    lo = (x - hi.astype(F32)).astype(BF16)
    return (jnp.dot(tril_bf16, hi, preferred_element_type=F32)
            + jnp.dot(tril_bf16, lo, preferred_element_type=F32))


def _rw_scan_kernel(r_ref, lw_ref, k_ref, v_ref, kk_ref, a_ref, g_ref, rk_ref, lng_ref, lnb_ref,
                    o_ref, state_ref, *, npairs):
    C = RW_CHUNK
    N = RW_HEAD_DIM
    C2 = 2 * C
    assert LANES == 2 * N and C2 == LANES

    @pl.when(pl.program_id(1) == 0)
    def _():
        state_ref[...] = jnp.zeros_like(state_ref)

    row = lax.broadcasted_iota(jnp.int32, (C2, C2), 0)
    col = lax.broadcasted_iota(jnp.int32, (C2, C2), 1)
    lower_incl = col <= row
    lower_strict = col < row
    seg_ones = ((row < N) == (col < N)).astype(BF16)
    tril = (lax.broadcasted_iota(jnp.int32, (C, C), 1) <= lax.broadcasted_iota(jnp.int32, (C, C), 0)).astype(BF16)
    head0 = lax.broadcasted_iota(jnp.int32, (C, LANES), 1) < N

    def stack(x):
        return jnp.concatenate([jnp.where(head0, x, 0.0), jnp.where(head0, 0.0, x)], axis=0)

    def seg_sum(x):
        return _mm(x, seg_ones)

    P = range(npairs)
    sl = [slice(p * LANES, (p + 1) * LANES) for p in P]
    r = [r_ref[:, sl[p]] for p in P]
    lw = [lw_ref[:, sl[p]] for p in P]
    k = [k_ref[:, sl[p]] for p in P]
    v = [v_ref[:, sl[p]] for p in P]
    kk = [kk_ref[:, sl[p]] for p in P]
    a = [a_ref[:, sl[p]] for p in P]
    kkn = [kk[p] * lax.rsqrt(seg_sum(kk[p] * kk[p]) + 1e-12) for p in P]
    bonus = [seg_sum(r[p] * k[p] * rk_ref[:, sl[p]]) * v[p] for p in P]
    cum = [_cumsum_rows(lw[p], tril) for p in P]
    inv = [jnp.exp(-cum[p]) for p in P]
    a2 = [stack(-kkn[p] * jnp.exp(cum[p] - lw[p])).astype(BF16) for p in P]
    r2 = [stack(r[p] * jnp.exp(cum[p])).astype(BF16) for p in P]
    b2 = [stack(kkn[p] * a[p] * inv[p]).astype(BF16) for p in P]
    k2 = [stack(k[p] * inv[p]).astype(BF16) for p in P]
    v2 = [stack(v[p]).astype(BF16) for p in P]
    ar = [jnp.concatenate([a2[p], r2[p]], axis=0) for p in P]
    bk = [jnp.concatenate([b2[p], k2[p]], axis=0) for p in P]
    gram = [_mm_nt(ar[p], bk[p]) for p in P]
    l_ab = [jnp.where(lower_strict, gram[p][:C2, :C2], 0.0).astype(BF16) for p in P]
    l_ak = [jnp.where(lower_strict, gram[p][:C2, C2:], 0.0).astype(BF16) for p in P]
    m_r = [jnp.concatenate([jnp.where(lower_incl, gram[p][C2:, :C2], 0.0),
                            jnp.where(lower_incl, gram[p][C2:, C2:], 0.0)], axis=1).astype(BF16) for p in P]
    s0 = [state_ref[p] for p in P]
    ars = [_mm_nt(ar[p], s0[p]) for p in P]
    x = [ars[p][:C2] + _mm(l_ak[p], v2[p]) for p in P]
    pw = l_ab
    levels = int(math.log2(C))
    for lvl in range(levels):
        if lvl + 1 < levels:
            prod = [_mm(pw[p], jnp.concatenate([x[p].astype(BF16), pw[p]], axis=1)) for p in P]
            x = [x[p] + prod[p][:, :C2] for p in P]
            pw = [prod[p][:, C2:].astype(BF16) for p in P]
        else:
            x = [x[p] + _mm(pw[p], x[p]) for p in P]
    u2 = [x[p].astype(BF16) for p in P]
    y2 = [ars[p][C2:] + _mm(m_r[p], jnp.concatenate([u2[p], v2[p]], axis=0)) for p in P]
    upd = [_mm_tn(jnp.concatenate([u2[p], v2[p]], axis=0), bk[p]) for p in P]
    for p in P:
        state_ref[p] = (s0[p] + upd[p]) * jnp.exp(cum[p][C - 1:C, :])
    y = [y2[p][:C] + y2[p][C:] for p in P]
    yc = [y[p] - seg_sum(y[p]) * (1.0 / N) for p in P]
    var = [seg_sum(yc[p] * yc[p]) * (1.0 / N) for p in P]
    for p in P:
        yn = yc[p] * lax.rsqrt(var[p] + GN_EPS) * lng_ref[:, sl[p]] + lnb_ref[:, sl[p]]
        o_ref[:, sl[p]] = ((yn + bonus[p]) * g_ref[:, sl[p]]).astype(o_ref.dtype)


def _rw_scan(r, lw, k, v, kk, a, g, r_k, ln_g, ln_b, l, *, npairs):
    T, W = r.shape
    C = RW_CHUNK
    wb = npairs * LANES
    blk = pl.BlockSpec((C, wb), lambda p, c: (c, p))
    vec = pl.BlockSpec((None, 1, wb), lambda p, c: (l, 0, p))
    return pl.pallas_call(
        functools.partial(_rw_scan_kernel, npairs=npairs),
        grid=(W // wb, T // C),
        in_specs=[blk] * 7 + [vec] * 3,
        out_specs=blk,
        out_shape=jax.ShapeDtypeStruct((T, W), BF16),
        scratch_shapes=[pltpu.VMEM((npairs, LANES, LANES), F32)],
        compiler_params=_cparams(("parallel", "arbitrary")),
        name="rw_scan",
    )(r, lw, k, v, kk, a, g, r_k, ln_g, ln_b)


def _out_proj_kernel(x_ref, sb_ref, rw_ref, wa_ref, wb_ref, o_ref):
    o_ref[...] = (x_ref[...]
                  + jnp.dot(sb_ref[...], wa_ref[...], preferred_element_type=F32)
                  + jnp.dot(rw_ref[...], wb_ref[...], preferred_element_type=F32))


def _out_proj(x, sb, rw, w_out, l, *, tm):
    T, D = x.shape
    Wa = sb.shape[1]
    Wb = rw.shape[1]
    assert Wa == Wb
    return pl.pallas_call(
        _out_proj_kernel,
        grid=(T // tm,),
        in_specs=[
            pl.BlockSpec((tm, D), lambda i: (i, 0)),
            pl.BlockSpec((tm, Wa), lambda i: (i, 0)),
            pl.BlockSpec((tm, Wb), lambda i: (i, 0)),
            pl.BlockSpec((None, Wa, D), lambda i: (l, 0, 0)),
            pl.BlockSpec((None, Wb, D), lambda i: (l, 1, 0)),
        ],
        out_specs=pl.BlockSpec((tm, D), lambda i: (i, 0)),
        out_shape=jax.ShapeDtypeStruct((T, D), F32),
        compiler_params=_cparams(("parallel",)),
        name="out_proj",
    )(x, sb, rw, w_out, w_out)


def _router_kernel(x_ref, g_ref, w_ref, b_ref, meta_ref, cnt_ref, carry_ref, *, tm):
    i = pl.program_id(0)

    @pl.when(i == 0)
    def _():
        carry_ref[...] = jnp.zeros_like(carry_ref)

    hn = _rms(x_ref[...]) * g_ref[...]
    logits = _mm_f32(hn, w_ref[...]) + b_ref[...]
    lane = lax.broadcasted_iota(jnp.int32, logits.shape, 1).astype(F32)
    big = float(LANES)
    neg = -jnp.inf
    is_grp = (lane >= N_EXPERTS) & (lane < N_EXPERTS + N_GROUPS)
    gl = jnp.where(is_grp, logits, neg)
    gmax = jnp.max(gl, axis=-1, keepdims=True)
    g_w = 1.0 / jnp.sum(jnp.where(is_grp, jnp.exp(gl - gmax), 0.0), axis=-1, keepdims=True)
    grp = jnp.min(jnp.where(gl == gmax, lane, big), axis=-1, keepdims=True) - N_EXPERTS
    lo = grp * EXPERTS_PER_GROUP
    in_grp = (lane >= lo) & (lane < lo + EXPERTS_PER_GROUP)
    es = jnp.where(in_grp, logits, neg)
    m1 = jnp.max(es, axis=-1, keepdims=True)
    i1 = jnp.min(jnp.where(es == m1, lane, big), axis=-1, keepdims=True)
    es2 = jnp.where(lane == i1, neg, es)
    m2 = jnp.max(es2, axis=-1, keepdims=True)
    i2 = jnp.min(jnp.where(es2 == m2, lane, big), axis=-1, keepdims=True)
    z = jnp.sum(jnp.where(in_grp, jnp.exp(es - m1), 0.0), axis=-1, keepdims=True)
    p1 = 1.0 / z
    p2 = jnp.exp(m2 - m1) / z
    w1 = g_w * (p1 / (p1 + p2))
    w2 = g_w * (p2 / (p1 + p2))
    oh1 = lane == i1
    oh2 = lane == i2
    oh = (oh1 | oh2).astype(BF16)
    row = lax.broadcasted_iota(jnp.int32, (tm, tm), 0)
    col = lax.broadcasted_iota(jnp.int32, (tm, tm), 1)
    before = jnp.dot((col < row).astype(BF16), oh, preferred_element_type=F32) + carry_ref[0:1, :]
    rank1 = jnp.sum(jnp.where(oh1, before, 0.0), axis=-1, keepdims=True)
    rank2 = jnp.sum(jnp.where(oh2, before, 0.0), axis=-1, keepdims=True)
    carry_ref[...] = carry_ref[...] + jnp.sum(oh.astype(F32), axis=0, keepdims=True)
    meta = jnp.where(lane == 0, i1, 0.0)
    meta = jnp.where(lane == 1, i2, meta)
    meta = jnp.where(lane == 2, w1, meta)
    meta = jnp.where(lane == 3, w2, meta)
    meta = jnp.where(lane == 4, rank1, meta)
    meta = jnp.where(lane == 5, rank2, meta)
    meta_ref[...] = meta
    cnt_ref[...] = carry_ref[...]


def _router(x, g, w, b, l, *, tm):
    T, D = x.shape
    return pl.pallas_call(
        functools.partial(_router_kernel, tm=tm),
        grid=(T // tm,),
        in_specs=[
            pl.BlockSpec((tm, D), lambda i: (i, 0)),
            _layer_vec(l, D),
            _layer_mat(l, (D, LANES)),
            _layer_vec(l, LANES),
        ],
        out_specs=[
            pl.BlockSpec((tm, LANES), lambda i: (i, 0)),
            pl.BlockSpec((SUBLANES, LANES), lambda i: (0, 0)),
        ],
        out_shape=[
            jax.ShapeDtypeStruct((T, LANES), F32),
            jax.ShapeDtypeStruct((SUBLANES, LANES), F32),
        ],
        scratch_shapes=[pltpu.VMEM((SUBLANES, LANES), F32)],
        compiler_params=_cparams(("arbitrary",)),
        name="router",
    )(x, g, w, b)


def _gather_rows_start(idx_ref, src_hbm, dst, sem):
    n = dst.shape[0]

    def body(grp, carry):
        base = pl.multiple_of(grp * GATHER_UNROLL, GATHER_UNROLL)
        for u in range(GATHER_UNROLL):
            t = idx_ref[0, 0, base + u]
            pltpu.make_async_copy(src_hbm.at[pl.ds(t, 1), :], dst.at[pl.ds(base + u, 1), :], sem).start()
        return carry

    lax.fori_loop(0, n // GATHER_UNROLL, body, 0)


def _gather_rows_wait(src_hbm, dst, sem):
    pltpu.make_async_copy(src_hbm.at[pl.ds(0, dst.shape[0]), :], dst, sem).wait()


def _scatter_rows_start(idx_ref, src, dst_hbm, sem):
    n = src.shape[0]

    def body(grp, carry):
        base = pl.multiple_of(grp * GATHER_UNROLL, GATHER_UNROLL)
        for u in range(GATHER_UNROLL):
            t = idx_ref[0, 0, base + u]
            pltpu.make_async_copy(src.at[pl.ds(base + u, 1), :], dst_hbm.at[pl.ds(t, 1), :], sem).start()
        return carry

    lax.fori_loop(0, n // GATHER_UNROLL, body, 0)


def _scatter_rows_wait(src, dst_hbm, sem):
    pltpu.make_async_copy(src, dst_hbm.at[pl.ds(0, src.shape[0]), :], sem).wait()


def _dispatch_kernel(d0_ref, d1_ref, x_ref, g_ref, zero_hbm, o_hbm, sbuf, sem):
    del zero_hbm
    i = pl.program_id(0)
    n = pl.num_programs(0)
    slot = i % 2
    sbuf[slot] = _rms(x_ref[...]) * g_ref[...]
    _scatter_rows_start(d0_ref, sbuf.at[slot], o_hbm, sem.at[slot])
    _scatter_rows_start(d1_ref, sbuf.at[slot], o_hbm, sem.at[slot])

    def drain(s):
        _scatter_rows_wait(sbuf.at[s], o_hbm, sem.at[s])
        _scatter_rows_wait(sbuf.at[s], o_hbm, sem.at[s])

    @pl.when(i > 0)
    def _():
        drain(1 - slot)

    @pl.when(i == n - 1)
    def _():
        drain(slot)


def _dispatch(x, g, dest0, dest1, n_rows, l):
    T, D = x.shape
    nt, _, tm = dest0.shape
    cur = pl.BlockSpec((1, 1, tm), lambda i: (i, 0, 0), memory_space=pltpu.SMEM)
    zeros = jnp.zeros((n_rows, D), F32)
    return pl.pallas_call(
        _dispatch_kernel,
        grid=(nt,),
        in_specs=[cur, cur, pl.BlockSpec((tm, D), lambda i: (i, 0)), _layer_vec(l, D),
                  pl.BlockSpec(memory_space=pl.ANY)],
        out_specs=pl.BlockSpec(memory_space=pl.ANY),
        out_shape=jax.ShapeDtypeStruct((n_rows, D), F32),
        scratch_shapes=[pltpu.VMEM((2, tm, D), F32), pltpu.SemaphoreType.DMA((2,))],
        input_output_aliases={4: 0},
        compiler_params=_cparams(("arbitrary",)),
        name="dispatch",
    )(dest0, dest1, x, g, zeros)


def _expert_kernel(be_ref, nused_ref, x_ref, wu_ref, wd_ref, o_ref, wu_bf, wd_bf):
    b = pl.program_id(0)
    n_used = nused_ref[0]
    cast_rows = 256

    @pl.when((b < n_used) & ((b == 0) | (be_ref[b] != be_ref[jnp.maximum(b - 1, 0)])))
    def _():
        def cast(ref, dst, c):
            rows = pl.ds(pl.multiple_of(c * cast_rows, cast_rows), cast_rows)
            dst[rows, :] = ref[rows, :].astype(BF16)

        def up(c, carry):
            cast(wu_ref, wu_bf, c)
            return carry

        def down(c, carry):
            cast(wd_ref, wd_bf, c)
            return carry

        lax.fori_loop(0, wu_bf.shape[0] // cast_rows, up, 0)
        lax.fori_loop(0, wd_bf.shape[0] // cast_rows, down, 0)

    @pl.when(b < n_used)
    def _():
        h = jnp.dot(x_ref[...].astype(BF16), wu_bf[...], preferred_element_type=F32)
        ff = h.shape[1] // 2
        gate = h[:, :ff]
        up = h[:, ff:]
        act = (gate * jax.nn.sigmoid(gate) * up).astype(BF16)
        o_ref[...] = jnp.dot(act, wd_bf[...], preferred_element_type=F32).astype(o_ref.dtype)

    @pl.when(b >= n_used)
    def _():
        o_ref[...] = jnp.zeros_like(o_ref)


def _expert_mlp(x_pad, blk_expert, n_used, w_up, w_down, l):
    R, D = x_pad.shape
    bm = MOE_ROWS
    F2 = w_up.shape[3]
    grid_spec = pltpu.PrefetchScalarGridSpec(
        num_scalar_prefetch=2,
        grid=(R // bm,),
        in_specs=[
            pl.BlockSpec((bm, D), lambda b, be, nu: (b, 0)),
            pl.BlockSpec((None, None, D, F2), lambda b, be, nu: (l, be[b], 0, 0)),
            pl.BlockSpec((None, None, F2 // 2, D), lambda b, be, nu: (l, be[b], 0, 0)),
        ],
        out_specs=pl.BlockSpec((bm, D), lambda b, be, nu: (b, 0)),
        scratch_shapes=[pltpu.VMEM((D, F2), BF16), pltpu.VMEM((F2 // 2, D), BF16)],
    )
    return pl.pallas_call(
        _expert_kernel,
        grid_spec=grid_spec,
        out_shape=jax.ShapeDtypeStruct((R, D), F32),
        compiler_params=_cparams(("arbitrary",)),
        name="expert_mlp",
    )(blk_expert, n_used, x_pad, w_up, w_down)


def _combine_ple_kernel(d0_ref, d1_ref, d0n_ref, d1n_ref, y_hbm, x_ref, meta_ref, p_ref, wg_ref, wp_ref, g_ref,
                        o_ref, ybuf, sem):
    i = pl.program_id(0)
    n = pl.num_programs(0)
    slot = i % 2

    def start(s, a_ref, b_ref):
        _gather_rows_start(a_ref, y_hbm, ybuf.at[s, 0], sem.at[s])
        _gather_rows_start(b_ref, y_hbm, ybuf.at[s, 1], sem.at[s])

    @pl.when(i == 0)
    def _():
        start(0, d0_ref, d1_ref)

    @pl.when(i + 1 < n)
    def _():
        start(1 - slot, d0n_ref, d1n_ref)

    _gather_rows_wait(y_hbm, ybuf.at[slot, 0], sem.at[slot])
    _gather_rows_wait(y_hbm, ybuf.at[slot, 1], sem.at[slot])
    meta = meta_ref[...]
    x = x_ref[...] + meta[:, 2:3] * ybuf[slot, 0] + meta[:, 3:4] * ybuf[slot, 1]
    gate = jax.nn.sigmoid(jnp.dot(_rms(x).astype(BF16), wg_ref[...], preferred_element_type=F32))
    pe = jnp.dot(p_ref[...].astype(BF16), wp_ref[...], preferred_element_type=F32)
    o_ref[...] = x + _rms(pe * gate) * g_ref[...]


def _combine_ple(x, y_pad, dest0, dest1, meta, p, w_gate, w_proj, g, l):
    T, D = x.shape
    P = p.shape[2]
    nt, _, tm = dest0.shape
    row = lambda w: pl.BlockSpec((tm, w), lambda i: (i, 0))
    cur = pl.BlockSpec((1, 1, tm), lambda i: (i, 0, 0), memory_space=pltpu.SMEM)
    nxt = pl.BlockSpec((1, 1, tm), lambda i: (jnp.minimum(i + 1, nt - 1), 0, 0), memory_space=pltpu.SMEM)
    return pl.pallas_call(
        _combine_ple_kernel,
        grid=(nt,),
        in_specs=[cur, cur, nxt, nxt, pl.BlockSpec(memory_space=pl.ANY),
                  row(D), row(LANES),
                  pl.BlockSpec((None, tm, P), lambda i: (l, i, 0)),
                  _layer_mat(l, (D, D)), _layer_mat(l, (P, D)), _layer_vec(l, D)],
        out_specs=row(D),
        out_shape=jax.ShapeDtypeStruct((T, D), F32),
        scratch_shapes=[pltpu.VMEM((2, 2, tm, D), F32), pltpu.SemaphoreType.DMA((2,))],
        compiler_params=_cparams(("arbitrary",)),
        name="combine_ple",
    )(dest0, dest1, dest0, dest1, y_pad, x, meta, p, w_gate, w_proj, g)


def _moe_layout(meta, counts, T):
    bm = MOE_ROWS
    n_rows = 2 * T + N_EXPERTS * bm
    nb = n_rows // bm
    e = meta[:, 0:2].astype(jnp.int32)
    rank = meta[:, 4:6].astype(jnp.int32)
    cnt = counts[0, :N_EXPERTS].astype(jnp.int32)
    padded = ((cnt + bm - 1) // bm) * bm
    pad_ends = jnp.cumsum(padded)
    pad_starts = pad_ends - padded
    start_of = jnp.sum(jnp.where(e[..., None] == jnp.arange(N_EXPERTS, dtype=jnp.int32), pad_starts, 0), axis=-1)
    dest = start_of + rank
    blk_start = jnp.arange(nb, dtype=jnp.int32) * bm
    blk_expert = jnp.minimum(jnp.sum(pad_ends[None, :] <= blk_start[:, None], axis=1), N_EXPERTS - 1)
    n_used = (pad_ends[-1:] // bm).astype(jnp.int32)
    return dest, n_rows, blk_expert.astype(jnp.int32), n_used


def kernel(x, p, ln1_g, w_in, sb_q_g, sb_k_g, sb_out_g, rw_mu, rw_w0, rw_w2, rw_a0, rw_a2, rw_g2, rw_k_k, rw_k_a, rw_r_k, rw_ln_g, rw_ln_b, w_out, ln2_g, router_g, router_g_b, router_e, router_e_b, w_up, w_down, ple_proj, ple_gate, ple_norm_g):
    B, S, D = x.shape
    depth = w_in.shape[0]
    T = B * S
    assert B == 1, "token shift and attention assume one sequence"
    sb_width = D // 2
    rw_width = D - sb_width
    n_sb_heads = sb_width // SB_HEAD_DIM
    lora = LORA_W + LORA_A + LORA_G
    col0 = 3 * sb_width
    W = rw_width
    tm = min(TOKEN_TILE, T)
    tw = min(WIDE_TOKEN_TILE, T)
    vec = lambda a: a.reshape(depth, 1, -1)

    w_in_b = jnp.pad(w_in, ((0, 0), (0, 0), (0, LORA_PAD - lora))).astype(BF16)
    w_out_b = w_out.astype(BF16)
    ple_gate_b = ple_gate.astype(BF16)
    ple_proj_b = ple_proj.astype(BF16)
    mu = vec(rw_mu)
    mu_r, mu_k, mu_v = mu[:, :, :W], mu[:, :, W:2 * W], mu[:, :, 2 * W:3 * W]
    mu_lo = jnp.pad(mu[:, :, 3 * W:], ((0, 0), (0, 0), (0, LORA_PAD - lora)))
    w2 = jnp.pad(rw_w2, ((0, 0), (0, LORA_A), (0, 0)))
    a2 = jnp.pad(rw_a2, ((0, 0), (LORA_W, 0), (0, 0)))
    g2 = jnp.pad(rw_g2, ((0, 0), (0, 2 * LANES - LORA_G), (0, 0)))
    w_r = jnp.pad(jnp.concatenate([router_e, router_g], axis=2),
                  ((0, 0), (0, 0), (0, LANES - N_EXPERTS - N_GROUPS)))
    b_r = vec(jnp.pad(jnp.concatenate([router_e_b, router_g_b], axis=1),
                      ((0, 0), (0, LANES - N_EXPERTS - N_GROUPS))))
    p3 = p.reshape(depth, T, -1)

    xf = x.reshape(T, D)
    for l in range(depth):
        proj = _norm_matmul(xf, vec(ln1_g), w_in_b, l, tm=min(1024, T), tn=LORA_PAD, out_dtype=F32, name="in_proj")
        sb = _sb_attention(proj, vec(sb_q_g), vec(sb_k_g), vec(sb_out_g), l, n_heads=n_sb_heads, tq=256,
                           nsub=min(4, T // 256))
        r, lw, k, v, kk, a, g = _rw_prep(
            proj, mu_r, mu_k, mu_v, mu_lo, vec(rw_w0), w2, vec(rw_a0), a2, g2, vec(rw_k_k), vec(rw_k_a), l,
            rw_width=rw_width, col0=col0, tm=tm)
        rw = _rw_scan(r, lw, k, v, kk, a, g, vec(rw_r_k), vec(rw_ln_g), vec(rw_ln_b), l, npairs=8)
        x1 = _out_proj(xf, sb, rw, w_out_b, l, tm=tw)
        meta, counts = _router(x1, vec(ln2_g), w_r, b_r, l, tm=tw)
        dest, n_rows, blk_expert, n_used = _moe_layout(meta, counts, T)
        tiled = lambda t: (dest[:, 0].reshape(T // t, 1, t), dest[:, 1].reshape(T // t, 1, t))
        x_pad = _dispatch(x1, vec(ln2_g), *tiled(tw), n_rows, l)
        y_pad = _expert_mlp(x_pad, blk_expert, n_used, w_up, w_down, l)
        xf = _combine_ple(x1, y_pad, *tiled(tm), meta, p3, ple_gate_b, ple_proj_b, vec(ple_norm_g), l)
    return xf.reshape(B, S, D)
```

```python
import functools
import math

import jax
import jax.numpy as jnp
from jax import lax
from jax.experimental import pallas as pl
from jax.experimental.pallas import tpu as pltpu

F32 = jnp.float32
BF16 = jnp.bfloat16

SB_HEAD_DIM = 128
RW_HEAD_DIM = 64
LORA_W = 64
LORA_A = 64
LORA_G = 160
LORA_PAD = 512
N_GROUPS = 4
EXPERTS_PER_GROUP = 8
N_EXPERTS = N_GROUPS * EXPERTS_PER_GROUP
RMS_EPS = 1e-6
GN_EPS = 64e-5
LANES = 128
SUBLANES = 8
VMEM_LIMIT = 56 * 1024 * 1024

EXP_UNDERFLOW = -104.0
RW_CHUNK = 64
MOE_ROWS = 256
TOKEN_TILE = 256
WIDE_TOKEN_TILE = 512
GATHER_UNROLL = 8


def _cparams(sem):
    return pltpu.CompilerParams(dimension_semantics=sem, vmem_limit_bytes=VMEM_LIMIT)


def _rms(x, eps=RMS_EPS):
    return x * lax.rsqrt(jnp.mean(x * x, axis=-1, keepdims=True) + eps)


def _softplus(y):
    return jnp.maximum(y, 0.0) + jnp.log(1.0 + jnp.exp(-jnp.abs(y)))


def _mm(a, b):
    return jnp.dot(a.astype(BF16), b.astype(BF16), preferred_element_type=F32)


def _mm_nt(a, b):
    return lax.dot_general(a.astype(BF16), b.astype(BF16), (((1,), (1,)), ((), ())),
                           preferred_element_type=F32)


def _mm_tn(a, b):
    return lax.dot_general(a.astype(BF16), b.astype(BF16), (((0,), (0,)), ((), ())),
                           preferred_element_type=F32)


def _mm_f32(a, b):
    return jnp.dot(a, b, preferred_element_type=F32, precision=lax.Precision.HIGHEST)


def _mm_split(a, b_exact):
    hi = a.astype(BF16)
    lo = (a - hi.astype(F32)).astype(BF16)
    return (jnp.dot(hi, b_exact, preferred_element_type=F32)
            + jnp.dot(lo, b_exact, preferred_element_type=F32))


def _layer_vec(l, width):
    return pl.BlockSpec((None, 1, width), lambda *_: (l, 0, 0))


def _layer_mat(l, shape):
    return pl.BlockSpec((None,) + tuple(shape), lambda *_: (l,) + (0,) * len(shape))


def _norm_matmul_kernel(x_ref, g_ref, w_ref, o_ref, xn_ref):
    @pl.when(pl.program_id(1) == 0)
    def _():
        xn_ref[...] = (_rms(x_ref[...]) * g_ref[...]).astype(BF16)

    o_ref[...] = jnp.dot(xn_ref[...], w_ref[...], preferred_element_type=F32).astype(o_ref.dtype)


def _norm_matmul(x, g, w, l, *, tm, tn, out_dtype, name):
    T, D = x.shape
    N = w.shape[2]
    return pl.pallas_call(
        _norm_matmul_kernel,
        grid=(T // tm, N // tn),
        in_specs=[
            pl.BlockSpec((tm, D), lambda i, j: (i, 0)),
            _layer_vec(l, D),
            pl.BlockSpec((None, D, tn), lambda i, j: (l, 0, j)),
        ],
        out_specs=pl.BlockSpec((tm, tn), lambda i, j: (i, j)),
        out_shape=jax.ShapeDtypeStruct((T, N), out_dtype),
        scratch_shapes=[pltpu.VMEM((tm, D), BF16)],
        compiler_params=_cparams(("parallel", "arbitrary")),
        name=name,
    )(x, g, w)


def _sb_attn_kernel(q_ref, k_ref, v_ref, qg_ref, kg_ref, og_ref, o_ref, kn_ref, vb_ref, *, tq, seq):
    i = pl.program_id(1)
    hd = SB_HEAD_DIM
    prep_rows = min(512, seq)

    @pl.when(i == 0)
    def _():
        def body(c, carry):
            rows = pl.ds(pl.multiple_of(c * prep_rows, prep_rows), prep_rows)
            kn_ref[rows, :] = (_rms(k_ref[rows, :]) * kg_ref[...]).astype(BF16)
            vb_ref[rows, :] = v_ref[rows, :].astype(BF16)
            return carry

        lax.fori_loop(0, seq // prep_rows, body, 0)

    nsub = q_ref.shape[0] // tq
    qn_all = (_rms(q_ref[...]) * qg_ref[...] * (1.0 / math.sqrt(hd))).astype(BF16)
    qn = [qn_all[s * tq:(s + 1) * tq] for s in range(nsub)]
    qi = [nsub * i + s for s in range(nsub)]
    row = lax.broadcasted_iota(jnp.int32, (tq, tq), 0)
    col = lax.broadcasted_iota(jnp.int32, (tq, tq), 1)
    tri = (row > col).astype(BF16)
    causal = col < row

    def scores(s, j, masked, valid=None):
        rows = pl.ds(pl.multiple_of(j * tq, tq), tq)
        z = _mm_nt(qn[s], kn_ref[rows, :])
        sp = _softplus(z)
        log_keep = -sp
        if masked:
            log_keep = jnp.where(causal, log_keep, 0.0)
        if valid is not None:
            log_keep = jnp.where(valid, log_keep, 0.0)
        later = _mm(log_keep, tri)
        return rows, z - sp, log_keep, later, masked, valid

    def accumulate(tile, carry, acc):
        rows, log_beta, log_keep, later, masked, valid = tile
        w = jnp.exp(log_beta + later + carry)
        if masked:
            w = jnp.where(causal, w, 0.0)
        if valid is not None:
            w = jnp.where(valid, w, 0.0)
        acc = acc + jnp.dot(w.astype(BF16), vb_ref[rows, :], preferred_element_type=F32)
        return carry + later[:, 0:1] + log_keep[:, 0:1], acc

    def live(carry):
        return (jnp.max(carry) > EXP_UNDERFLOW).astype(jnp.int32)

    diag = [scores(s, qi[s], True) for s in range(nsub)]
    prev = [scores(s, jnp.maximum(qi[s] - 1, 0), False, None if s > 0 else i > 0) for s in range(nsub)]
    state = []
    for s in range(nsub):
        carry, acc = accumulate(diag[s], jnp.zeros((tq, 1), F32), jnp.zeros((tq, hd), F32))
        state.append(accumulate(prev[s], carry, acc))
    for s in range(nsub):
        carry, acc = state[s]

        def cond(st, s=s):
            return (st[0] < qi[s]) & (st[1] > 0)

        def body(st, s=s):
            n, _, carry, acc = st
            carry, acc = accumulate(scores(s, qi[s] - 1 - n, False), carry, acc)
            return n + 1, live(carry), carry, acc

        _, _, carry, acc = lax.while_loop(cond, body, (jnp.int32(1), live(carry), carry, acc))
        o_ref[s * tq:(s + 1) * tq, :] = (_rms(acc) * og_ref[...]).astype(o_ref.dtype)


def _sb_attention(proj, q_g, k_g, out_g, l, *, n_heads, tq, nsub):
    T = proj.shape[0]
    hd = SB_HEAD_DIM
    tb = nsub * tq
    return pl.pallas_call(
        functools.partial(_sb_attn_kernel, tq=tq, seq=T),
        grid=(n_heads, T // tb),
        in_specs=[
            pl.BlockSpec((tb, hd), lambda h, i: (i, h)),
            pl.BlockSpec((T, hd), lambda h, i: (0, n_heads + h)),
            pl.BlockSpec((T, hd), lambda h, i: (0, 2 * n_heads + h)),
            _layer_vec(l, hd), _layer_vec(l, hd), _layer_vec(l, hd),
        ],
        out_specs=pl.BlockSpec((tb, hd), lambda h, i: (i, h)),
        out_shape=jax.ShapeDtypeStruct((T, n_heads * hd), BF16),
        scratch_shapes=[pltpu.VMEM((T, hd), BF16), pltpu.VMEM((T, hd), BF16)],
        compiler_params=_cparams(("parallel", "arbitrary")),
        name="sb_attention",
    )(proj, proj, proj, q_g, k_g, out_g)


def _rw_prep_kernel(r_ref, k_ref, v_ref, lo_ref, rp_ref, kp_ref, vp_ref, lop_ref,
                    mu_r_ref, mu_k_ref, mu_v_ref, mu_lo_ref, w0_ref, w2_ref, a0_ref, a2_ref, g2_ref,
                    kk_ref, ka_ref,
                    r_out, lw_out, k_out, v_out, kk_out, a_out, g_out):
    first = pl.program_id(0) == 0

    def shift(cur_ref, prev_ref, mu_ref):
        cur = cur_ref[...]
        prev_row = jnp.where(first, 0.0, prev_ref[SUBLANES - 1:SUBLANES, :])
        rolled = pltpu.roll(cur, 1, 0)
        rowi = lax.broadcasted_iota(jnp.int32, cur.shape, 0)
        prev = jnp.where(rowi == 0, prev_row, rolled)
        return cur + mu_ref[...] * (prev - cur)

    r = shift(r_ref, rp_ref, mu_r_ref)
    k = shift(k_ref, kp_ref, mu_k_ref)
    v = shift(v_ref, vp_ref, mu_v_ref)
    lo = shift(lo_ref, lop_ref, mu_lo_ref)
    wa_lo = lo[:, :LORA_W + LORA_A]
    g_lo = lo[:, LORA_W + LORA_A:LORA_W + LORA_A + g2_ref.shape[0]]
    w = -_softplus(-(w0_ref[...] + _mm_f32(jnp.tanh(wa_lo), w2_ref[...]))) - 0.5
    a = jax.nn.sigmoid(a0_ref[...] + _mm_f32(wa_lo, a2_ref[...]))
    g = _mm_f32(jax.nn.sigmoid(g_lo), g2_ref[...])
    r_out[...] = r
    lw_out[...] = -jnp.exp(w)
    k_out[...] = k * (1.0 + (a - 1.0) * ka_ref[...])
    v_out[...] = v
    kk_out[...] = k * kk_ref[...]
    a_out[...] = a
    g_out[...] = g


def _rw_prep(proj, mu_r, mu_k, mu_v, mu_lo, w0, w2, a0, a2, g2, k_k, k_a, l, *, rw_width, col0, tm):
    T = proj.shape[0]
    W = rw_width
    cb = col0 // W
    lb = (col0 + 3 * W) // LORA_PAD

    def cur(width, blk):
        return pl.BlockSpec((tm, width), lambda i: (i, blk))

    def prev(width, blk):
        return pl.BlockSpec((SUBLANES, width), lambda i: (jnp.maximum(i * (tm // SUBLANES) - 1, 0), blk))

    out = jax.ShapeDtypeStruct((T, W), F32)
    return pl.pallas_call(
        _rw_prep_kernel,
        grid=(T // tm,),
        in_specs=[cur(W, cb), cur(W, cb + 1), cur(W, cb + 2), cur(LORA_PAD, lb),
                  prev(W, cb), prev(W, cb + 1), prev(W, cb + 2), prev(LORA_PAD, lb),
                  _layer_vec(l, W), _layer_vec(l, W), _layer_vec(l, W), _layer_vec(l, LORA_PAD),
                  _layer_vec(l, W), _layer_mat(l, w2.shape[1:]), _layer_vec(l, W), _layer_mat(l, a2.shape[1:]),
                  _layer_mat(l, g2.shape[1:]), _layer_vec(l, W), _layer_vec(l, W)],
        out_specs=[pl.BlockSpec((tm, W), lambda i: (i, 0))] * 7,
        out_shape=[out] * 7,
        compiler_params=_cparams(("parallel",)),
        name="rw_prep",
    )(proj, proj, proj, proj, proj, proj, proj, proj,
      mu_r, mu_k, mu_v, mu_lo, w0, w2, a0, a2, g2, k_k, k_a)


def _cumsum_rows(x, tril_bf16):
    hi = x.astype(BF16)
    lo = (x - hi.astype(F32)).astype(BF16)
    return (jnp.dot(tril_bf16, hi, preferred_element_type=F32)
            + jnp.dot(tril_bf16, lo, preferred_element_type=F32))


def _rw_scan_kernel(r_ref, lw_ref, k_ref, v_ref, kk_ref, a_ref, g_ref, rk_ref, lng_ref, lnb_ref,
                    o_ref, state_ref, *, npairs):
    C = RW_CHUNK
    N = RW_HEAD_DIM
    C2 = 2 * C
    assert LANES == 2 * N and C2 == LANES

    @pl.when(pl.program_id(1) == 0)
    def _():
        state_ref[...] = jnp.zeros_like(state_ref)

    row = lax.broadcasted_iota(jnp.int32, (C2, C2), 0)
    col = lax.broadcasted_iota(jnp.int32, (C2, C2), 1)
    lower_incl = col <= row
    lower_strict = col < row
    seg_ones = ((row < N) == (col < N)).astype(BF16)
    tril = (lax.broadcasted_iota(jnp.int32, (C, C), 1) <= lax.broadcasted_iota(jnp.int32, (C, C), 0)).astype(BF16)
    head0 = lax.broadcasted_iota(jnp.int32, (C, LANES), 1) < N

    def stack(x):
        return jnp.concatenate([jnp.where(head0, x, 0.0), jnp.where(head0, 0.0, x)], axis=0)

    def seg_sum(x):
        return _mm(x, seg_ones)

    P = range(npairs)
    sl = [slice(p * LANES, (p + 1) * LANES) for p in P]
    r = [r_ref[:, sl[p]] for p in P]
    lw = [lw_ref[:, sl[p]] for p in P]
    k = [k_ref[:, sl[p]] for p in P]
    v = [v_ref[:, sl[p]] for p in P]
    kk = [kk_ref[:, sl[p]] for p in P]
    a = [a_ref[:, sl[p]] for p in P]
    kkn = [kk[p] * lax.rsqrt(seg_sum(kk[p] * kk[p]) + 1e-12) for p in P]
    bonus = [seg_sum(r[p] * k[p] * rk_ref[:, sl[p]]) * v[p] for p in P]
    cum = [_cumsum_rows(lw[p], tril) for p in P]
    inv = [jnp.exp(-cum[p]) for p in P]
    a2 = [stack(-kkn[p] * jnp.exp(cum[p] - lw[p])).astype(BF16) for p in P]
    r2 = [stack(r[p] * jnp.exp(cum[p])).astype(BF16) for p in P]
    b2 = [stack(kkn[p] * a[p] * inv[p]).astype(BF16) for p in P]
    k2 = [stack(k[p] * inv[p]).astype(BF16) for p in P]
    v2 = [stack(v[p]).astype(BF16) for p in P]
    ar = [jnp.concatenate([a2[p], r2[p]], axis=0) for p in P]
    bk = [jnp.concatenate([b2[p], k2[p]], axis=0) for p in P]
    gram = [_mm_nt(ar[p], bk[p]) for p in P]
    l_ab = [jnp.where(lower_strict, gram[p][:C2, :C2], 0.0).astype(BF16) for p in P]
    l_ak = [jnp.where(lower_strict, gram[p][:C2, C2:], 0.0).astype(BF16) for p in P]
    m_r = [jnp.concatenate([jnp.where(lower_incl, gram[p][C2:, :C2], 0.0),
                            jnp.where(lower_incl, gram[p][C2:, C2:], 0.0)], axis=1).astype(BF16) for p in P]
    s0 = [state_ref[p] for p in P]
    ars = [_mm_nt(ar[p], s0[p]) for p in P]
    x = [ars[p][:C2] + _mm(l_ak[p], v2[p]) for p in P]
    pw = l_ab
    levels = int(math.log2(C))
    for lvl in range(levels):
        if lvl + 1 < levels:
            prod = [_mm(pw[p], jnp.concatenate([x[p].astype(BF16), pw[p]], axis=1)) for p in P]
            x = [x[p] + prod[p][:, :C2] for p in P]
            pw = [prod[p][:, C2:].astype(BF16) for p in P]
        else:
            x = [x[p] + _mm(pw[p], x[p]) for p in P]
    u2 = [x[p].astype(BF16) for p in P]
    y2 = [ars[p][C2:] + _mm(m_r[p], jnp.concatenate([u2[p], v2[p]], axis=0)) for p in P]
    upd = [_mm_tn(jnp.concatenate([u2[p], v2[p]], axis=0), bk[p]) for p in P]
    for p in P:
        state_ref[p] = (s0[p] + upd[p]) * jnp.exp(cum[p][C - 1:C, :])
    y = [y2[p][:C] + y2[p][C:] for p in P]
    yc = [y[p] - seg_sum(y[p]) * (1.0 / N) for p in P]
    var = [seg_sum(yc[p] * yc[p]) * (1.0 / N) for p in P]
    for p in P:
        yn = yc[p] * lax.rsqrt(var[p] + GN_EPS) * lng_ref[:, sl[p]] + lnb_ref[:, sl[p]]
        o_ref[:, sl[p]] = ((yn + bonus[p]) * g_ref[:, sl[p]]).astype(o_ref.dtype)


def _rw_scan(r, lw, k, v, kk, a, g, r_k, ln_g, ln_b, l, *, npairs):
    T, W = r.shape
    C = RW_CHUNK
    wb = npairs * LANES
    blk = pl.BlockSpec((C, wb), lambda p, c: (c, p))
    vec = pl.BlockSpec((None, 1, wb), lambda p, c: (l, 0, p))
    return pl.pallas_call(
        functools.partial(_rw_scan_kernel, npairs=npairs),
        grid=(W // wb, T // C),
        in_specs=[blk] * 7 + [vec] * 3,
        out_specs=blk,
        out_shape=jax.ShapeDtypeStruct((T, W), BF16),
        scratch_shapes=[pltpu.VMEM((npairs, LANES, LANES), F32)],
        compiler_params=_cparams(("parallel", "arbitrary")),
        name="rw_scan",
    )(r, lw, k, v, kk, a, g, r_k, ln_g, ln_b)


def _out_proj_kernel(x_ref, sb_ref, rw_ref, wa_ref, wb_ref, o_ref):
    o_ref[...] = (x_ref[...]
                  + jnp.dot(sb_ref[...], wa_ref[...], preferred_element_type=F32)
                  + jnp.dot(rw_ref[...], wb_ref[...], preferred_element_type=F32))


def _out_proj(x, sb, rw, w_out, l, *, tm):
    T, D = x.shape
    Wa = sb.shape[1]
    Wb = rw.shape[1]
    assert Wa == Wb
    return pl.pallas_call(
        _out_proj_kernel,
        grid=(T // tm,),
        in_specs=[
            pl.BlockSpec((tm, D), lambda i: (i, 0)),
            pl.BlockSpec((tm, Wa), lambda i: (i, 0)),
            pl.BlockSpec((tm, Wb), lambda i: (i, 0)),
            pl.BlockSpec((None, Wa, D), lambda i: (l, 0, 0)),
            pl.BlockSpec((None, Wb, D), lambda i: (l, 1, 0)),
        ],
        out_specs=pl.BlockSpec((tm, D), lambda i: (i, 0)),
        out_shape=jax.ShapeDtypeStruct((T, D), F32),
        compiler_params=_cparams(("parallel",)),
        name="out_proj",
    )(x, sb, rw, w_out, w_out)


def _router_kernel(x_ref, g_ref, w_ref, b_ref, meta_ref, cnt_ref, pad_hbm, carry_ref, zbuf, zsem, *, tm, fills):
    i = pl.program_id(0)
    n = pl.num_programs(0)
    zrows = zbuf.shape[0]

    def fill_copies(step):
        return [pltpu.make_async_copy(
            zbuf, pad_hbm.at[pl.ds(pl.multiple_of((step * fills + c) * zrows, zrows), zrows), :], zsem)
            for c in range(fills)]

    @pl.when(i == 0)
    def _():
        carry_ref[...] = jnp.zeros_like(carry_ref)
        zbuf[...] = jnp.zeros_like(zbuf)

    @pl.when(i > 0)
    def _():
        for c in fill_copies(i - 1):
            c.wait()

    for c in fill_copies(i):
        c.start()

    hn = _rms(x_ref[...]) * g_ref[...]
    logits = _mm_f32(hn, w_ref[...]) + b_ref[...]
    lane = lax.broadcasted_iota(jnp.int32, logits.shape, 1).astype(F32)
    big = float(LANES)
    neg = -jnp.inf
    is_grp = (lane >= N_EXPERTS) & (lane < N_EXPERTS + N_GROUPS)
    gl = jnp.where(is_grp, logits, neg)
    gmax = jnp.max(gl, axis=-1, keepdims=True)
    g_w = 1.0 / jnp.sum(jnp.where(is_grp, jnp.exp(gl - gmax), 0.0), axis=-1, keepdims=True)
    grp = jnp.min(jnp.where(gl == gmax, lane, big), axis=-1, keepdims=True) - N_EXPERTS
    lo = grp * EXPERTS_PER_GROUP
    in_grp = (lane >= lo) & (lane < lo + EXPERTS_PER_GROUP)
    es = jnp.where(in_grp, logits, neg)
    m1 = jnp.max(es, axis=-1, keepdims=True)
    i1 = jnp.min(jnp.where(es == m1, lane, big), axis=-1, keepdims=True)
    es2 = jnp.where(lane == i1, neg, es)
    m2 = jnp.max(es2, axis=-1, keepdims=True)
    i2 = jnp.min(jnp.where(es2 == m2, lane, big), axis=-1, keepdims=True)
    z = jnp.sum(jnp.where(in_grp, jnp.exp(es - m1), 0.0), axis=-1, keepdims=True)
    p1 = 1.0 / z
    p2 = jnp.exp(m2 - m1) / z
    w1 = g_w * (p1 / (p1 + p2))
    w2 = g_w * (p2 / (p1 + p2))
    oh1 = lane == i1
    oh2 = lane == i2
    oh = (oh1 | oh2).astype(BF16)
    row = lax.broadcasted_iota(jnp.int32, (tm, tm), 0)
    col = lax.broadcasted_iota(jnp.int32, (tm, tm), 1)
    before = jnp.dot((col < row).astype(BF16), oh, preferred_element_type=F32) + carry_ref[0:1, :]
    rank1 = jnp.sum(jnp.where(oh1, before, 0.0), axis=-1, keepdims=True)
    rank2 = jnp.sum(jnp.where(oh2, before, 0.0), axis=-1, keepdims=True)
    carry_ref[...] = carry_ref[...] + jnp.sum(oh.astype(F32), axis=0, keepdims=True)
    meta = jnp.where(lane == 0, i1, 0.0)
    meta = jnp.where(lane == 1, i2, meta)
    meta = jnp.where(lane == 2, w1, meta)
    meta = jnp.where(lane == 3, w2, meta)
    meta = jnp.where(lane == 4, rank1, meta)
    meta = jnp.where(lane == 5, rank2, meta)
    meta_ref[...] = meta
    cnt_ref[...] = carry_ref[...]

    @pl.when(i == n - 1)
    def _():
        for c in fill_copies(i):
            c.wait()


def _router(x, g, w, b, l, *, tm, n_rows):
    T, D = x.shape
    steps = T // tm
    zrows = MOE_ROWS
    fills = n_rows // (steps * zrows)
    assert fills * steps * zrows == n_rows
    return pl.pallas_call(
        functools.partial(_router_kernel, tm=tm, fills=fills),
        grid=(steps,),
        in_specs=[
            pl.BlockSpec((tm, D), lambda i: (i, 0)),
            _layer_vec(l, D),
            _layer_mat(l, (D, LANES)),
            _layer_vec(l, LANES),
        ],
        out_specs=[
            pl.BlockSpec((tm, LANES), lambda i: (i, 0)),
            pl.BlockSpec((SUBLANES, LANES), lambda i: (0, 0)),
            pl.BlockSpec(memory_space=pl.ANY),
        ],
        out_shape=[
            jax.ShapeDtypeStruct((T, LANES), F32),
            jax.ShapeDtypeStruct((SUBLANES, LANES), F32),
            jax.ShapeDtypeStruct((n_rows, D), F32),
        ],
        scratch_shapes=[pltpu.VMEM((SUBLANES, LANES), F32), pltpu.VMEM((zrows, D), F32), pltpu.SemaphoreType.DMA],
        compiler_params=_cparams(("arbitrary",)),
        name="router",
    )(x, g, w, b)


def _gather_rows_start(idx_ref, src_hbm, dst, sem):
    n = dst.shape[0]

    def body(grp, carry):
        base = pl.multiple_of(grp * GATHER_UNROLL, GATHER_UNROLL)
        for u in range(GATHER_UNROLL):
            t = idx_ref[0, 0, base + u]
            pltpu.make_async_copy(src_hbm.at[pl.ds(t, 1), :], dst.at[pl.ds(base + u, 1), :], sem).start()
        return carry

    lax.fori_loop(0, n // GATHER_UNROLL, body, 0)


def _gather_rows_wait(src_hbm, dst, sem):
    pltpu.make_async_copy(src_hbm.at[pl.ds(0, dst.shape[0]), :], dst, sem).wait()


def _scatter_rows_start(idx_ref, src, dst_hbm, sem):
    n = src.shape[0]

    def body(grp, carry):
        base = pl.multiple_of(grp * GATHER_UNROLL, GATHER_UNROLL)
        for u in range(GATHER_UNROLL):
            t = idx_ref[0, 0, base + u]
            pltpu.make_async_copy(src.at[pl.ds(base + u, 1), :], dst_hbm.at[pl.ds(t, 1), :], sem).start()
        return carry

    lax.fori_loop(0, n // GATHER_UNROLL, body, 0)


def _scatter_rows_wait(src, dst_hbm, sem):
    pltpu.make_async_copy(src, dst_hbm.at[pl.ds(0, src.shape[0]), :], sem).wait()


def _dispatch_kernel(d0_ref, d1_ref, x_ref, g_ref, zero_hbm, o_hbm, sbuf, sem):
    del zero_hbm
    i = pl.program_id(0)
    n = pl.num_programs(0)
    slot = i % 2
    sbuf[slot] = _rms(x_ref[...]) * g_ref[...]
    _scatter_rows_start(d0_ref, sbuf.at[slot], o_hbm, sem.at[slot])
    _scatter_rows_start(d1_ref, sbuf.at[slot], o_hbm, sem.at[slot])

    def drain(s):
        _scatter_rows_wait(sbuf.at[s], o_hbm, sem.at[s])
        _scatter_rows_wait(sbuf.at[s], o_hbm, sem.at[s])

    @pl.when(i > 0)
    def _():
        drain(1 - slot)

    @pl.when(i == n - 1)
    def _():
        drain(slot)


def _dispatch(x, g, dest0, dest1, zeros, l):
    T, D = x.shape
    n_rows = zeros.shape[0]
    nt, _, tm = dest0.shape
    cur = pl.BlockSpec((1, 1, tm), lambda i: (i, 0, 0), memory_space=pltpu.SMEM)
    return pl.pallas_call(
        _dispatch_kernel,
        grid=(nt,),
        in_specs=[cur, cur, pl.BlockSpec((tm, D), lambda i: (i, 0)), _layer_vec(l, D),
                  pl.BlockSpec(memory_space=pl.ANY)],
        out_specs=pl.BlockSpec(memory_space=pl.ANY),
        out_shape=jax.ShapeDtypeStruct((n_rows, D), F32),
        scratch_shapes=[pltpu.VMEM((2, tm, D), F32), pltpu.SemaphoreType.DMA((2,))],
        input_output_aliases={4: 0},
        compiler_params=_cparams(("arbitrary",)),
        name="dispatch",
    )(dest0, dest1, x, g, zeros)


def _expert_kernel(be_ref, nused_ref, x_ref, wu_ref, wd_ref, o_ref, wu_bf, wd_bf):
    b = pl.program_id(0)
    n_used = nused_ref[0]
    cast_rows = 256

    @pl.when((b < n_used) & ((b == 0) | (be_ref[b] != be_ref[jnp.maximum(b - 1, 0)])))
    def _():
        def cast(ref, dst, c):
            rows = pl.ds(pl.multiple_of(c * cast_rows, cast_rows), cast_rows)
            dst[rows, :] = ref[rows, :].astype(BF16)

        def up(c, carry):
            cast(wu_ref, wu_bf, c)
            return carry

        def down(c, carry):
            cast(wd_ref, wd_bf, c)
            return carry

        lax.fori_loop(0, wu_bf.shape[0] // cast_rows, up, 0)
        lax.fori_loop(0, wd_bf.shape[0] // cast_rows, down, 0)

    @pl.when(b < n_used)
    def _():
        h = jnp.dot(x_ref[...].astype(BF16), wu_bf[...], preferred_element_type=F32)
        ff = h.shape[1] // 2
        gate = h[:, :ff]
        up = h[:, ff:]
        act = (gate * jax.nn.sigmoid(gate) * up).astype(BF16)
        o_ref[...] = jnp.dot(act, wd_bf[...], preferred_element_type=F32).astype(o_ref.dtype)

    @pl.when(b >= n_used)
    def _():
        o_ref[...] = jnp.zeros_like(o_ref)


def _expert_mlp(x_pad, blk_expert, n_used, w_up, w_down, l):
    R, D = x_pad.shape
    bm = MOE_ROWS
    F2 = w_up.shape[3]
    grid_spec = pltpu.PrefetchScalarGridSpec(
        num_scalar_prefetch=2,
        grid=(R // bm,),
        in_specs=[
            pl.BlockSpec((bm, D), lambda b, be, nu: (b, 0)),
            pl.BlockSpec((None, None, D, F2), lambda b, be, nu: (l, be[b], 0, 0)),
            pl.BlockSpec((None, None, F2 // 2, D), lambda b, be, nu: (l, be[b], 0, 0)),
        ],
        out_specs=pl.BlockSpec((bm, D), lambda b, be, nu: (b, 0)),
        scratch_shapes=[pltpu.VMEM((D, F2), BF16), pltpu.VMEM((F2 // 2, D), BF16)],
    )
    return pl.pallas_call(
        _expert_kernel,
        grid_spec=grid_spec,
        out_shape=jax.ShapeDtypeStruct((R, D), F32),
        compiler_params=_cparams(("arbitrary",)),
        name="expert_mlp",
    )(blk_expert, n_used, x_pad, w_up, w_down)


def _combine_ple_kernel(d0_ref, d1_ref, d0n_ref, d1n_ref, y_hbm, x_ref, meta_ref, p_ref, wg_ref, wp_ref, g_ref,
                        o_ref, ybuf, sem):
    i = pl.program_id(0)
    n = pl.num_programs(0)
    slot = i % 2

    def start(s, a_ref, b_ref):
        _gather_rows_start(a_ref, y_hbm, ybuf.at[s, 0], sem.at[s])
        _gather_rows_start(b_ref, y_hbm, ybuf.at[s, 1], sem.at[s])

    @pl.when(i == 0)
    def _():
        start(0, d0_ref, d1_ref)

    @pl.when(i + 1 < n)
    def _():
        start(1 - slot, d0n_ref, d1n_ref)

    _gather_rows_wait(y_hbm, ybuf.at[slot, 0], sem.at[slot])
    _gather_rows_wait(y_hbm, ybuf.at[slot, 1], sem.at[slot])
    meta = meta_ref[...]
    x = x_ref[...] + meta[:, 2:3] * ybuf[slot, 0] + meta[:, 3:4] * ybuf[slot, 1]
    gate = jax.nn.sigmoid(jnp.dot(_rms(x).astype(BF16), wg_ref[...], preferred_element_type=F32))
    pe = jnp.dot(p_ref[...].astype(BF16), wp_ref[...], preferred_element_type=F32)
    o_ref[...] = x + _rms(pe * gate) * g_ref[...]


def _combine_ple(x, y_pad, dest0, dest1, meta, p, w_gate, w_proj, g, l):
    T, D = x.shape
    P = p.shape[2]
    nt, _, tm = dest0.shape
    row = lambda w: pl.BlockSpec((tm, w), lambda i: (i, 0))
    cur = pl.BlockSpec((1, 1, tm), lambda i: (i, 0, 0), memory_space=pltpu.SMEM)
    nxt = pl.BlockSpec((1, 1, tm), lambda i: (jnp.minimum(i + 1, nt - 1), 0, 0), memory_space=pltpu.SMEM)
    return pl.pallas_call(
        _combine_ple_kernel,
        grid=(nt,),
        in_specs=[cur, cur, nxt, nxt, pl.BlockSpec(memory_space=pl.ANY),
                  row(D), row(LANES),
                  pl.BlockSpec((None, tm, P), lambda i: (l, i, 0)),
                  _layer_mat(l, (D, D)), _layer_mat(l, (P, D)), _layer_vec(l, D)],
        out_specs=row(D),
        out_shape=jax.ShapeDtypeStruct((T, D), F32),
        scratch_shapes=[pltpu.VMEM((2, 2, tm, D), F32), pltpu.SemaphoreType.DMA((2,))],
        compiler_params=_cparams(("arbitrary",)),
        name="combine_ple",
    )(dest0, dest1, dest0, dest1, y_pad, x, meta, p, w_gate, w_proj, g)


def _moe_layout(meta, counts, n_rows):
    bm = MOE_ROWS
    nb = n_rows // bm
    e = meta[:, 0:2].astype(jnp.int32)
    rank = meta[:, 4:6].astype(jnp.int32)
    cnt = counts[0, :N_EXPERTS].astype(jnp.int32)
    padded = ((cnt + bm - 1) // bm) * bm
    pad_ends = jnp.cumsum(padded)
    pad_starts = pad_ends - padded
    start_of = jnp.sum(jnp.where(e[..., None] == jnp.arange(N_EXPERTS, dtype=jnp.int32), pad_starts, 0), axis=-1)
    dest = start_of + rank
    blk_start = jnp.arange(nb, dtype=jnp.int32) * bm
    blk_expert = jnp.minimum(jnp.sum(pad_ends[None, :] <= blk_start[:, None], axis=1), N_EXPERTS - 1)
    n_used = (pad_ends[-1:] // bm).astype(jnp.int32)
    return dest, blk_expert.astype(jnp.int32), n_used


def kernel(x, p, ln1_g, w_in, sb_q_g, sb_k_g, sb_out_g, rw_mu, rw_w0, rw_w2, rw_a0, rw_a2, rw_g2, rw_k_k, rw_k_a, rw_r_k, rw_ln_g, rw_ln_b, w_out, ln2_g, router_g, router_g_b, router_e, router_e_b, w_up, w_down, ple_proj, ple_gate, ple_norm_g):
    B, S, D = x.shape
    depth = w_in.shape[0]
    T = B * S
    assert B == 1, "token shift and attention assume one sequence"
    sb_width = D // 2
    rw_width = D - sb_width
    n_sb_heads = sb_width // SB_HEAD_DIM
    lora = LORA_W + LORA_A + LORA_G
    col0 = 3 * sb_width
    W = rw_width
    tm = min(TOKEN_TILE, T)
    tw = min(WIDE_TOKEN_TILE, T)
    n_rows = 2 * T + N_EXPERTS * MOE_ROWS
    vec = lambda a: a.reshape(depth, 1, -1)

    w_in_b = jnp.pad(w_in, ((0, 0), (0, 0), (0, LORA_PAD - lora))).astype(BF16)
    w_out_b = w_out.astype(BF16)
    ple_gate_b = ple_gate.astype(BF16)
    ple_proj_b = ple_proj.astype(BF16)
    mu = vec(rw_mu)
    mu_r, mu_k, mu_v = mu[:, :, :W], mu[:, :, W:2 * W], mu[:, :, 2 * W:3 * W]
    mu_lo = jnp.pad(mu[:, :, 3 * W:], ((0, 0), (0, 0), (0, LORA_PAD - lora)))
    w2 = jnp.pad(rw_w2, ((0, 0), (0, LORA_A), (0, 0)))
    a2 = jnp.pad(rw_a2, ((0, 0), (LORA_W, 0), (0, 0)))
    g2 = jnp.pad(rw_g2, ((0, 0), (0, 2 * LANES - LORA_G), (0, 0)))
    w_r = jnp.pad(jnp.concatenate([router_e, router_g], axis=2),
                  ((0, 0), (0, 0), (0, LANES - N_EXPERTS - N_GROUPS)))
    b_r = vec(jnp.pad(jnp.concatenate([router_e_b, router_g_b], axis=1),
                      ((0, 0), (0, LANES - N_EXPERTS - N_GROUPS))))
    p3 = p.reshape(depth, T, -1)

    xf = x.reshape(T, D)
    for l in range(depth):
        proj = _norm_matmul(xf, vec(ln1_g), w_in_b, l, tm=min(1024, T), tn=LORA_PAD, out_dtype=F32, name="in_proj")
        sb = _sb_attention(proj, vec(sb_q_g), vec(sb_k_g), vec(sb_out_g), l, n_heads=n_sb_heads, tq=256,
                           nsub=min(4, T // 256))
        r, lw, k, v, kk, a, g = _rw_prep(
            proj, mu_r, mu_k, mu_v, mu_lo, vec(rw_w0), w2, vec(rw_a0), a2, g2, vec(rw_k_k), vec(rw_k_a), l,
            rw_width=rw_width, col0=col0, tm=tm)
        rw = _rw_scan(r, lw, k, v, kk, a, g, vec(rw_r_k), vec(rw_ln_g), vec(rw_ln_b), l, npairs=8)
        x1 = _out_proj(xf, sb, rw, w_out_b, l, tm=tw)
        meta, counts, zeros = _router(x1, vec(ln2_g), w_r, b_r, l, tm=tw, n_rows=n_rows)
        dest, blk_expert, n_used = _moe_layout(meta, counts, n_rows)
        tiled = lambda t: (dest[:, 0].reshape(T // t, 1, t), dest[:, 1].reshape(T // t, 1, t))
        x_pad = _dispatch(x1, vec(ln2_g), *tiled(tw), zeros, l)
        y_pad = _expert_mlp(x_pad, blk_expert, n_used, w_up, w_down, l)
        xf = _combine_ple(x1, y_pad, *tiled(tm), meta, p3, ple_gate_b, ple_proj_b, vec(ple_norm_g), l)
    return xf.reshape(B, S, D)
```

```python
import functools
import math

import jax
import jax.numpy as jnp
from jax import lax
from jax.experimental import pallas as pl
from jax.experimental.pallas import tpu as pltpu

F32 = jnp.float32
BF16 = jnp.bfloat16

SB_HEAD_DIM = 128
RW_HEAD_DIM = 64
LORA_W = 64
LORA_A = 64
LORA_G = 160
LORA_PAD = 512
N_GROUPS = 4
EXPERTS_PER_GROUP = 8
N_EXPERTS = N_GROUPS * EXPERTS_PER_GROUP
RMS_EPS = 1e-6
GN_EPS = 64e-5
LANES = 128
SUBLANES = 8
VMEM_LIMIT = 56 * 1024 * 1024

EXP_UNDERFLOW = -104.0
RW_CHUNK = 64
MOE_ROWS = 256
TOKEN_TILE = 256
WIDE_TOKEN_TILE = 512
GATHER_UNROLL = 8


def _cparams(sem):
    return pltpu.CompilerParams(dimension_semantics=sem, vmem_limit_bytes=VMEM_LIMIT)


def _rms(x, eps=RMS_EPS):
    return x * lax.rsqrt(jnp.mean(x * x, axis=-1, keepdims=True) + eps)


def _softplus(y):
    return jnp.maximum(y, 0.0) + jnp.log(1.0 + jnp.exp(-jnp.abs(y)))


def _mm(a, b):
    return jnp.dot(a.astype(BF16), b.astype(BF16), preferred_element_type=F32)


def _mm_nt(a, b):
    return lax.dot_general(a.astype(BF16), b.astype(BF16), (((1,), (1,)), ((), ())),
                           preferred_element_type=F32)


def _mm_tn(a, b):
    return lax.dot_general(a.astype(BF16), b.astype(BF16), (((0,), (0,)), ((), ())),
                           preferred_element_type=F32)


def _mm_f32(a, b):
    return jnp.dot(a, b, preferred_element_type=F32, precision=lax.Precision.HIGHEST)


def _mm_split(a, b_exact):
    hi = a.astype(BF16)
    lo = (a - hi.astype(F32)).astype(BF16)
    return (jnp.dot(hi, b_exact, preferred_element_type=F32)
            + jnp.dot(lo, b_exact, preferred_element_type=F32))


def _layer_vec(l, width):
    return pl.BlockSpec((None, 1, width), lambda *_: (l, 0, 0))


def _layer_mat(l, shape):
    return pl.BlockSpec((None,) + tuple(shape), lambda *_: (l,) + (0,) * len(shape))


def _norm_matmul_kernel(x_ref, g_ref, w_ref, o_ref, xn_ref):
    @pl.when(pl.program_id(1) == 0)
    def _():
        xn_ref[...] = (_rms(x_ref[...]) * g_ref[...]).astype(BF16)

    o_ref[...] = jnp.dot(xn_ref[...], w_ref[...], preferred_element_type=F32).astype(o_ref.dtype)


def _norm_matmul(x, g, w, l, *, tm, tn, out_dtype, name):
    T, D = x.shape
    N = w.shape[2]
    return pl.pallas_call(
        _norm_matmul_kernel,
        grid=(T // tm, N // tn),
        in_specs=[
            pl.BlockSpec((tm, D), lambda i, j: (i, 0)),
            _layer_vec(l, D),
            pl.BlockSpec((None, D, tn), lambda i, j: (l, 0, j)),
        ],
        out_specs=pl.BlockSpec((tm, tn), lambda i, j: (i, j)),
        out_shape=jax.ShapeDtypeStruct((T, N), out_dtype),
        scratch_shapes=[pltpu.VMEM((tm, D), BF16)],
        compiler_params=_cparams(("parallel", "arbitrary")),
        name=name,
    )(x, g, w)


def _sb_attn_kernel(q_ref, k_ref, v_ref, qg_ref, kg_ref, og_ref, o_ref, kn_ref, vb_ref, *, tq, seq):
    i = pl.program_id(1)
    hd = SB_HEAD_DIM
    prep_rows = min(512, seq)

    @pl.when(i == 0)
    def _():
        def body(c, carry):
            rows = pl.ds(pl.multiple_of(c * prep_rows, prep_rows), prep_rows)
            kn_ref[rows, :] = (_rms(k_ref[rows, :]) * kg_ref[...]).astype(BF16)
            vb_ref[rows, :] = v_ref[rows, :].astype(BF16)
            return carry

        lax.fori_loop(0, seq // prep_rows, body, 0)

    nsub = q_ref.shape[0] // tq
    qn_all = (_rms(q_ref[...]) * qg_ref[...] * (1.0 / math.sqrt(hd))).astype(BF16)
    qn = [qn_all[s * tq:(s + 1) * tq] for s in range(nsub)]
    qi = [nsub * i + s for s in range(nsub)]
    row = lax.broadcasted_iota(jnp.int32, (tq, tq), 0)
    col = lax.broadcasted_iota(jnp.int32, (tq, tq), 1)
    tri = (row > col).astype(BF16)
    causal = col < row

    def scores(s, j, masked, valid=None):
        rows = pl.ds(pl.multiple_of(j * tq, tq), tq)
        z = _mm_nt(qn[s], kn_ref[rows, :])
        sp = _softplus(z)
        log_keep = -sp
        if masked:
            log_keep = jnp.where(causal, log_keep, 0.0)
        if valid is not None:
            log_keep = jnp.where(valid, log_keep, 0.0)
        later = _mm(log_keep, tri)
        return rows, z - sp, log_keep, later, masked, valid

    def accumulate(tile, carry, acc):
        rows, log_beta, log_keep, later, masked, valid = tile
        w = jnp.exp(log_beta + later + carry)
        if masked:
            w = jnp.where(causal, w, 0.0)
        if valid is not None:
            w = jnp.where(valid, w, 0.0)
        acc = acc + jnp.dot(w.astype(BF16), vb_ref[rows, :], preferred_element_type=F32)
        return carry + later[:, 0:1] + log_keep[:, 0:1], acc

    def live(carry):
        return (jnp.max(carry) > EXP_UNDERFLOW).astype(jnp.int32)

    diag = [scores(s, qi[s], True) for s in range(nsub)]
    prev = [scores(s, jnp.maximum(qi[s] - 1, 0), False, None if s > 0 else i > 0) for s in range(nsub)]
    state = []
    for s in range(nsub):
        carry, acc = accumulate(diag[s], jnp.zeros((tq, 1), F32), jnp.zeros((tq, hd), F32))
        state.append(accumulate(prev[s], carry, acc))
    for s in range(nsub):
        carry, acc = state[s]

        def cond(st, s=s):
            return (st[0] < qi[s]) & (st[1] > 0)

        def body(st, s=s):
            n, _, carry, acc = st
            carry, acc = accumulate(scores(s, qi[s] - 1 - n, False), carry, acc)
            return n + 1, live(carry), carry, acc

        _, _, carry, acc = lax.while_loop(cond, body, (jnp.int32(1), live(carry), carry, acc))
        o_ref[s * tq:(s + 1) * tq, :] = (_rms(acc) * og_ref[...]).astype(o_ref.dtype)


def _sb_attention(proj, q_g, k_g, out_g, l, *, n_heads, tq, nsub):
    T = proj.shape[0]
    hd = SB_HEAD_DIM
    tb = nsub * tq
    return pl.pallas_call(
        functools.partial(_sb_attn_kernel, tq=tq, seq=T),
        grid=(n_heads, T // tb),
        in_specs=[
            pl.BlockSpec((tb, hd), lambda h, i: (i, h)),
            pl.BlockSpec((T, hd), lambda h, i: (0, n_heads + h)),
            pl.BlockSpec((T, hd), lambda h, i: (0, 2 * n_heads + h)),
            _layer_vec(l, hd), _layer_vec(l, hd), _layer_vec(l, hd),
        ],
        out_specs=pl.BlockSpec((tb, hd), lambda h, i: (i, h)),
        out_shape=jax.ShapeDtypeStruct((T, n_heads * hd), BF16),
        scratch_shapes=[pltpu.VMEM((T, hd), BF16), pltpu.VMEM((T, hd), BF16)],
        compiler_params=_cparams(("parallel", "arbitrary")),
        name="sb_attention",
    )(proj, proj, proj, q_g, k_g, out_g)


def _rw_prep_kernel(r_ref, k_ref, v_ref, lo_ref, rp_ref, kp_ref, vp_ref, lop_ref,
                    mu_r_ref, mu_k_ref, mu_v_ref, mu_lo_ref, w0_ref, w2_ref, a0_ref, a2_ref, g2_ref,
                    kk_ref, ka_ref,
                    r_out, lw_out, k_out, v_out, kk_out, a_out, g_out):
    first = pl.program_id(0) == 0

    def shift(cur_ref, prev_ref, mu_ref):
        cur = cur_ref[...]
        prev_row = jnp.where(first, 0.0, prev_ref[SUBLANES - 1:SUBLANES, :])
        rolled = pltpu.roll(cur, 1, 0)
        rowi = lax.broadcasted_iota(jnp.int32, cur.shape, 0)
        prev = jnp.where(rowi == 0, prev_row, rolled)
        return cur + mu_ref[...] * (prev - cur)

    r = shift(r_ref, rp_ref, mu_r_ref)
    k = shift(k_ref, kp_ref, mu_k_ref)
    v = shift(v_ref, vp_ref, mu_v_ref)
    lo = shift(lo_ref, lop_ref, mu_lo_ref)
    wa_lo = lo[:, :LORA_W + LORA_A]
    g_lo = lo[:, LORA_W + LORA_A:LORA_W + LORA_A + g2_ref.shape[0]]
    w = -_softplus(-(w0_ref[...] + _mm_f32(jnp.tanh(wa_lo), w2_ref[...]))) - 0.5
    a = jax.nn.sigmoid(a0_ref[...] + _mm_f32(wa_lo, a2_ref[...]))
    g = _mm_f32(jax.nn.sigmoid(g_lo), g2_ref[...])
    r_out[...] = r
    lw_out[...] = -jnp.exp(w)
    k_out[...] = k * (1.0 + (a - 1.0) * ka_ref[...])
    v_out[...] = v
    kk_out[...] = k * kk_ref[...]
    a_out[...] = a
    g_out[...] = g


def _rw_prep(proj, mu_r, mu_k, mu_v, mu_lo, w0, w2, a0, a2, g2, k_k, k_a, l, *, rw_width, col0, tm):
    T = proj.shape[0]
    W = rw_width
    cb = col0 // W
    lb = (col0 + 3 * W) // LORA_PAD

    def cur(width, blk):
        return pl.BlockSpec((tm, width), lambda i: (i, blk))

    def prev(width, blk):
        return pl.BlockSpec((SUBLANES, width), lambda i: (jnp.maximum(i * (tm // SUBLANES) - 1, 0), blk))

    out = jax.ShapeDtypeStruct((T, W), F32)
    return pl.pallas_call(
        _rw_prep_kernel,
        grid=(T // tm,),
        in_specs=[cur(W, cb), cur(W, cb + 1), cur(W, cb + 2), cur(LORA_PAD, lb),
                  prev(W, cb), prev(W, cb + 1), prev(W, cb + 2), prev(LORA_PAD, lb),
                  _layer_vec(l, W), _layer_vec(l, W), _layer_vec(l, W), _layer_vec(l, LORA_PAD),
                  _layer_vec(l, W), _layer_mat(l, w2.shape[1:]), _layer_vec(l, W), _layer_mat(l, a2.shape[1:]),
                  _layer_mat(l, g2.shape[1:]), _layer_vec(l, W), _layer_vec(l, W)],
        out_specs=[pl.BlockSpec((tm, W), lambda i: (i, 0))] * 7,
        out_shape=[out] * 7,
        compiler_params=_cparams(("parallel",)),
        name="rw_prep",
    )(proj, proj, proj, proj, proj, proj, proj, proj,
      mu_r, mu_k, mu_v, mu_lo, w0, w2, a0, a2, g2, k_k, k_a)


def _cumsum_rows(x, tril_bf16):
    hi = x.astype(BF16)
    lo = (x - hi.astype(F32)).astype(BF16)
    return (jnp.dot(tril_bf16, hi, preferred_element_type=F32)
            + jnp.dot(tril_bf16, lo, preferred_element_type=F32))


def _rw_scan_kernel(r_ref, lw_ref, k_ref, v_ref, kk_ref, a_ref, g_ref, rk_ref, lng_ref, lnb_ref,
                    o_ref, state_ref, *, npairs):
    C = RW_CHUNK
    N = RW_HEAD_DIM
    C2 = 2 * C
    assert LANES == 2 * N and C2 == LANES

    @pl.when(pl.program_id(1) == 0)
    def _():
        state_ref[...] = jnp.zeros_like(state_ref)

    row = lax.broadcasted_iota(jnp.int32, (C2, C2), 0)
    col = lax.broadcasted_iota(jnp.int32, (C2, C2), 1)
    lower_incl = col <= row
    lower_strict = col < row
    seg_ones = ((row < N) == (col < N)).astype(BF16)
    tril = (lax.broadcasted_iota(jnp.int32, (C, C), 1) <= lax.broadcasted_iota(jnp.int32, (C, C), 0)).astype(BF16)
    head0 = lax.broadcasted_iota(jnp.int32, (C, LANES), 1) < N

    def stack(x):
        return jnp.concatenate([jnp.where(head0, x, 0.0), jnp.where(head0, 0.0, x)], axis=0)

    def seg_sum(x):
        return _mm(x, seg_ones)

    P = range(npairs)
    sl = [slice(p * LANES, (p + 1) * LANES) for p in P]
    r = [r_ref[:, sl[p]] for p in P]
    lw = [lw_ref[:, sl[p]] for p in P]
    k = [k_ref[:, sl[p]] for p in P]
    v = [v_ref[:, sl[p]] for p in P]
    kk = [kk_ref[:, sl[p]] for p in P]
    a = [a_ref[:, sl[p]] for p in P]
    kkn = [kk[p] * lax.rsqrt(seg_sum(kk[p] * kk[p]) + 1e-12) for p in P]
    bonus = [seg_sum(r[p] * k[p] * rk_ref[:, sl[p]]) * v[p] for p in P]
    cum = [_cumsum_rows(lw[p], tril) for p in P]
    inv = [jnp.exp(-cum[p]) for p in P]
    a2 = [stack(-kkn[p] * jnp.exp(cum[p] - lw[p])).astype(BF16) for p in P]
    r2 = [stack(r[p] * jnp.exp(cum[p])).astype(BF16) for p in P]
    b2 = [stack(kkn[p] * a[p] * inv[p]).astype(BF16) for p in P]
    k2 = [stack(k[p] * inv[p]).astype(BF16) for p in P]
    v2 = [stack(v[p]).astype(BF16) for p in P]
    ar = [jnp.concatenate([a2[p], r2[p]], axis=0) for p in P]
    bk = [jnp.concatenate([b2[p], k2[p]], axis=0) for p in P]
    gram = [_mm_nt(ar[p], bk[p]) for p in P]
    l_ab = [jnp.where(lower_strict, gram[p][:C2, :C2], 0.0).astype(BF16) for p in P]
    l_ak = [jnp.where(lower_strict, gram[p][:C2, C2:], 0.0).astype(BF16) for p in P]
    m_r = [jnp.concatenate([jnp.where(lower_incl, gram[p][C2:, :C2], 0.0),
                            jnp.where(lower_incl, gram[p][C2:, C2:], 0.0)], axis=1).astype(BF16) for p in P]
    s0 = [state_ref[p] for p in P]
    ars = [_mm_nt(ar[p], s0[p]) for p in P]
    x = [ars[p][:C2] + _mm(l_ak[p], v2[p]) for p in P]
    pw = l_ab
    levels = int(math.log2(C))
    for lvl in range(levels):
        if lvl + 1 < levels:
            prod = [_mm(pw[p], jnp.concatenate([x[p].astype(BF16), pw[p]], axis=1)) for p in P]
            x = [x[p] + prod[p][:, :C2] for p in P]
            pw = [prod[p][:, C2:].astype(BF16) for p in P]
        else:
            x = [x[p] + _mm(pw[p], x[p]) for p in P]
    u2 = [x[p].astype(BF16) for p in P]
    y2 = [ars[p][C2:] + _mm(m_r[p], jnp.concatenate([u2[p], v2[p]], axis=0)) for p in P]
    upd = [_mm_tn(jnp.concatenate([u2[p], v2[p]], axis=0), bk[p]) for p in P]
    for p in P:
        state_ref[p] = (s0[p] + upd[p]) * jnp.exp(cum[p][C - 1:C, :])
    y = [y2[p][:C] + y2[p][C:] for p in P]
    yc = [y[p] - seg_sum(y[p]) * (1.0 / N) for p in P]
    var = [seg_sum(yc[p] * yc[p]) * (1.0 / N) for p in P]
    for p in P:
        yn = yc[p] * lax.rsqrt(var[p] + GN_EPS) * lng_ref[:, sl[p]] + lnb_ref[:, sl[p]]
        o_ref[:, sl[p]] = ((yn + bonus[p]) * g_ref[:, sl[p]]).astype(o_ref.dtype)


def _rw_scan(r, lw, k, v, kk, a, g, r_k, ln_g, ln_b, l, *, npairs):
    T, W = r.shape
    C = RW_CHUNK
    wb = npairs * LANES
    blk = pl.BlockSpec((C, wb), lambda p, c: (c, p))
    vec = pl.BlockSpec((None, 1, wb), lambda p, c: (l, 0, p))
    return pl.pallas_call(
        functools.partial(_rw_scan_kernel, npairs=npairs),
        grid=(W // wb, T // C),
        in_specs=[blk] * 7 + [vec] * 3,
        out_specs=blk,
        out_shape=jax.ShapeDtypeStruct((T, W), BF16),
        scratch_shapes=[pltpu.VMEM((npairs, LANES, LANES), F32)],
        compiler_params=_cparams(("parallel", "arbitrary")),
        name="rw_scan",
    )(r, lw, k, v, kk, a, g, r_k, ln_g, ln_b)


def _out_proj_kernel(x_ref, sb_ref, rw_ref, wa_ref, wb_ref, o_ref):
    o_ref[...] = (x_ref[...]
                  + jnp.dot(sb_ref[...], wa_ref[...], preferred_element_type=F32)
                  + jnp.dot(rw_ref[...], wb_ref[...], preferred_element_type=F32))


def _out_proj(x, sb, rw, w_out, l, *, tm):
    T, D = x.shape
    Wa = sb.shape[1]
    Wb = rw.shape[1]
    assert Wa == Wb
    return pl.pallas_call(
        _out_proj_kernel,
        grid=(T // tm,),
        in_specs=[
            pl.BlockSpec((tm, D), lambda i: (i, 0)),
            pl.BlockSpec((tm, Wa), lambda i: (i, 0)),
            pl.BlockSpec((tm, Wb), lambda i: (i, 0)),
            pl.BlockSpec((None, Wa, D), lambda i: (l, 0, 0)),
            pl.BlockSpec((None, Wb, D), lambda i: (l, 1, 0)),
        ],
        out_specs=pl.BlockSpec((tm, D), lambda i: (i, 0)),
        out_shape=jax.ShapeDtypeStruct((T, D), F32),
        compiler_params=_cparams(("parallel",)),
        name="out_proj",
    )(x, sb, rw, w_out, w_out)


def _router_kernel(x_ref, g_ref, w_ref, b_ref, meta_ref, cnt_ref, pad_hbm, carry_ref, zbuf, zsem, *, tm, fills):
    i = pl.program_id(0)
    n = pl.num_programs(0)
    zrows = zbuf.shape[0]

    def fill_copies(step):
        return [pltpu.make_async_copy(
            zbuf, pad_hbm.at[pl.ds(pl.multiple_of((step * fills + c) * zrows, zrows), zrows), :], zsem)
            for c in range(fills)]

    @pl.when(i == 0)
    def _():
        carry_ref[...] = jnp.zeros_like(carry_ref)
        zbuf[...] = jnp.zeros_like(zbuf)

    @pl.when(i > 0)
    def _():
        for c in fill_copies(i - 1):
            c.wait()

    for c in fill_copies(i):
        c.start()

    hn = _rms(x_ref[...]) * g_ref[...]
    logits = _mm_f32(hn, w_ref[...]) + b_ref[...]
    lane = lax.broadcasted_iota(jnp.int32, logits.shape, 1).astype(F32)
    big = float(LANES)
    neg = -jnp.inf
    is_grp = (lane >= N_EXPERTS) & (lane < N_EXPERTS + N_GROUPS)
    gl = jnp.where(is_grp, logits, neg)
    gmax = jnp.max(gl, axis=-1, keepdims=True)
    g_w = 1.0 / jnp.sum(jnp.where(is_grp, jnp.exp(gl - gmax), 0.0), axis=-1, keepdims=True)
    grp = jnp.min(jnp.where(gl == gmax, lane, big), axis=-1, keepdims=True) - N_EXPERTS
    lo = grp * EXPERTS_PER_GROUP
    in_grp = (lane >= lo) & (lane < lo + EXPERTS_PER_GROUP)
    es = jnp.where(in_grp, logits, neg)
    m1 = jnp.max(es, axis=-1, keepdims=True)
    i1 = jnp.min(jnp.where(es == m1, lane, big), axis=-1, keepdims=True)
    es2 = jnp.where(lane == i1, neg, es)
    m2 = jnp.max(es2, axis=-1, keepdims=True)
    i2 = jnp.min(jnp.where(es2 == m2, lane, big), axis=-1, keepdims=True)
    z = jnp.sum(jnp.where(in_grp, jnp.exp(es - m1), 0.0), axis=-1, keepdims=True)
    p1 = 1.0 / z
    p2 = jnp.exp(m2 - m1) / z
    w1 = g_w * (p1 / (p1 + p2))
    w2 = g_w * (p2 / (p1 + p2))
    oh1 = lane == i1
    oh2 = lane == i2
    oh = (oh1 | oh2).astype(BF16)
    row = lax.broadcasted_iota(jnp.int32, (tm, tm), 0)
    col = lax.broadcasted_iota(jnp.int32, (tm, tm), 1)
    before = jnp.dot((col < row).astype(BF16), oh, preferred_element_type=F32) + carry_ref[0:1, :]
    rank1 = jnp.sum(jnp.where(oh1, before, 0.0), axis=-1, keepdims=True)
    rank2 = jnp.sum(jnp.where(oh2, before, 0.0), axis=-1, keepdims=True)
    carry_ref[...] = carry_ref[...] + jnp.sum(oh.astype(F32), axis=0, keepdims=True)
    meta = jnp.where(lane == 0, i1, 0.0)
    meta = jnp.where(lane == 1, i2, meta)
    meta = jnp.where(lane == 2, w1, meta)
    meta = jnp.where(lane == 3, w2, meta)
    meta = jnp.where(lane == 4, rank1, meta)
    meta = jnp.where(lane == 5, rank2, meta)
    meta_ref[...] = meta
    cnt_ref[...] = carry_ref[...]

    @pl.when(i == n - 1)
    def _():
        for c in fill_copies(i):
            c.wait()


def _router(x, g, w, b, l, *, tm, n_rows):
    T, D = x.shape
    steps = T // tm
    zrows = MOE_ROWS
    fills = n_rows // (steps * zrows)
    assert fills * steps * zrows == n_rows
    return pl.pallas_call(
        functools.partial(_router_kernel, tm=tm, fills=fills),
        grid=(steps,),
        in_specs=[
            pl.BlockSpec((tm, D), lambda i: (i, 0)),
            _layer_vec(l, D),
            _layer_mat(l, (D, LANES)),
            _layer_vec(l, LANES),
        ],
        out_specs=[
            pl.BlockSpec((tm, LANES), lambda i: (i, 0)),
            pl.BlockSpec((SUBLANES, LANES), lambda i: (0, 0)),
            pl.BlockSpec(memory_space=pl.ANY),
        ],
        out_shape=[
            jax.ShapeDtypeStruct((T, LANES), F32),
            jax.ShapeDtypeStruct((SUBLANES, LANES), F32),
            jax.ShapeDtypeStruct((n_rows, D), F32),
        ],
        scratch_shapes=[pltpu.VMEM((SUBLANES, LANES), F32), pltpu.VMEM((zrows, D), F32), pltpu.SemaphoreType.DMA],
        compiler_params=_cparams(("arbitrary",)),
        name="router",
    )(x, g, w, b)


def _gather_rows_start(idx_ref, src_hbm, dst, sem):
    n = dst.shape[0]

    def body(grp, carry):
        base = pl.multiple_of(grp * GATHER_UNROLL, GATHER_UNROLL)
        for u in range(GATHER_UNROLL):
            t = idx_ref[0, 0, base + u]
            pltpu.make_async_copy(src_hbm.at[pl.ds(t, 1), :], dst.at[pl.ds(base + u, 1), :], sem).start()
        return carry

    lax.fori_loop(0, n // GATHER_UNROLL, body, 0)


def _gather_rows_wait(src_hbm, dst, sem):
    pltpu.make_async_copy(src_hbm.at[pl.ds(0, dst.shape[0]), :], dst, sem).wait()


def _scatter_rows_start(idx_ref, src, dst_hbm, sem):
    n = src.shape[0]

    def body(grp, carry):
        base = pl.multiple_of(grp * GATHER_UNROLL, GATHER_UNROLL)
        for u in range(GATHER_UNROLL):
            t = idx_ref[0, 0, base + u]
            pltpu.make_async_copy(src.at[pl.ds(base + u, 1), :], dst_hbm.at[pl.ds(t, 1), :], sem).start()
        return carry

    lax.fori_loop(0, n // GATHER_UNROLL, body, 0)


def _scatter_rows_wait(src, dst_hbm, sem):
    pltpu.make_async_copy(src, dst_hbm.at[pl.ds(0, src.shape[0]), :], sem).wait()


def _dispatch_kernel(d0_ref, d1_ref, x_ref, g_ref, zero_hbm, o_hbm, sbuf, sem):
    del zero_hbm
    i = pl.program_id(0)
    n = pl.num_programs(0)
    slot = i % 2
    sbuf[slot] = _rms(x_ref[...]) * g_ref[...]
    _scatter_rows_start(d0_ref, sbuf.at[slot], o_hbm, sem.at[slot])
    _scatter_rows_start(d1_ref, sbuf.at[slot], o_hbm, sem.at[slot])

    def drain(s):
        _scatter_rows_wait(sbuf.at[s], o_hbm, sem.at[s])
        _scatter_rows_wait(sbuf.at[s], o_hbm, sem.at[s])

    @pl.when(i > 0)
    def _():
        drain(1 - slot)

    @pl.when(i == n - 1)
    def _():
        drain(slot)


def _dispatch(x, g, dest0, dest1, zeros, l):
    T, D = x.shape
    n_rows = zeros.shape[0]
    nt, _, tm = dest0.shape
    cur = pl.BlockSpec((1, 1, tm), lambda i: (i, 0, 0), memory_space=pltpu.SMEM)
    return pl.pallas_call(
        _dispatch_kernel,
        grid=(nt,),
        in_specs=[cur, cur, pl.BlockSpec((tm, D), lambda i: (i, 0)), _layer_vec(l, D),
                  pl.BlockSpec(memory_space=pl.ANY)],
        out_specs=pl.BlockSpec(memory_space=pl.ANY),
        out_shape=jax.ShapeDtypeStruct((n_rows, D), F32),
        scratch_shapes=[pltpu.VMEM((2, tm, D), F32), pltpu.SemaphoreType.DMA((2,))],
        input_output_aliases={4: 0},
        compiler_params=_cparams(("arbitrary",)),
        name="dispatch",
    )(dest0, dest1, x, g, zeros)


def _expert_kernel(be_ref, nused_ref, x_ref, wu_ref, wd_ref, o_ref):
    del be_ref
    b = pl.program_id(0)
    n_used = nused_ref[0]

    @pl.when(b < n_used)
    def _():
        h = jnp.dot(x_ref[...].astype(BF16), wu_ref[...].astype(BF16), preferred_element_type=F32)
        ff = h.shape[1] // 2
        gate = h[:, :ff]
        up = h[:, ff:]
        act = (gate * jax.nn.sigmoid(gate) * up).astype(BF16)
        o_ref[...] = jnp.dot(act, wd_ref[...].astype(BF16), preferred_element_type=F32).astype(o_ref.dtype)

    @pl.when(b >= n_used)
    def _():
        o_ref[...] = jnp.zeros_like(o_ref)


def _expert_mlp(x_pad, blk_expert, n_used, w_up, w_down, l):
    R, D = x_pad.shape
    bm = MOE_ROWS
    F2 = w_up.shape[3]
    grid_spec = pltpu.PrefetchScalarGridSpec(
        num_scalar_prefetch=2,
        grid=(R // bm,),
        in_specs=[
            pl.BlockSpec((bm, D), lambda b, be, nu: (b, 0)),
            pl.BlockSpec((None, None, D, F2), lambda b, be, nu: (l, be[b], 0, 0)),
            pl.BlockSpec((None, None, F2 // 2, D), lambda b, be, nu: (l, be[b], 0, 0)),
        ],
        out_specs=pl.BlockSpec((bm, D), lambda b, be, nu: (b, 0)),
    )
    return pl.pallas_call(
        _expert_kernel,
        grid_spec=grid_spec,
        out_shape=jax.ShapeDtypeStruct((R, D), F32),
        compiler_params=_cparams(("arbitrary",)),
        name="expert_mlp",
    )(blk_expert, n_used, x_pad, w_up, w_down)


def _combine_ple_kernel(d0_ref, d1_ref, d0n_ref, d1n_ref, y_hbm, x_ref, meta_ref, p_ref, wg_ref, wp_ref, g_ref,
                        o_ref, ybuf, sem):
    i = pl.program_id(0)
    n = pl.num_programs(0)
    slot = i % 2

    def start(s, a_ref, b_ref):
        _gather_rows_start(a_ref, y_hbm, ybuf.at[s, 0], sem.at[s])
        _gather_rows_start(b_ref, y_hbm, ybuf.at[s, 1], sem.at[s])

    @pl.when(i == 0)
    def _():
        start(0, d0_ref, d1_ref)

    @pl.when(i + 1 < n)
    def _():
        start(1 - slot, d0n_ref, d1n_ref)

    _gather_rows_wait(y_hbm, ybuf.at[slot, 0], sem.at[slot])
    _gather_rows_wait(y_hbm, ybuf.at[slot, 1], sem.at[slot])
    meta = meta_ref[...]
    x = x_ref[...] + meta[:, 2:3] * ybuf[slot, 0] + meta[:, 3:4] * ybuf[slot, 1]
    gate = jax.nn.sigmoid(jnp.dot(_rms(x).astype(BF16), wg_ref[...], preferred_element_type=F32))
    pe = jnp.dot(p_ref[...].astype(BF16), wp_ref[...], preferred_element_type=F32)
    o_ref[...] = x + _rms(pe * gate) * g_ref[...]


def _combine_ple(x, y_pad, dest0, dest1, meta, p, w_gate, w_proj, g, l):
    T, D = x.shape
    P = p.shape[2]
    nt, _, tm = dest0.shape
    row = lambda w: pl.BlockSpec((tm, w), lambda i: (i, 0))
    cur = pl.BlockSpec((1, 1, tm), lambda i: (i, 0, 0), memory_space=pltpu.SMEM)
    nxt = pl.BlockSpec((1, 1, tm), lambda i: (jnp.minimum(i + 1, nt - 1), 0, 0), memory_space=pltpu.SMEM)
    return pl.pallas_call(
        _combine_ple_kernel,
        grid=(nt,),
        in_specs=[cur, cur, nxt, nxt, pl.BlockSpec(memory_space=pl.ANY),
                  row(D), row(LANES),
                  pl.BlockSpec((None, tm, P), lambda i: (l, i, 0)),
                  _layer_mat(l, (D, D)), _layer_mat(l, (P, D)), _layer_vec(l, D)],
        out_specs=row(D),
        out_shape=jax.ShapeDtypeStruct((T, D), F32),
        scratch_shapes=[pltpu.VMEM((2, 2, tm, D), F32), pltpu.SemaphoreType.DMA((2,))],
        compiler_params=_cparams(("arbitrary",)),
        name="combine_ple",
    )(dest0, dest1, dest0, dest1, y_pad, x, meta, p, w_gate, w_proj, g)


def _moe_layout(meta, counts, n_rows):
    bm = MOE_ROWS
    nb = n_rows // bm
    e = meta[:, 0:2].astype(jnp.int32)
    rank = meta[:, 4:6].astype(jnp.int32)
    cnt = counts[0, :N_EXPERTS].astype(jnp.int32)
    padded = ((cnt + bm - 1) // bm) * bm
    pad_ends = jnp.cumsum(padded)
    pad_starts = pad_ends - padded
    start_of = jnp.sum(jnp.where(e[..., None] == jnp.arange(N_EXPERTS, dtype=jnp.int32), pad_starts, 0), axis=-1)
    dest = start_of + rank
    blk_start = jnp.arange(nb, dtype=jnp.int32) * bm
    blk_expert = jnp.minimum(jnp.sum(pad_ends[None, :] <= blk_start[:, None], axis=1), N_EXPERTS - 1)
    n_used = (pad_ends[-1:] // bm).astype(jnp.int32)
    return dest, blk_expert.astype(jnp.int32), n_used


def kernel(x, p, ln1_g, w_in, sb_q_g, sb_k_g, sb_out_g, rw_mu, rw_w0, rw_w2, rw_a0, rw_a2, rw_g2, rw_k_k, rw_k_a, rw_r_k, rw_ln_g, rw_ln_b, w_out, ln2_g, router_g, router_g_b, router_e, router_e_b, w_up, w_down, ple_proj, ple_gate, ple_norm_g):
    B, S, D = x.shape
    depth = w_in.shape[0]
    T = B * S
    assert B == 1, "token shift and attention assume one sequence"
    sb_width = D // 2
    rw_width = D - sb_width
    n_sb_heads = sb_width // SB_HEAD_DIM
    lora = LORA_W + LORA_A + LORA_G
    col0 = 3 * sb_width
    W = rw_width
    tm = min(TOKEN_TILE, T)
    tw = min(WIDE_TOKEN_TILE, T)
    n_rows = 2 * T + N_EXPERTS * MOE_ROWS
    vec = lambda a: a.reshape(depth, 1, -1)

    w_in_b = jnp.pad(w_in, ((0, 0), (0, 0), (0, LORA_PAD - lora))).astype(BF16)
    w_out_b = w_out.astype(BF16)
    ple_gate_b = ple_gate.astype(BF16)
    ple_proj_b = ple_proj.astype(BF16)
    mu = vec(rw_mu)
    mu_r, mu_k, mu_v = mu[:, :, :W], mu[:, :, W:2 * W], mu[:, :, 2 * W:3 * W]
    mu_lo = jnp.pad(mu[:, :, 3 * W:], ((0, 0), (0, 0), (0, LORA_PAD - lora)))
    w2 = jnp.pad(rw_w2, ((0, 0), (0, LORA_A), (0, 0)))
    a2 = jnp.pad(rw_a2, ((0, 0), (LORA_W, 0), (0, 0)))
    g2 = jnp.pad(rw_g2, ((0, 0), (0, 2 * LANES - LORA_G), (0, 0)))
    w_r = jnp.pad(jnp.concatenate([router_e, router_g], axis=2),
                  ((0, 0), (0, 0), (0, LANES - N_EXPERTS - N_GROUPS)))
    b_r = vec(jnp.pad(jnp.concatenate([router_e_b, router_g_b], axis=1),
                      ((0, 0), (0, LANES - N_EXPERTS - N_GROUPS))))
    p3 = p.reshape(depth, T, -1)

    xf = x.reshape(T, D)
    for l in range(depth):
        proj = _norm_matmul(xf, vec(ln1_g), w_in_b, l, tm=min(1024, T), tn=LORA_PAD, out_dtype=F32, name="in_proj")
        sb = _sb_attention(proj, vec(sb_q_g), vec(sb_k_g), vec(sb_out_g), l, n_heads=n_sb_heads, tq=256,
                           nsub=min(4, T // 256))
        r, lw, k, v, kk, a, g = _rw_prep(
            proj, mu_r, mu_k, mu_v, mu_lo, vec(rw_w0), w2, vec(rw_a0), a2, g2, vec(rw_k_k), vec(rw_k_a), l,
            rw_width=rw_width, col0=col0, tm=tm)
        rw = _rw_scan(r, lw, k, v, kk, a, g, vec(rw_r_k), vec(rw_ln_g), vec(rw_ln_b), l, npairs=8)
        x1 = _out_proj(xf, sb, rw, w_out_b, l, tm=tw)
        meta, counts, zeros = _router(x1, vec(ln2_g), w_r, b_r, l, tm=tw, n_rows=n_rows)
        dest, blk_expert, n_used = _moe_layout(meta, counts, n_rows)
        tiled = lambda t: (dest[:, 0].reshape(T // t, 1, t), dest[:, 1].reshape(T // t, 1, t))
        x_pad = _dispatch(x1, vec(ln2_g), *tiled(tw), zeros, l)
        y_pad = _expert_mlp(x_pad, blk_expert, n_used, w_up, w_down, l)
        xf = _combine_ple(x1, y_pad, *tiled(tm), meta, p3, ple_gate_b, ple_proj_b, vec(ple_norm_g), l)
    return xf.reshape(B, S, D)
```

```python
import functools
import math

import jax
import jax.numpy as jnp
from jax import lax
from jax.experimental import pallas as pl
from jax.experimental.pallas import tpu as pltpu

F32 = jnp.float32
BF16 = jnp.bfloat16

SB_HEAD_DIM = 128
RW_HEAD_DIM = 64
LORA_W = 64
LORA_A = 64
LORA_G = 160
LORA_PAD = 512
N_GROUPS = 4
EXPERTS_PER_GROUP = 8
N_EXPERTS = N_GROUPS * EXPERTS_PER_GROUP
RMS_EPS = 1e-6
GN_EPS = 64e-5
LANES = 128
SUBLANES = 8
VMEM_LIMIT = 56 * 1024 * 1024

EXP_UNDERFLOW = -104.0
RW_CHUNK = 64
MOE_ROWS = 256
TOKEN_TILE = 256
WIDE_TOKEN_TILE = 512
GATHER_UNROLL = 8


def _cparams(sem):
    return pltpu.CompilerParams(dimension_semantics=sem, vmem_limit_bytes=VMEM_LIMIT)


def _rms(x, eps=RMS_EPS):
    return x * lax.rsqrt(jnp.mean(x * x, axis=-1, keepdims=True) + eps)


def _softplus(y):
    return jnp.maximum(y, 0.0) + jnp.log(1.0 + jnp.exp(-jnp.abs(y)))


def _mm(a, b):
    return jnp.dot(a.astype(BF16), b.astype(BF16), preferred_element_type=F32)


def _mm_nt(a, b):
    return lax.dot_general(a.astype(BF16), b.astype(BF16), (((1,), (1,)), ((), ())),
                           preferred_element_type=F32)


def _mm_tn(a, b):
    return lax.dot_general(a.astype(BF16), b.astype(BF16), (((0,), (0,)), ((), ())),
                           preferred_element_type=F32)


def _mm_f32(a, b):
    return jnp.dot(a, b, preferred_element_type=F32, precision=lax.Precision.HIGHEST)


def _mm_split(a, b_exact):
    hi = a.astype(BF16)
    lo = (a - hi.astype(F32)).astype(BF16)
    return (jnp.dot(hi, b_exact, preferred_element_type=F32)
            + jnp.dot(lo, b_exact, preferred_element_type=F32))


def _layer_vec(l, width):
    return pl.BlockSpec((None, 1, width), lambda *_: (l, 0, 0))


def _layer_mat(l, shape):
    return pl.BlockSpec((None,) + tuple(shape), lambda *_: (l,) + (0,) * len(shape))


def _norm_matmul_kernel(x_ref, g_ref, w_ref, o_ref, xn_ref):
    @pl.when(pl.program_id(1) == 0)
    def _():
        xn_ref[...] = (_rms(x_ref[...]) * g_ref[...]).astype(BF16)

    o_ref[...] = jnp.dot(xn_ref[...], w_ref[...], preferred_element_type=F32).astype(o_ref.dtype)


def _norm_matmul(x, g, w, l, *, tm, tn, out_dtype, name):
    T, D = x.shape
    N = w.shape[2]
    return pl.pallas_call(
        _norm_matmul_kernel,
        grid=(T // tm, N // tn),
        in_specs=[
            pl.BlockSpec((tm, D), lambda i, j: (i, 0)),
            _layer_vec(l, D),
            pl.BlockSpec((None, D, tn), lambda i, j: (l, 0, j)),
        ],
        out_specs=pl.BlockSpec((tm, tn), lambda i, j: (i, j)),
        out_shape=jax.ShapeDtypeStruct((T, N), out_dtype),
        scratch_shapes=[pltpu.VMEM((tm, D), BF16)],
        compiler_params=_cparams(("parallel", "arbitrary")),
        name=name,
    )(x, g, w)


def _sb_attn_kernel(q_ref, k_ref, v_ref, qg_ref, kg_ref, og_ref, o_ref, kn_ref, vb_ref, *, tq, seq):
    i = pl.program_id(1)
    hd = SB_HEAD_DIM
    prep_rows = min(512, seq)

    @pl.when(i == 0)
    def _():
        def body(c, carry):
            rows = pl.ds(pl.multiple_of(c * prep_rows, prep_rows), prep_rows)
            kn_ref[rows, :] = (_rms(k_ref[rows, :]) * kg_ref[...]).astype(BF16)
            vb_ref[rows, :] = v_ref[rows, :].astype(BF16)
            return carry

        lax.fori_loop(0, seq // prep_rows, body, 0)

    nsub = q_ref.shape[0] // tq
    qn_all = (_rms(q_ref[...]) * qg_ref[...] * (1.0 / math.sqrt(hd))).astype(BF16)
    qn = [qn_all[s * tq:(s + 1) * tq] for s in range(nsub)]
    qi = [nsub * i + s for s in range(nsub)]
    row = lax.broadcasted_iota(jnp.int32, (tq, tq), 0)
    col = lax.broadcasted_iota(jnp.int32, (tq, tq), 1)
    tri = (row > col).astype(BF16)
    causal = col < row

    def scores(s, j, masked, valid=None):
        rows = pl.ds(pl.multiple_of(j * tq, tq), tq)
        z = _mm_nt(qn[s], kn_ref[rows, :])
        sp = _softplus(z)
        log_keep = -sp
        if masked:
            log_keep = jnp.where(causal, log_keep, 0.0)
        if valid is not None:
            log_keep = jnp.where(valid, log_keep, 0.0)
        later = _mm(log_keep, tri)
        return rows, z - sp, log_keep, later, masked, valid

    def accumulate(tile, carry, acc):
        rows, log_beta, log_keep, later, masked, valid = tile
        w = jnp.exp(log_beta + later + carry)
        if masked:
            w = jnp.where(causal, w, 0.0)
        if valid is not None:
            w = jnp.where(valid, w, 0.0)
        acc = acc + jnp.dot(w.astype(BF16), vb_ref[rows, :], preferred_element_type=F32)
        return carry + later[:, 0:1] + log_keep[:, 0:1], acc

    def live(carry):
        return (jnp.max(carry) > EXP_UNDERFLOW).astype(jnp.int32)

    diag = [scores(s, qi[s], True) for s in range(nsub)]
    prev = [scores(s, jnp.maximum(qi[s] - 1, 0), False, None if s > 0 else i > 0) for s in range(nsub)]
    state = []
    for s in range(nsub):
        carry, acc = accumulate(diag[s], jnp.zeros((tq, 1), F32), jnp.zeros((tq, hd), F32))
        state.append(accumulate(prev[s], carry, acc))
    for s in range(nsub):
        carry, acc = state[s]

        def cond(st, s=s):
            return (st[0] < qi[s]) & (st[1] > 0)

        def body(st, s=s):
            n, _, carry, acc = st
            carry, acc = accumulate(scores(s, qi[s] - 1 - n, False), carry, acc)
            return n + 1, live(carry), carry, acc

        _, _, carry, acc = lax.while_loop(cond, body, (jnp.int32(1), live(carry), carry, acc))
        o_ref[s * tq:(s + 1) * tq, :] = (_rms(acc) * og_ref[...]).astype(o_ref.dtype)


def _sb_attention(proj, q_g, k_g, out_g, l, *, n_heads, tq, nsub):
    T = proj.shape[0]
    hd = SB_HEAD_DIM
    tb = nsub * tq
    return pl.pallas_call(
        functools.partial(_sb_attn_kernel, tq=tq, seq=T),
        grid=(n_heads, T // tb),
        in_specs=[
            pl.BlockSpec((tb, hd), lambda h, i: (i, h)),
            pl.BlockSpec((T, hd), lambda h, i: (0, n_heads + h)),
            pl.BlockSpec((T, hd), lambda h, i: (0, 2 * n_heads + h)),
            _layer_vec(l, hd), _layer_vec(l, hd), _layer_vec(l, hd),
        ],
        out_specs=pl.BlockSpec((tb, hd), lambda h, i: (i, h)),
        out_shape=jax.ShapeDtypeStruct((T, n_heads * hd), BF16),
        scratch_shapes=[pltpu.VMEM((T, hd), BF16), pltpu.VMEM((T, hd), BF16)],
        compiler_params=_cparams(("parallel", "arbitrary")),
        name="sb_attention",
    )(proj, proj, proj, q_g, k_g, out_g)


def _rw_prep_kernel(r_ref, k_ref, v_ref, lo_ref, rp_ref, kp_ref, vp_ref, lop_ref,
                    mu_r_ref, mu_k_ref, mu_v_ref, mu_lo_ref, w0_ref, w2_ref, a0_ref, a2_ref, g2_ref,
                    kk_ref, ka_ref,
                    r_out, lw_out, k_out, v_out, kk_out, a_out, g_out):
    first = pl.program_id(0) == 0

    def shift(cur_ref, prev_ref, mu_ref):
        cur = cur_ref[...]
        prev_row = jnp.where(first, 0.0, prev_ref[SUBLANES - 1:SUBLANES, :])
        rolled = pltpu.roll(cur, 1, 0)
        rowi = lax.broadcasted_iota(jnp.int32, cur.shape, 0)
        prev = jnp.where(rowi == 0, prev_row, rolled)
        return cur + mu_ref[...] * (prev - cur)

    r = shift(r_ref, rp_ref, mu_r_ref)
    k = shift(k_ref, kp_ref, mu_k_ref)
    v = shift(v_ref, vp_ref, mu_v_ref)
    lo = shift(lo_ref, lop_ref, mu_lo_ref)
    wa_lo = lo[:, :LORA_W + LORA_A]
    g_lo = lo[:, LORA_W + LORA_A:LORA_W + LORA_A + g2_ref.shape[0]]
    w = -_softplus(-(w0_ref[...] + _mm_f32(jnp.tanh(wa_lo), w2_ref[...]))) - 0.5
    a = jax.nn.sigmoid(a0_ref[...] + _mm_f32(wa_lo, a2_ref[...]))
    g = _mm_f32(jax.nn.sigmoid(g_lo), g2_ref[...])
    r_out[...] = r
    lw_out[...] = -jnp.exp(w)
    k_out[...] = k * (1.0 + (a - 1.0) * ka_ref[...])
    v_out[...] = v
    kk_out[...] = k * kk_ref[...]
    a_out[...] = a
    g_out[...] = g


def _rw_prep(proj, mu_r, mu_k, mu_v, mu_lo, w0, w2, a0, a2, g2, k_k, k_a, l, *, rw_width, col0, tm):
    T = proj.shape[0]
    W = rw_width
    cb = col0 // W
    lb = (col0 + 3 * W) // LORA_PAD

    def cur(width, blk):
        return pl.BlockSpec((tm, width), lambda i: (i, blk))

    def prev(width, blk):
        return pl.BlockSpec((SUBLANES, width), lambda i: (jnp.maximum(i * (tm // SUBLANES) - 1, 0), blk))

    out = jax.ShapeDtypeStruct((T, W), F32)
    return pl.pallas_call(
        _rw_prep_kernel,
        grid=(T // tm,),
        in_specs=[cur(W, cb), cur(W, cb + 1), cur(W, cb + 2), cur(LORA_PAD, lb),
                  prev(W, cb), prev(W, cb + 1), prev(W, cb + 2), prev(LORA_PAD, lb),
                  _layer_vec(l, W), _layer_vec(l, W), _layer_vec(l, W), _layer_vec(l, LORA_PAD),
                  _layer_vec(l, W), _layer_mat(l, w2.shape[1:]), _layer_vec(l, W), _layer_mat(l, a2.shape[1:]),
                  _layer_mat(l, g2.shape[1:]), _layer_vec(l, W), _layer_vec(l, W)],
        out_specs=[pl.BlockSpec((tm, W), lambda i: (i, 0))] * 7,
        out_shape=[out] * 7,
        compiler_params=_cparams(("parallel",)),
        name="rw_prep",
    )(proj, proj, proj, proj, proj, proj, proj, proj,
      mu_r, mu_k, mu_v, mu_lo, w0, w2, a0, a2, g2, k_k, k_a)


def _cumsum_rows(x, tril_bf16):
    hi = x.astype(BF16)
    lo = (x - hi.astype(F32)).astype(BF16)
    return (jnp.dot(tril_bf16, hi, preferred_element_type=F32)
            + jnp.dot(tril_bf16, lo, preferred_element_type=F32))


def _rw_scan_kernel(r_ref, lw_ref, k_ref, v_ref, kk_ref, a_ref, g_ref, rk_ref, lng_ref, lnb_ref,
                    o_ref, state_ref, *, npairs):
    C = RW_CHUNK
    N = RW_HEAD_DIM
    C2 = 2 * C
    assert LANES == 2 * N and C2 == LANES

    @pl.when(pl.program_id(1) == 0)
    def _():
        state_ref[...] = jnp.zeros_like(state_ref)

    row = lax.broadcasted_iota(jnp.int32, (C2, C2), 0)
    col = lax.broadcasted_iota(jnp.int32, (C2, C2), 1)
    lower_incl = col <= row
    lower_strict = col < row
    seg_ones = ((row < N) == (col < N)).astype(BF16)
    tril = (lax.broadcasted_iota(jnp.int32, (C, C), 1) <= lax.broadcasted_iota(jnp.int32, (C, C), 0)).astype(BF16)
    head0 = lax.broadcasted_iota(jnp.int32, (C, LANES), 1) < N

    def stack(x):
        return jnp.concatenate([jnp.where(head0, x, 0.0), jnp.where(head0, 0.0, x)], axis=0)

    def seg_sum(x):
        return _mm(x, seg_ones)

    P = range(npairs)
    sl = [slice(p * LANES, (p + 1) * LANES) for p in P]
    r = [r_ref[:, sl[p]] for p in P]
    lw = [lw_ref[:, sl[p]] for p in P]
    k = [k_ref[:, sl[p]] for p in P]
    v = [v_ref[:, sl[p]] for p in P]
    kk = [kk_ref[:, sl[p]] for p in P]
    a = [a_ref[:, sl[p]] for p in P]
    kkn = [kk[p] * lax.rsqrt(seg_sum(kk[p] * kk[p]) + 1e-12) for p in P]
    bonus = [seg_sum(r[p] * k[p] * rk_ref[:, sl[p]]) * v[p] for p in P]
    cum = [_cumsum_rows(lw[p], tril) for p in P]
    inv = [jnp.exp(-cum[p]) for p in P]
    a2 = [stack(-kkn[p] * jnp.exp(cum[p] - lw[p])).astype(BF16) for p in P]
    r2 = [stack(r[p] * jnp.exp(cum[p])).astype(BF16) for p in P]
    b2 = [stack(kkn[p] * a[p] * inv[p]).astype(BF16) for p in P]
    k2 = [stack(k[p] * inv[p]).astype(BF16) for p in P]
    v2 = [stack(v[p]).astype(BF16) for p in P]
    ar = [jnp.concatenate([a2[p], r2[p]], axis=0) for p in P]
    bk = [jnp.concatenate([b2[p], k2[p]], axis=0) for p in P]
    gram = [_mm_nt(ar[p], bk[p]) for p in P]
    l_ab = [jnp.where(lower_strict, gram[p][:C2, :C2], 0.0).astype(BF16) for p in P]
    l_ak = [jnp.where(lower_strict, gram[p][:C2, C2:], 0.0).astype(BF16) for p in P]
    m_r = [jnp.concatenate([jnp.where(lower_incl, gram[p][C2:, :C2], 0.0),
                            jnp.where(lower_incl, gram[p][C2:, C2:], 0.0)], axis=1).astype(BF16) for p in P]
    s0 = [state_ref[p] for p in P]
    ars = [_mm_nt(ar[p], s0[p]) for p in P]
    x = [ars[p][:C2] + _mm(l_ak[p], v2[p]) for p in P]
    pw = l_ab
    levels = int(math.log2(C))
    for lvl in range(levels):
        if lvl + 1 < levels:
            prod = [_mm(pw[p], jnp.concatenate([x[p].astype(BF16), pw[p]], axis=1)) for p in P]
            x = [x[p] + prod[p][:, :C2] for p in P]
            pw = [prod[p][:, C2:].astype(BF16) for p in P]
        else:
            x = [x[p] + _mm(pw[p], x[p]) for p in P]
    u2 = [x[p].astype(BF16) for p in P]
    y2 = [ars[p][C2:] + _mm(m_r[p], jnp.concatenate([u2[p], v2[p]], axis=0)) for p in P]
    upd = [_mm_tn(jnp.concatenate([u2[p], v2[p]], axis=0), bk[p]) for p in P]
    for p in P:
        state_ref[p] = (s0[p] + upd[p]) * jnp.exp(cum[p][C - 1:C, :])
    y = [y2[p][:C] + y2[p][C:] for p in P]
    yc = [y[p] - seg_sum(y[p]) * (1.0 / N) for p in P]
    var = [seg_sum(yc[p] * yc[p]) * (1.0 / N) for p in P]
    for p in P:
        yn = yc[p] * lax.rsqrt(var[p] + GN_EPS) * lng_ref[:, sl[p]] + lnb_ref[:, sl[p]]
        o_ref[:, sl[p]] = ((yn + bonus[p]) * g_ref[:, sl[p]]).astype(o_ref.dtype)


def _rw_scan(r, lw, k, v, kk, a, g, r_k, ln_g, ln_b, l, *, npairs):
    T, W = r.shape
    C = RW_CHUNK
    wb = npairs * LANES
    blk = pl.BlockSpec((C, wb), lambda p, c: (c, p))
    vec = pl.BlockSpec((None, 1, wb), lambda p, c: (l, 0, p))
    return pl.pallas_call(
        functools.partial(_rw_scan_kernel, npairs=npairs),
        grid=(W // wb, T // C),
        in_specs=[blk] * 7 + [vec] * 3,
        out_specs=blk,
        out_shape=jax.ShapeDtypeStruct((T, W), BF16),
        scratch_shapes=[pltpu.VMEM((npairs, LANES, LANES), F32)],
        compiler_params=_cparams(("parallel", "arbitrary")),
        name="rw_scan",
    )(r, lw, k, v, kk, a, g, r_k, ln_g, ln_b)


def _out_proj_kernel(x_ref, sb_ref, rw_ref, wa_ref, wb_ref, o_ref):
    o_ref[...] = (x_ref[...]
                  + jnp.dot(sb_ref[...], wa_ref[...], preferred_element_type=F32)
                  + jnp.dot(rw_ref[...], wb_ref[...], preferred_element_type=F32))


def _out_proj(x, sb, rw, w_out, l, *, tm):
    T, D = x.shape
    Wa = sb.shape[1]
    Wb = rw.shape[1]
    assert Wa == Wb
    return pl.pallas_call(
        _out_proj_kernel,
        grid=(T // tm,),
        in_specs=[
            pl.BlockSpec((tm, D), lambda i: (i, 0)),
            pl.BlockSpec((tm, Wa), lambda i: (i, 0)),
            pl.BlockSpec((tm, Wb), lambda i: (i, 0)),
            pl.BlockSpec((None, Wa, D), lambda i: (l, 0, 0)),
            pl.BlockSpec((None, Wb, D), lambda i: (l, 1, 0)),
        ],
        out_specs=pl.BlockSpec((tm, D), lambda i: (i, 0)),
        out_shape=jax.ShapeDtypeStruct((T, D), F32),
        compiler_params=_cparams(("parallel",)),
        name="out_proj",
    )(x, sb, rw, w_out, w_out)


def _router_kernel(x_ref, g_ref, w_ref, b_ref, meta_ref, cnt_ref, pad_hbm, carry_ref, zbuf, zsem, *, tm, fills):
    i = pl.program_id(0)
    n = pl.num_programs(0)
    zrows = zbuf.shape[0]

    def fill_copies(step):
        return [pltpu.make_async_copy(
            zbuf, pad_hbm.at[pl.ds(pl.multiple_of((step * fills + c) * zrows, zrows), zrows), :], zsem)
            for c in range(fills)]

    @pl.when(i == 0)
    def _():
        carry_ref[...] = jnp.zeros_like(carry_ref)
        zbuf[...] = jnp.zeros_like(zbuf)

    @pl.when(i > 0)
    def _():
        for c in fill_copies(i - 1):
            c.wait()

    for c in fill_copies(i):
        c.start()

    hn = _rms(x_ref[...]) * g_ref[...]
    logits = _mm_f32(hn, w_ref[...]) + b_ref[...]
    lane = lax.broadcasted_iota(jnp.int32, logits.shape, 1).astype(F32)
    big = float(LANES)
    neg = -jnp.inf
    is_grp = (lane >= N_EXPERTS) & (lane < N_EXPERTS + N_GROUPS)
    gl = jnp.where(is_grp, logits, neg)
    gmax = jnp.max(gl, axis=-1, keepdims=True)
    g_w = 1.0 / jnp.sum(jnp.where(is_grp, jnp.exp(gl - gmax), 0.0), axis=-1, keepdims=True)
    grp = jnp.min(jnp.where(gl == gmax, lane, big), axis=-1, keepdims=True) - N_EXPERTS
    lo = grp * EXPERTS_PER_GROUP
    in_grp = (lane >= lo) & (lane < lo + EXPERTS_PER_GROUP)
    es = jnp.where(in_grp, logits, neg)
    m1 = jnp.max(es, axis=-1, keepdims=True)
    i1 = jnp.min(jnp.where(es == m1, lane, big), axis=-1, keepdims=True)
    es2 = jnp.where(lane == i1, neg, es)
    m2 = jnp.max(es2, axis=-1, keepdims=True)
    i2 = jnp.min(jnp.where(es2 == m2, lane, big), axis=-1, keepdims=True)
    z = jnp.sum(jnp.where(in_grp, jnp.exp(es - m1), 0.0), axis=-1, keepdims=True)
    p1 = 1.0 / z
    p2 = jnp.exp(m2 - m1) / z
    w1 = g_w * (p1 / (p1 + p2))
    w2 = g_w * (p2 / (p1 + p2))
    oh1 = lane == i1
    oh2 = lane == i2
    oh = (oh1 | oh2).astype(BF16)
    row = lax.broadcasted_iota(jnp.int32, (tm, tm), 0)
    col = lax.broadcasted_iota(jnp.int32, (tm, tm), 1)
    before = jnp.dot((col < row).astype(BF16), oh, preferred_element_type=F32) + carry_ref[0:1, :]
    rank1 = jnp.sum(jnp.where(oh1, before, 0.0), axis=-1, keepdims=True)
    rank2 = jnp.sum(jnp.where(oh2, before, 0.0), axis=-1, keepdims=True)
    carry_ref[...] = carry_ref[...] + jnp.sum(oh.astype(F32), axis=0, keepdims=True)
    meta = jnp.where(lane == 0, i1, 0.0)
    meta = jnp.where(lane == 1, i2, meta)
    meta = jnp.where(lane == 2, w1, meta)
    meta = jnp.where(lane == 3, w2, meta)
    meta = jnp.where(lane == 4, rank1, meta)
    meta = jnp.where(lane == 5, rank2, meta)
    meta_ref[...] = meta
    cnt_ref[...] = carry_ref[...]

    @pl.when(i == n - 1)
    def _():
        for c in fill_copies(i):
            c.wait()


def _router(x, g, w, b, l, *, tm, n_rows):
    T, D = x.shape
    steps = T // tm
    zrows = MOE_ROWS
    fills = n_rows // (steps * zrows)
    assert fills * steps * zrows == n_rows
    return pl.pallas_call(
        functools.partial(_router_kernel, tm=tm, fills=fills),
        grid=(steps,),
        in_specs=[
            pl.BlockSpec((tm, D), lambda i: (i, 0)),
            _layer_vec(l, D),
            _layer_mat(l, (D, LANES)),
            _layer_vec(l, LANES),
        ],
        out_specs=[
            pl.BlockSpec((tm, LANES), lambda i: (i, 0)),
            pl.BlockSpec((SUBLANES, LANES), lambda i: (0, 0)),
            pl.BlockSpec(memory_space=pl.ANY),
        ],
        out_shape=[
            jax.ShapeDtypeStruct((T, LANES), F32),
            jax.ShapeDtypeStruct((SUBLANES, LANES), F32),
            jax.ShapeDtypeStruct((n_rows, D), F32),
        ],
        scratch_shapes=[pltpu.VMEM((SUBLANES, LANES), F32), pltpu.VMEM((zrows, D), F32), pltpu.SemaphoreType.DMA],
        compiler_params=_cparams(("arbitrary",)),
        name="router",
    )(x, g, w, b)


def _gather_rows_start(idx_ref, src_hbm, dst, sem):
    n = dst.shape[0]

    def body(grp, carry):
        base = pl.multiple_of(grp * GATHER_UNROLL, GATHER_UNROLL)
        for u in range(GATHER_UNROLL):
            t = idx_ref[0, 0, base + u]
            pltpu.make_async_copy(src_hbm.at[pl.ds(t, 1), :], dst.at[pl.ds(base + u, 1), :], sem).start(
                priority=u % 2)
        return carry

    lax.fori_loop(0, n // GATHER_UNROLL, body, 0)


def _gather_rows_wait(src_hbm, dst, sem):
    pltpu.make_async_copy(src_hbm.at[pl.ds(0, dst.shape[0]), :], dst, sem).wait()


def _scatter_rows_start(idx_ref, src, dst_hbm, sem):
    n = src.shape[0]

    def body(grp, carry):
        base = pl.multiple_of(grp * GATHER_UNROLL, GATHER_UNROLL)
        for u in range(GATHER_UNROLL):
            t = idx_ref[0, 0, base + u]
            pltpu.make_async_copy(src.at[pl.ds(base + u, 1), :], dst_hbm.at[pl.ds(t, 1), :], sem).start(
                priority=u % 2)
        return carry

    lax.fori_loop(0, n // GATHER_UNROLL, body, 0)


def _scatter_rows_wait(src, dst_hbm, sem):
    pltpu.make_async_copy(src, dst_hbm.at[pl.ds(0, src.shape[0]), :], sem).wait()


def _dispatch_kernel(d0_ref, d1_ref, x_ref, g_ref, zero_hbm, o_hbm, sbuf, sem):
    del zero_hbm
    i = pl.program_id(0)
    n = pl.num_programs(0)
    slot = i % 2
    sbuf[slot] = _rms(x_ref[...]) * g_ref[...]
    _scatter_rows_start(d0_ref, sbuf.at[slot], o_hbm, sem.at[slot])
    _scatter_rows_start(d1_ref, sbuf.at[slot], o_hbm, sem.at[slot])

    def drain(s):
        _scatter_rows_wait(sbuf.at[s], o_hbm, sem.at[s])
        _scatter_rows_wait(sbuf.at[s], o_hbm, sem.at[s])

    @pl.when(i > 0)
    def _():
        drain(1 - slot)

    @pl.when(i == n - 1)
    def _():
        drain(slot)


def _dispatch(x, g, dest0, dest1, zeros, l):
    T, D = x.shape
    n_rows = zeros.shape[0]
    nt, _, tm = dest0.shape
    cur = pl.BlockSpec((1, 1, tm), lambda i: (i, 0, 0), memory_space=pltpu.SMEM)
    return pl.pallas_call(
        _dispatch_kernel,
        grid=(nt,),
        in_specs=[cur, cur, pl.BlockSpec((tm, D), lambda i: (i, 0)), _layer_vec(l, D),
                  pl.BlockSpec(memory_space=pl.ANY)],
        out_specs=pl.BlockSpec(memory_space=pl.ANY),
        out_shape=jax.ShapeDtypeStruct((n_rows, D), F32),
        scratch_shapes=[pltpu.VMEM((2, tm, D), F32), pltpu.SemaphoreType.DMA((2,))],
        input_output_aliases={4: 0},
        compiler_params=_cparams(("arbitrary",)),
        name="dispatch",
    )(dest0, dest1, x, g, zeros)


def _expert_kernel(be_ref, nused_ref, x_ref, wu_ref, wd_ref, o_ref):
    del be_ref
    b = pl.program_id(0)
    n_used = nused_ref[0]

    @pl.when(b < n_used)
    def _():
        h = jnp.dot(x_ref[...].astype(BF16), wu_ref[...].astype(BF16), preferred_element_type=F32)
        ff = h.shape[1] // 2
        gate = h[:, :ff]
        up = h[:, ff:]
        act = (gate * jax.nn.sigmoid(gate) * up).astype(BF16)
        o_ref[...] = jnp.dot(act, wd_ref[...].astype(BF16), preferred_element_type=F32).astype(o_ref.dtype)

    @pl.when(b >= n_used)
    def _():
        o_ref[...] = jnp.zeros_like(o_ref)


def _expert_mlp(x_pad, blk_expert, n_used, w_up, w_down, l):
    R, D = x_pad.shape
    bm = MOE_ROWS
    F2 = w_up.shape[3]
    grid_spec = pltpu.PrefetchScalarGridSpec(
        num_scalar_prefetch=2,
        grid=(R // bm,),
        in_specs=[
            pl.BlockSpec((bm, D), lambda b, be, nu: (b, 0)),
            pl.BlockSpec((None, None, D, F2), lambda b, be, nu: (l, be[b], 0, 0)),
            pl.BlockSpec((None, None, F2 // 2, D), lambda b, be, nu: (l, be[b], 0, 0)),
        ],
        out_specs=pl.BlockSpec((bm, D), lambda b, be, nu: (b, 0)),
    )
    return pl.pallas_call(
        _expert_kernel,
        grid_spec=grid_spec,
        out_shape=jax.ShapeDtypeStruct((R, D), F32),
        compiler_params=_cparams(("arbitrary",)),
        name="expert_mlp",
    )(blk_expert, n_used, x_pad, w_up, w_down)


def _combine_ple_kernel(d0_ref, d1_ref, d0n_ref, d1n_ref, y_hbm, x_ref, meta_ref, p_ref, wg_ref, wp_ref, g_ref,
                        o_ref, ybuf, sem):
    i = pl.program_id(0)
    n = pl.num_programs(0)
    slot = i % 2

    def start(s, a_ref, b_ref):
        _gather_rows_start(a_ref, y_hbm, ybuf.at[s, 0], sem.at[s])
        _gather_rows_start(b_ref, y_hbm, ybuf.at[s, 1], sem.at[s])

    @pl.when(i == 0)
    def _():
        start(0, d0_ref, d1_ref)

    @pl.when(i + 1 < n)
    def _():
        start(1 - slot, d0n_ref, d1n_ref)

    _gather_rows_wait(y_hbm, ybuf.at[slot, 0], sem.at[slot])
    _gather_rows_wait(y_hbm, ybuf.at[slot, 1], sem.at[slot])
    meta = meta_ref[...]
    x = x_ref[...] + meta[:, 2:3] * ybuf[slot, 0] + meta[:, 3:4] * ybuf[slot, 1]
    gate = jax.nn.sigmoid(jnp.dot(_rms(x).astype(BF16), wg_ref[...], preferred_element_type=F32))
    pe = jnp.dot(p_ref[...].astype(BF16), wp_ref[...], preferred_element_type=F32)
    o_ref[...] = x + _rms(pe * gate) * g_ref[...]


def _combine_ple(x, y_pad, dest0, dest1, meta, p, w_gate, w_proj, g, l):
    T, D = x.shape
    P = p.shape[2]
    nt, _, tm = dest0.shape
    row = lambda w: pl.BlockSpec((tm, w), lambda i: (i, 0))
    cur = pl.BlockSpec((1, 1, tm), lambda i: (i, 0, 0), memory_space=pltpu.SMEM)
    nxt = pl.BlockSpec((1, 1, tm), lambda i: (jnp.minimum(i + 1, nt - 1), 0, 0), memory_space=pltpu.SMEM)
    return pl.pallas_call(
        _combine_ple_kernel,
        grid=(nt,),
        in_specs=[cur, cur, nxt, nxt, pl.BlockSpec(memory_space=pl.ANY),
                  row(D), row(LANES),
                  pl.BlockSpec((None, tm, P), lambda i: (l, i, 0)),
                  _layer_mat(l, (D, D)), _layer_mat(l, (P, D)), _layer_vec(l, D)],
        out_specs=row(D),
        out_shape=jax.ShapeDtypeStruct((T, D), F32),
        scratch_shapes=[pltpu.VMEM((2, 2, tm, D), F32), pltpu.SemaphoreType.DMA((2,))],
        compiler_params=_cparams(("arbitrary",)),
        name="combine_ple",
    )(dest0, dest1, dest0, dest1, y_pad, x, meta, p, w_gate, w_proj, g)


def _moe_layout(meta, counts, n_rows):
    bm = MOE_ROWS
    nb = n_rows // bm
    e = meta[:, 0:2].astype(jnp.int32)
    rank = meta[:, 4:6].astype(jnp.int32)
    cnt = counts[0, :N_EXPERTS].astype(jnp.int32)
    padded = ((cnt + bm - 1) // bm) * bm
    pad_ends = jnp.cumsum(padded)
    pad_starts = pad_ends - padded
    start_of = jnp.sum(jnp.where(e[..., None] == jnp.arange(N_EXPERTS, dtype=jnp.int32), pad_starts, 0), axis=-1)
    dest = start_of + rank
    blk_start = jnp.arange(nb, dtype=jnp.int32) * bm
    blk_expert = jnp.minimum(jnp.sum(pad_ends[None, :] <= blk_start[:, None], axis=1), N_EXPERTS - 1)
    n_used = (pad_ends[-1:] // bm).astype(jnp.int32)
    return dest, blk_expert.astype(jnp.int32), n_used


def kernel(x, p, ln1_g, w_in, sb_q_g, sb_k_g, sb_out_g, rw_mu, rw_w0, rw_w2, rw_a0, rw_a2, rw_g2, rw_k_k, rw_k_a, rw_r_k, rw_ln_g, rw_ln_b, w_out, ln2_g, router_g, router_g_b, router_e, router_e_b, w_up, w_down, ple_proj, ple_gate, ple_norm_g):
    B, S, D = x.shape
    depth = w_in.shape[0]
    T = B * S
    assert B == 1, "token shift and attention assume one sequence"
    sb_width = D // 2
    rw_width = D - sb_width
    n_sb_heads = sb_width // SB_HEAD_DIM
    lora = LORA_W + LORA_A + LORA_G
    col0 = 3 * sb_width
    W = rw_width
    tm = min(TOKEN_TILE, T)
    tw = min(WIDE_TOKEN_TILE, T)
    n_rows = 2 * T + N_EXPERTS * MOE_ROWS
    vec = lambda a: a.reshape(depth, 1, -1)

    w_in_b = jnp.pad(w_in, ((0, 0), (0, 0), (0, LORA_PAD - lora))).astype(BF16)
    w_out_b = w_out.astype(BF16)
    ple_gate_b = ple_gate.astype(BF16)
    ple_proj_b = ple_proj.astype(BF16)
    mu = vec(rw_mu)
    mu_r, mu_k, mu_v = mu[:, :, :W], mu[:, :, W:2 * W], mu[:, :, 2 * W:3 * W]
    mu_lo = jnp.pad(mu[:, :, 3 * W:], ((0, 0), (0, 0), (0, LORA_PAD - lora)))
    w2 = jnp.pad(rw_w2, ((0, 0), (0, LORA_A), (0, 0)))
    a2 = jnp.pad(rw_a2, ((0, 0), (LORA_W, 0), (0, 0)))
    g2 = jnp.pad(rw_g2, ((0, 0), (0, 2 * LANES - LORA_G), (0, 0)))
    w_r = jnp.pad(jnp.concatenate([router_e, router_g], axis=2),
                  ((0, 0), (0, 0), (0, LANES - N_EXPERTS - N_GROUPS)))
    b_r = vec(jnp.pad(jnp.concatenate([router_e_b, router_g_b], axis=1),
                      ((0, 0), (0, LANES - N_EXPERTS - N_GROUPS))))
    p3 = p.reshape(depth, T, -1)

    xf = x.reshape(T, D)
    for l in range(depth):
        proj = _norm_matmul(xf, vec(ln1_g), w_in_b, l, tm=min(1024, T), tn=LORA_PAD, out_dtype=F32, name="in_proj")
        sb = _sb_attention(proj, vec(sb_q_g), vec(sb_k_g), vec(sb_out_g), l, n_heads=n_sb_heads, tq=256,
                           nsub=min(4, T // 256))
        r, lw, k, v, kk, a, g = _rw_prep(
            proj, mu_r, mu_k, mu_v, mu_lo, vec(rw_w0), w2, vec(rw_a0), a2, g2, vec(rw_k_k), vec(rw_k_a), l,
            rw_width=rw_width, col0=col0, tm=tm)
        rw = _rw_scan(r, lw, k, v, kk, a, g, vec(rw_r_k), vec(rw_ln_g), vec(rw_ln_b), l, npairs=8)
        x1 = _out_proj(xf, sb, rw, w_out_b, l, tm=tw)
        meta, counts, zeros = _router(x1, vec(ln2_g), w_r, b_r, l, tm=tw, n_rows=n_rows)
        dest, blk_expert, n_used = _moe_layout(meta, counts, n_rows)
        tiled = lambda t: (dest[:, 0].reshape(T // t, 1, t), dest[:, 1].reshape(T // t, 1, t))
        x_pad = _dispatch(x1, vec(ln2_g), *tiled(tw), zeros, l)
        y_pad = _expert_mlp(x_pad, blk_expert, n_used, w_up, w_down, l)
        xf = _combine_ple(x1, y_pad, *tiled(tm), meta, p3, ple_gate_b, ple_proj_b, vec(ple_norm_g), l)
    return xf.reshape(B, S, D)
```
